```python
import jax, jax.numpy as jnp
from jax import lax
import numpy as np

D_MODEL = 1024
BATCH = 32
SEQ = 256
DEPTH = 2
DEC_BATCH = 4
DEC_SEQ = 2048
PAST_LEN = 512

GRID_W = 64
HEAD_DIM = 64
FOURIER_GROUPS = 4
FOURIER_GROUP_W = 64
FOURIER_W = FOURIER_GROUPS * FOURIER_GROUP_W
RET_HEADS = 4
RET_DK = 64
RET_DV = 64
RET_QK_W = RET_HEADS * RET_DK
RET_W = RET_HEADS * RET_DV
RET_CHUNK = 128
ATT_Q_HEADS = 8
ATT_KV_HEADS = 2
ATT_GROUP = ATT_Q_HEADS // ATT_KV_HEADS
ATT_W = ATT_Q_HEADS * HEAD_DIM
ATT_KV_W = ATT_KV_HEADS * HEAD_DIM
WINDOW = 128
ATT_BLOCK = 128
N_BRANCH = 3
ROPE_BASE = 10000.0
EPS = 1e-6
SPLITS = (FOURIER_W, FOURIER_W, RET_QK_W, RET_QK_W, RET_W, RET_W,
          ATT_W, ATT_KV_W, ATT_KV_W, ATT_W, N_BRANCH * D_MODEL)
IN_W = sum(SPLITS)

kernel_name = 'hybrid_fourier_retention_swa_diffusion_step'

F32 = jnp.float32


def rms_norm(x, g):
    xf = x.astype(F32)
    y = xf * lax.rsqrt(jnp.mean(xf * xf, axis=-1, keepdims=True) + EPS)
    return (y * g.astype(F32)).astype(x.dtype)


def split_columns(u):
    parts = []
    start = 0
    for w in SPLITS:
        parts.append(u[..., start:start + w])
        start += w
    return parts


def axial_rope(x):
    n = x.shape[1]
    rows = n // GRID_W
    row = jnp.repeat(jnp.arange(rows), GRID_W)
    col = jnp.tile(jnp.arange(GRID_W), rows)
    quarter = HEAD_DIM // 4
    half = HEAD_DIM // 2
    inv = ROPE_BASE ** (-jnp.arange(quarter, dtype=F32) / quarter)

    def rot(xp, pos):
        ang = pos.astype(F32)[:, None] * inv[None, :]
        cos = jnp.cos(ang)[None, :, None, :]
        sin = jnp.sin(ang)[None, :, None, :]
        x1, x2 = xp[..., :quarter], xp[..., quarter:]
        return jnp.concatenate([x1 * cos - x2 * sin, x1 * sin + x2 * cos], axis=-1)

    xf = x.astype(F32)
    return jnp.concatenate([rot(xf[..., :half], row), rot(xf[..., half:], col)], axis=-1).astype(x.dtype)


def fourier_mix(f):
    b, t, _ = f.shape
    fg = f.astype(F32).reshape(b, t, FOURIER_GROUPS, FOURIER_GROUP_W)
    return jnp.fft.fftn(fg, axes=(1, 3), norm='ortho').real.reshape(b, t, FOURIER_W).astype(f.dtype)


def retention_scan(q, k, v, log_gamma, s0):
    b, t, h, _ = q.shape
    dv = v.shape[-1]
    nc = t // RET_CHUNK

    def chunks(a):
        return jnp.moveaxis(a.astype(F32).reshape(b, nc, RET_CHUNK, h, a.shape[-1]), 1, 0)

    i = jnp.arange(RET_CHUNK, dtype=F32)
    diff = i[:, None] - i[None, :]
    intra = jnp.where(diff[None] >= 0,
                      jnp.exp(jnp.maximum(diff, 0.0)[None] * log_gamma[:, None, None]), 0.0)
    read = jnp.exp((i[:, None] + 1.0) * log_gamma[None, :])
    write = jnp.exp((RET_CHUNK - 1.0 - i)[:, None] * log_gamma[None, :])
    carry_decay = jnp.exp(RET_CHUNK * log_gamma)

    def step(s, qkv):
        qc, kc, vc = qkv
        att = jnp.einsum('bihd,bjhd->bhij', qc, kc) * intra
        o = (jnp.einsum('bhij,bjhe->bihe', att, vc)
             + jnp.einsum('bihd,bhde->bihe', qc, s) * read[None, :, :, None])
        s = carry_decay[None, :, None, None] * s + jnp.einsum(
            'bjhd,bjhe->bhde', kc * write[None, :, :, None], vc)
        return s, o

    s_fin, o = lax.scan(step, s0.astype(F32), (chunks(q), chunks(k), chunks(v)))
    return jnp.moveaxis(o, 0, 1).reshape(b, t, h, dv), s_fin


def bidir_retention(q, k, v, ret_logit, s0_f, s0_b):
    lg = jax.nn.log_sigmoid(ret_logit.astype(F32))
    of, sf = retention_scan(q, k, v, lg[0], s0_f)
    ob, sb = retention_scan(q[:, ::-1], k[:, ::-1], v[:, ::-1], lg[1], s0_b)
    return of + ob[:, ::-1], jnp.stack([sf, sb], axis=1)


def softmax_with_sink(logits, sink):
    s = jnp.broadcast_to(sink.astype(F32)[:, :, None, None], logits.shape[:-1] + (1,))
    p = jax.nn.softmax(jnp.concatenate([logits, s], axis=-1), axis=-1)
    return p[..., :-1]


def context_attention(q, k, v, sink):
    b, l = q.shape[:2]
    nb = l // ATT_BLOCK
    scale = HEAD_DIM ** -0.5
    qb = jnp.moveaxis(q.reshape(b, nb, ATT_BLOCK, ATT_KV_HEADS, ATT_GROUP, HEAD_DIM), 1, 0)
    kf = k.astype(F32)
    vf = v.astype(F32)

    def one(qblk):
        logits = jnp.einsum('bqkgd,bskd->bkgqs', qblk.astype(F32), kf) * scale
        p = softmax_with_sink(logits, sink)
        return jnp.einsum('bkgqs,bskd->bqkgd', p, vf)

    o = lax.map(one, qb)
    return jnp.moveaxis(o, 0, 1).reshape(b, l, ATT_W).astype(q.dtype)


def latent_attention(q, k, v, k_ctx, v_ctx, sink):
    b, n = q.shape[:2]
    blk = ATT_BLOCK
    nb = n // blk
    scale = HEAD_DIM ** -0.5
    qb = q.astype(F32).reshape(b, nb, blk, ATT_KV_HEADS, ATT_GROUP, HEAD_DIM)
    pad = ((0, 0), (blk, blk), (0, 0), (0, 0))
    kp = jnp.pad(k.astype(F32), pad)
    vp = jnp.pad(v.astype(F32), pad)
    band = jnp.arange(nb)[:, None] * blk + jnp.arange(3 * blk)[None, :]
    kb = kp[:, band]
    vb = vp[:, band]
    qpos = jnp.arange(nb)[:, None] * blk + jnp.arange(blk)[None, :]
    kpos = band - blk
    valid = ((jnp.abs(qpos[:, :, None] - kpos[:, None, :]) <= WINDOW)
             & (kpos[:, None, :] >= 0) & (kpos[:, None, :] < n))
    loc = jnp.einsum('bnqkgd,bnskd->bnkgqs', qb, kb) * scale
    loc = jnp.where(valid[None, :, None, None], loc, -jnp.inf)
    ctx = jnp.einsum('bnqkgd,blkd->bnkgql', qb, k_ctx.astype(F32)) * scale
    p = softmax_with_sink(jnp.concatenate([loc, ctx], axis=-1), sink)
    o = (jnp.einsum('bnkgqs,bnskd->bnqkgd', p[..., :3 * blk], vb)
         + jnp.einsum('bnkgql,blkd->bnqkgd', p[..., 3 * blk:], v_ctx.astype(F32)))
    return o.reshape(b, n, ATT_W).astype(q.dtype)


def trunk_layer(x, cvec, w_mod, b_mod, g_pre, g_post, w_in, w_four, ret_logit, ret_gn,
                attn_sink, w_pa, w_pb, w_pc, w_out, k_ctx=None, v_ctx=None, s_ctx=None):
    b, t, _ = x.shape
    is_latent = k_ctx is not None
    mod = (jax.nn.silu(cvec) @ w_mod + b_mod).reshape(-1, 1, 3 * D_MODEL)
    shift, scale, gate = jnp.split(mod, 3, axis=-1)
    h = rms_norm(x, g_pre) * (1.0 + scale) + shift
    fx, fz, rq, rk, rv, rz, aq, ak, av, az, mg = split_columns(h @ w_in)

    ya = (fourier_mix(fx) @ w_four) * jax.nn.silu(fz)

    rq = rq.reshape(b, t, RET_HEADS, RET_DK)
    rk = rk.reshape(b, t, RET_HEADS, RET_DK) * (RET_DK ** -0.5)
    rv = rv.reshape(b, t, RET_HEADS, RET_DV)
    aq = aq.reshape(b, t, ATT_Q_HEADS, HEAD_DIM)
    ak = ak.reshape(b, t, ATT_KV_HEADS, HEAD_DIM)
    av = av.reshape(b, t, ATT_KV_HEADS, HEAD_DIM)
    if is_latent:
        rq, rk = axial_rope(rq), axial_rope(rk)
        aq, ak = axial_rope(aq), axial_rope(ak)
        s0f, s0b = s_ctx[:, 0], s_ctx[:, 1]
    else:
        s0f = s0b = jnp.zeros((b, RET_HEADS, RET_DK, RET_DV), F32)
    ro, s_fin = bidir_retention(rq, rk, rv, ret_logit, s0f, s0b)
    ro = rms_norm(ro, ret_gn.reshape(RET_HEADS, RET_DV)).reshape(b, t, RET_W)
    yb = ro.astype(x.dtype) * jax.nn.silu(rz)

    if is_latent:
        ao = latent_attention(aq, ak, av, k_ctx, v_ctx, attn_sink)
    else:
        ao = context_attention(aq, ak, av, attn_sink)
    yc = ao * jax.nn.silu(az)

    ga, gb, gc = jnp.split(jax.nn.sigmoid(mg), N_BRANCH, axis=-1)
    merged = ga * (ya @ w_pa) + gb * (yb @ w_pb) + gc * (yc @ w_pc)
    out = merged @ w_out
    x = x + gate * rms_norm(out, g_post)
    return x, ak, av, s_fin


def setup_inputs(seed: int = 0) -> dict:
    key = jax.random.key(seed)
    ks = jax.random.split(key, 24)

    def nrm(k, shape, s):
        return jax.random.normal(k, shape, F32) * s

    d = D_MODEL
    base_logit = jnp.asarray(np.log(2.0 ** (5 + np.arange(RET_HEADS)) - 1.0), F32)
    return {
        'x_prompt': nrm(ks[0], (BATCH, SEQ, d), 1.0),
        'x_sample': nrm(ks[1], (DEC_BATCH, DEC_SEQ, d), 1.0),
        'cache_k': nrm(ks[2], (DEC_BATCH, DEPTH, PAST_LEN, ATT_KV_HEADS, HEAD_DIM), 1.0),
        'cache_v': nrm(ks[3], (DEC_BATCH, DEPTH, PAST_LEN, ATT_KV_HEADS, HEAD_DIM), 1.0),
        'state_ret': nrm(ks[4], (DEC_BATCH, DEPTH, 2, RET_HEADS, RET_DK, RET_DV), 1.0),
        'c': nrm(ks[5], (DEC_BATCH, d), 1.0),
        'c_ctx': nrm(ks[6], (d,), 1.0),
        'w_mod': nrm(ks[7], (DEPTH, d, 3 * d), 0.5 * d ** -0.5),
        'b_mod': nrm(ks[8], (DEPTH, 3 * d), 0.02),
        'g_pre': 1.0 + nrm(ks[9], (DEPTH, d), 0.02),
        'g_post': 1.0 + nrm(ks[10], (DEPTH, d), 0.02),
        'w_in': nrm(ks[11], (DEPTH, d, IN_W), d ** -0.5),
        'w_four': nrm(ks[12], (DEPTH, FOURIER_W, FOURIER_W), FOURIER_W ** -0.5),
        'ret_decay': base_logit[None, None, :] + nrm(ks[13], (DEPTH, 2, RET_HEADS), 0.1),
        'ret_gn': 1.0 + nrm(ks[14], (DEPTH, RET_W), 0.02),
        'attn_sink': nrm(ks[15], (DEPTH, ATT_KV_HEADS, ATT_GROUP), 0.5),
        'w_branch_a': nrm(ks[16], (DEPTH, FOURIER_W, d), FOURIER_W ** -0.5),
        'w_branch_b': nrm(ks[17], (DEPTH, RET_W, d), RET_W ** -0.5),
        'w_branch_c': nrm(ks[18], (DEPTH, ATT_W, d), ATT_W ** -0.5),
        'w_out': nrm(ks[19], (DEPTH, d, d), d ** -0.5),
    }


def reference(x_prompt, x_sample, cache_k, cache_v, state_ret, c, c_ctx, w_mod, b_mod,
              g_pre, g_post, w_in, w_four, ret_decay, ret_gn, attn_sink,
              w_branch_a, w_branch_b, w_branch_c, w_out):
    xp = x_prompt
    ks_new, vs_new, ss_new = [], [], []
    for l in range(DEPTH):
        xp, k_l, v_l, s_l = trunk_layer(
            xp, c_ctx, w_mod[l], b_mod[l], g_pre[l], g_post[l], w_in[l], w_four[l],
            ret_decay[l], ret_gn[l], attn_sink[l], w_branch_a[l], w_branch_b[l],
            w_branch_c[l], w_out[l])
        ks_new.append(k_l)
        vs_new.append(v_l)
        ss_new.append(s_l)
    new_cache_k = jnp.stack(ks_new, axis=1)
    new_cache_v = jnp.stack(vs_new, axis=1)
    new_state_ret = jnp.stack(ss_new, axis=1)

    xs = x_sample
    for l in range(DEPTH):
        xs = trunk_layer(
            xs, c, w_mod[l], b_mod[l], g_pre[l], g_post[l], w_in[l], w_four[l],
            ret_decay[l], ret_gn[l], attn_sink[l], w_branch_a[l], w_branch_b[l],
            w_branch_c[l], w_out[l],
            k_ctx=cache_k[:, l], v_ctx=cache_v[:, l], s_ctx=state_ret[:, l])[0]

    return (xp, xs, new_cache_k, new_cache_v, new_state_ret)
```

```python
import functools

import numpy as np
import jax
import jax.numpy as jnp
from jax import lax
from jax.experimental import pallas as pl
from jax.experimental.pallas import tpu as pltpu

F32 = jnp.float32
BF16 = jnp.bfloat16

D_MODEL = 1024
DEPTH = 2
GRID_W = 64
HEAD_DIM = 64
FOURIER_GROUPS = 4
FOURIER_GROUP_W = 64
FOURIER_W = FOURIER_GROUPS * FOURIER_GROUP_W
RET_HEADS = 4
RET_DK = 64
RET_W = RET_HEADS * RET_DK
RET_CHUNK = 128
ATT_Q_HEADS = 8
ATT_KV_HEADS = 2
ATT_W = ATT_Q_HEADS * HEAD_DIM
ATT_KV_W = ATT_KV_HEADS * HEAD_DIM
WINDOW = 128
ROPE_BASE = 10000.0
EPS = 1e-6
MOD_ROWS = 8
LANES = 128
NEG = -1e30
VMEM_LIMIT = 56 * 1024 * 1024

C_FX, C_FZ, C_RQ, C_RZ_END = 0, 256, 512, 1536
C_AQ, C_AK, C_AZ, C_MG, C_END = 1536, 2048, 2304, 2816, 5888


def _silu(x):
    return x * jax.nn.sigmoid(x)


def _params(sem):
    return pltpu.CompilerParams(dimension_semantics=sem, vmem_limit_bytes=VMEM_LIMIT)


def _const_spec(shape):
    nd = len(shape)
    return pl.BlockSpec(shape, lambda *_: (0,) * nd, pipeline_mode=pl.Buffered(1))


def _dft_tables(t):
    c = np.arange(FOURIER_GROUP_W)
    ang = 2.0 * np.pi * ((c[:, None] * c[None, :]) % FOURIER_GROUP_W) / FOURIER_GROUP_W
    eye = np.eye(FOURIER_GROUPS)
    s64 = FOURIER_GROUP_W ** -0.5
    chan = np.concatenate([np.kron(eye, np.cos(ang) * s64), np.kron(eye, np.sin(ang) * s64)], axis=1)
    p = np.arange(t)
    angt = 2.0 * np.pi * ((p[:, None] * p[None, :]) % t) / t
    pos = np.concatenate([np.cos(angt), -np.sin(angt)], axis=1) * (t ** -0.5)
    return jnp.asarray(chan, F32).astype(BF16), jnp.asarray(pos, F32).astype(BF16)


def _rope_tables(t):
    quarter = HEAD_DIM // 4
    lane = np.arange(LANES) % HEAD_DIM
    inv = ROPE_BASE ** (-(lane % quarter).astype(np.float64) / quarter)
    n = np.arange(t)
    pos = np.where(lane[None, :] < HEAD_DIM // 2, (n // GRID_W)[:, None], (n % GRID_W)[:, None])
    ang = pos.astype(np.float64) * inv[None, :]
    sign = np.where((lane % (2 * quarter)) < quarter, -1.0, 1.0)
    return jnp.asarray(np.cos(ang), F32), jnp.asarray(np.sin(ang) * sign[None, :], F32)


def _rope(x, cos, sin):
    lane = lax.broadcasted_iota(jnp.int32, x.shape, 1)
    first = (lane & 31) < 16
    partner = jnp.where(first, pltpu.roll(x, LANES - 16, axis=1), pltpu.roll(x, 16, axis=1))
    return x * cos + partner * sin


def _mod_kernel(cv_ref, w_ref, b_ref, o_ref):
    a = _silu(cv_ref[...])
    o_ref[...] = jnp.dot(a, w_ref[...], preferred_element_type=F32,
                         precision=lax.Precision.HIGHEST) + b_ref[...]


def _modulation(cv, w_mod, b_mod):
    tn = 1024
    return pl.pallas_call(
        _mod_kernel,
        grid=(DEPTH, 3 * D_MODEL // tn),
        in_specs=[pl.BlockSpec((MOD_ROWS, D_MODEL), lambda l, j: (0, 0)),
                  pl.BlockSpec((None, D_MODEL, tn), lambda l, j: (l, 0, j)),
                  pl.BlockSpec((None, 1, tn), lambda l, j: (l, 0, j))],
        out_specs=pl.BlockSpec((None, MOD_ROWS, tn), lambda l, j: (l, 0, j)),
        out_shape=jax.ShapeDtypeStruct((DEPTH, MOD_ROWS, 3 * D_MODEL), F32),
        compiler_params=_params(("parallel", "parallel")),
        name="modulation",
    )(cv, w_mod, b_mod.reshape(DEPTH, 1, 3 * D_MODEL))


def _mod_row(i, tm, row0, rows_per_mod):
    return row0 + (i * tm) // rows_per_mod


def _inproj_kernel(x_ref, mod_ref, g_ref, w_ref, dft_ref,
                   xcs_ref, fz_ref, ur_ref, ua_ref, mg_ref, kv_ref, *, tm, row0, rows_per_mod):
    row = _mod_row(pl.program_id(0), tm, row0, rows_per_mod)
    x = x_ref[...]
    ms = jnp.mean(x * x, axis=-1, keepdims=True)
    y = x * lax.rsqrt(ms + EPS) * g_ref[...]
    shift = mod_ref[pl.ds(row, 1), 0:D_MODEL]
    scale = mod_ref[pl.ds(row, 1), D_MODEL:2 * D_MODEL]
    h = (y * (1.0 + scale) + shift).astype(BF16)

    def mm(c0, c1):
        return jnp.dot(h, w_ref[:, c0:c1], preferred_element_type=F32)

    f = mm(C_FX, C_RQ)
    xcs_ref[...] = jnp.dot(f[:, :FOURIER_W].astype(BF16), dft_ref[...],
                           preferred_element_type=F32).astype(BF16)
    fz_ref[...] = f[:, FOURIER_W:].astype(BF16)
    ur_ref[...] = mm(C_RQ, C_RZ_END).astype(BF16)
    ua_ref[:, 0:ATT_W] = mm(C_AQ, C_AK).astype(BF16)
    ua_ref[:, ATT_W:2 * ATT_W] = mm(C_AZ, C_MG).astype(BF16)
    kv = mm(C_AK, C_AZ)
    kv_ref[...] = kv
    ua_ref[:, 2 * ATT_W:] = kv.astype(BF16)
    for c in range(3):
        mg_ref[:, c * D_MODEL:(c + 1) * D_MODEL] = mm(
            C_MG + c * D_MODEL, C_MG + (c + 1) * D_MODEL).astype(BF16)


def _inproj(x2, mod_l, g_pre, w_in, dft, *, row0, rows_per_mod):
    n = x2.shape[0]
    tm = 512
    row_spec = lambda w: pl.BlockSpec((tm, w), lambda i: (i, 0))
    widths = (2 * FOURIER_W, FOURIER_W, 4 * RET_W, 2 * ATT_W + 2 * ATT_KV_W, 3 * D_MODEL)
    out_shape = [jax.ShapeDtypeStruct((n, w), BF16) for w in widths]
    out_shape.append(jax.ShapeDtypeStruct((n, 2 * ATT_KV_W), F32))
    return pl.pallas_call(
        functools.partial(_inproj_kernel, tm=tm, row0=row0, rows_per_mod=rows_per_mod),
        grid=(n // tm,),
        in_specs=[row_spec(D_MODEL),
                  _const_spec((MOD_ROWS, 3 * D_MODEL)),
                  _const_spec((1, D_MODEL)),
                  _const_spec((D_MODEL, C_END)),
                  _const_spec((FOURIER_W, 2 * FOURIER_W))],
        out_specs=[row_spec(w) for w in widths] + [row_spec(2 * ATT_KV_W)],
        out_shape=out_shape,
        compiler_params=_params(("parallel",)),
        name="inproj",
    )(x2, mod_l, g_pre.reshape(1, D_MODEL), w_in, dft)


def _fourier_kernel(ct_ref, xcs_ref, fz_ref, w_ref, ya_ref, *, bg, t):
    for b in range(bg):
        yr = (jnp.dot(ct_ref[:, 0:t], xcs_ref[b, :, 0:FOURIER_W], preferred_element_type=F32)
              + jnp.dot(ct_ref[:, t:2 * t], xcs_ref[b, :, FOURIER_W:], preferred_element_type=F32))
        ya = jnp.dot(yr.astype(BF16), w_ref[...], preferred_element_type=F32)
        ya_ref[b] = (ya * _silu(fz_ref[b].astype(F32))).astype(BF16)


def _fourier(xcs, fz, ct, w_four, *, bg, tq):
    b, t, _ = xcs.shape
    return pl.pallas_call(
        functools.partial(_fourier_kernel, bg=bg, t=t),
        grid=(b // bg, t // tq),
        in_specs=[pl.BlockSpec((tq, 2 * t), lambda i, j: (j, 0)),
                  pl.BlockSpec((bg, t, 2 * FOURIER_W), lambda i, j: (i, 0, 0)),
                  pl.BlockSpec((bg, tq, FOURIER_W), lambda i, j: (i, j, 0)),
                  _const_spec((FOURIER_W, FOURIER_W))],
        out_specs=pl.BlockSpec((bg, tq, FOURIER_W), lambda i, j: (i, j, 0)),
        out_shape=jax.ShapeDtypeStruct((b, t, FOURIER_W), BF16),
        compiler_params=_params(("parallel", "parallel")),
        name="fourier",
    )(ct, xcs, fz, w_four)


def _log_sigmoid(x):
    return jnp.minimum(x, 0.0) - jnp.log(1.0 + jnp.exp(-jnp.abs(x)))


def _head_blocks(a, width):
    lane = lax.broadcasted_iota(jnp.int32, a.shape, 1)
    zero = jnp.zeros_like(a)
    return jnp.concatenate(
        [jnp.where((lane >= h * width) & (lane < (h + 1) * width), a, zero) for h in range(RET_HEADS)],
        axis=0)


def _ret_kernel(*refs, t, rope, has_s0):
    refs = list(refs)
    ur_ref = refs.pop(0)
    cos_ref = refs.pop(0) if rope else None
    sin_ref = refs.pop(0) if rope else None
    s0_ref = refs.pop(0) if has_s0 else None
    dec_qk_ref, dec_att_ref, gn_ref, yb_ref, sfin_ref, sf_ref, sb_ref, sbs_ref = refs
    c = RET_CHUNK
    nc = t // c

    lgf = _log_sigmoid(dec_qk_ref[0:1, :])
    lgb = _log_sigmoid(dec_qk_ref[1:2, :])
    ri = lax.broadcasted_iota(jnp.int32, (c, RET_W), 0).astype(F32)
    read_f = jnp.exp((ri + 1.0) * lgf)
    read_b = jnp.exp((c - ri) * lgb)
    write_f = jnp.exp((c - 1.0 - ri) * lgf)
    write_b = jnp.exp(ri * lgb)
    carry_f = jnp.exp(c * lgf)
    carry_b = jnp.exp(c * lgb)
    lgf_a = _log_sigmoid(dec_att_ref[0:1, :])
    lgb_a = _log_sigmoid(dec_att_ref[1:2, :])
    ii = lax.broadcasted_iota(jnp.int32, (c, RET_HEADS * c), 0)
    jj = lax.broadcasted_iota(jnp.int32, (c, RET_HEADS * c), 1) & (c - 1)
    diff = (ii - jj).astype(F32)
    decay = (jnp.where(diff >= 0, jnp.exp(jnp.maximum(diff, 0.0) * lgf_a), 0.0)
             + jnp.where(diff <= 0, jnp.exp(jnp.maximum(-diff, 0.0) * lgb_a), 0.0))
    r2 = lax.broadcasted_iota(jnp.int32, (RET_W, RET_W), 0)
    c2 = lax.broadcasted_iota(jnp.int32, (RET_W, RET_W), 1)
    same_head = (r2 // RET_DK) == (c2 // RET_DK)
    group_mean = jnp.where(same_head, 1.0 / RET_DK, 0.0).astype(BF16)

    def load_qk(col, r0):
        a = ur_ref[pl.ds(r0, c), col:col + RET_W].astype(F32)
        if rope:
            cs = cos_ref[pl.ds(r0, c), :]
            sn = sin_ref[pl.ds(r0, c), :]
            a = jnp.concatenate([_rope(a[:, :LANES], cs, sn), _rope(a[:, LANES:], cs, sn)], axis=1)
        return a

    def state_delta(k, v, write):
        kw = (k * write).astype(BF16)
        d = lax.dot_general(kw, v, (((0,), (0,)), ((), ())), preferred_element_type=F32)
        return jnp.where(same_head, d, 0.0)

    if has_s0:
        sf_ref[...] = s0_ref[0]
        sb_ref[...] = s0_ref[1]
    else:
        sf_ref[...] = jnp.zeros((RET_W, RET_W), F32)
        sb_ref[...] = jnp.zeros((RET_W, RET_W), F32)

    def bwd_step(n, carry):
        ci = nc - 1 - n
        r0 = pl.multiple_of(ci * c, c)
        sbs_ref[ci] = sb_ref[...].astype(BF16)
        k = load_qk(RET_W, r0) * (RET_DK ** -0.5)
        v = ur_ref[pl.ds(r0, c), 2 * RET_W:3 * RET_W]
        sb_ref[...] = carry_b * sb_ref[...] + state_delta(k, v, write_b)
        return carry

    def fwd_step(ci, carry):
        r0 = pl.multiple_of(ci * c, c)
        q = load_qk(0, r0)
        k = load_qk(RET_W, r0) * (RET_DK ** -0.5)
        v = ur_ref[pl.ds(r0, c), 2 * RET_W:3 * RET_W]
        att = lax.dot_general(q.astype(BF16), _head_blocks(k.astype(BF16), RET_DK),
                              (((1,), (1,)), ((), ())), preferred_element_type=F32)
        o = jnp.dot((att * decay).astype(BF16), _head_blocks(v, RET_DK), preferred_element_type=F32)
        o = o + jnp.dot((q * read_f).astype(BF16), sf_ref[...].astype(BF16), preferred_element_type=F32)
        o = o + jnp.dot((q * read_b).astype(BF16), sbs_ref[ci], preferred_element_type=F32)
        sf_ref[...] = carry_f * sf_ref[...] + state_delta(k, v, write_f)
        sq = o * o
        hi = sq.astype(BF16)
        lo = (sq - hi.astype(F32)).astype(BF16)
        ms = (jnp.dot(hi, group_mean, preferred_element_type=F32)
              + jnp.dot(lo, group_mean, preferred_element_type=F32))
        y = o * lax.rsqrt(ms + EPS) * gn_ref[...]
        z = ur_ref[pl.ds(r0, c), 3 * RET_W:4 * RET_W].astype(F32)
        yb_ref[pl.ds(r0, c), :] = (y * _silu(z)).astype(BF16)
        return carry

    lax.fori_loop(0, nc, bwd_step, 0)
    lax.fori_loop(0, nc, fwd_step, 0)
    for h in range(RET_HEADS):
        sl = slice(h * RET_DK, (h + 1) * RET_DK)
        sfin_ref[0, h] = sf_ref[sl, sl]
        sfin_ref[1, h] = sb_ref[sl, sl]


def _retention(ur, dec_qk, dec_att, gn, *, rope_tabs=None, s0=None):
    b, t, _ = ur.shape
    nc = t // RET_CHUNK
    rope = rope_tabs is not None
    has_s0 = s0 is not None
    in_specs = [pl.BlockSpec((None, t, 4 * RET_W), lambda i: (i, 0, 0))]
    args = [ur]
    if rope:
        in_specs += [_const_spec((t, LANES)), _const_spec((t, LANES))]
        args += list(rope_tabs)
    if has_s0:
        in_specs.append(pl.BlockSpec((None, 2, RET_W, RET_W), lambda i: (i, 0, 0, 0)))
        args.append(s0)
    in_specs += [_const_spec((2, RET_W)), _const_spec((2, RET_HEADS * RET_CHUNK)), _const_spec((1, RET_W))]
    args += [dec_qk, dec_att, gn]
    return pl.pallas_call(
        functools.partial(_ret_kernel, t=t, rope=rope, has_s0=has_s0),
        grid=(b,),
        in_specs=in_specs,
        out_specs=[pl.BlockSpec((None, t, RET_W), lambda i: (i, 0, 0)),
                   pl.BlockSpec((None, 2, RET_HEADS, RET_DK, RET_DK), lambda i: (i, 0, 0, 0, 0))],
        out_shape=[jax.ShapeDtypeStruct((b, t, RET_W), BF16),
                   jax.ShapeDtypeStruct((b, 2, RET_HEADS, RET_DK, RET_DK), F32)],
        scratch_shapes=[pltpu.VMEM((RET_W, RET_W), F32), pltpu.VMEM((RET_W, RET_W), F32),
                        pltpu.VMEM((nc, RET_W, RET_W), BF16)],
        compiler_params=_params(("parallel",)),
        name="retention",
    )(*args)


def _kv_variants(a):
    lane_half = lax.broadcasted_iota(jnp.int32, a.shape, 1) // HEAD_DIM
    swapped = pltpu.roll(a, HEAD_DIM, axis=1)
    out = []
    for g in range(ATT_KV_HEADS):
        out.append([jnp.where(lane_half == half, a if half == g else swapped, 0.0).astype(BF16)
                    for half in range(2)])
    return out


def _attend_pair(qp, keys, values, masks, sinks, zp):
    o = None
    recips = []
    for half in range(2):
        logits = []
        for kpart, mask in zip(keys[half], masks):
            l = lax.dot_general(qp, kpart, (((1,), (1,)), ((), ())), preferred_element_type=F32)
            logits.append(l if mask is None else jnp.where(mask, l, NEG))
        m = sinks[half]
        for l in logits:
            m = jnp.maximum(m, jnp.max(l, axis=-1, keepdims=True))
        den = jnp.exp(sinks[half] - m)
        for l, vpart in zip(logits, values[half]):
            e = jnp.exp(l - m)
            den = den + jnp.sum(e, axis=-1, keepdims=True)
            pv = jnp.dot(e.astype(BF16), vpart, preferred_element_type=F32)
            o = pv if o is None else o + pv
        recips.append(1.0 / den)
    lane = lax.broadcasted_iota(jnp.int32, o.shape, 1)
    r = jnp.where(lane < HEAD_DIM, recips[0], recips[1])
    return (o * r * _silu(zp)).astype(BF16)


def _ctx_attn_kernel(sink_ref, ua_ref, yc_ref):
    scale = HEAD_DIM ** -0.5
    kv = ua_ref[:, 2 * ATT_W:].astype(F32)
    kvar = _kv_variants(kv[:, :LANES] * scale)
    vvar = _kv_variants(kv[:, LANES:])
    for p in range(ATT_Q_HEADS // 2):
        g = p // (ATT_Q_HEADS // ATT_KV_HEADS // 2)
        sl = slice(p * LANES, (p + 1) * LANES)
        yc_ref[:, sl] = _attend_pair(
            ua_ref[:, sl], [[kvar[g][0]], [kvar[g][1]]], [[vvar[g][0]], [vvar[g][1]]], [None],
            [sink_ref[2 * p], sink_ref[2 * p + 1]], ua_ref[:, ATT_W + p * LANES:ATT_W + (p + 1) * LANES].astype(F32))


def _ctx_attention(ua, sink):
    b, t, w = ua.shape
    return pl.pallas_call(
        _ctx_attn_kernel,
        grid=(b,),
        in_specs=[pl.BlockSpec(memory_space=pltpu.SMEM),
                  pl.BlockSpec((None, t, w), lambda i: (i, 0, 0))],
        out_specs=pl.BlockSpec((None, t, ATT_W), lambda i: (i, 0, 0)),
        out_shape=jax.ShapeDtypeStruct((b, t, ATT_W), BF16),
        compiler_params=_params(("parallel",)),
        name="ctx_attention",
    )(sink, ua)


def _lat_attn_kernel(sink_ref, qz_ref, kv_ref, ck_ref, cv_ref, cos_ref, sin_ref, yc_ref,
                     kl_ref, vl_ref, kc_ref, vc_ref, *, t, tq):
    j = pl.program_id(1)
    scale = HEAD_DIM ** -0.5

    @pl.when(j == 0)
    def _prepare():
        pad = jnp.zeros((WINDOW, LANES), BF16)
        kv = kv_ref[...].astype(F32)
        kvar = _kv_variants(_rope(kv[:, :LANES], cos_ref[...], sin_ref[...]) * scale)
        vvar = _kv_variants(kv[:, LANES:])
        cvar = _kv_variants(ck_ref[...] * scale)
        dvar = _kv_variants(cv_ref[...])
        for g in range(ATT_KV_HEADS):
            for half in range(2):
                i = 2 * g + half
                for ref, var in ((kl_ref, kvar), (vl_ref, vvar)):
                    ref[i, 0:WINDOW, :] = pad
                    ref[i, WINDOW:WINDOW + t, :] = var[g][half]
                    ref[i, WINDOW + t:, :] = pad
                kc_ref[i] = cvar[g][half]
                vc_ref[i] = dvar[g][half]

    r0 = pl.multiple_of(j * tq, tq)
    nloc = tq + 2 * WINDOW
    rr = lax.broadcasted_iota(jnp.int32, (tq, nloc), 0)
    ss = lax.broadcasted_iota(jnp.int32, (tq, nloc), 1)
    band = ((ss - rr >= 0) & (ss - rr <= 2 * WINDOW)
            & (ss >= WINDOW - j * tq) & (ss < t + WINDOW - j * tq))
    cs = cos_ref[pl.ds(r0, tq), :]
    sn = sin_ref[pl.ds(r0, tq), :]
    for p in range(ATT_Q_HEADS // 2):
        g = p // (ATT_Q_HEADS // ATT_KV_HEADS // 2)
        sl = slice(p * LANES, (p + 1) * LANES)
        qp = _rope(qz_ref[:, sl].astype(F32), cs, sn).astype(BF16)
        keys = [[kl_ref[2 * g + half, pl.ds(r0, nloc), :], kc_ref[2 * g + half]] for half in range(2)]
        vals = [[vl_ref[2 * g + half, pl.ds(r0, nloc), :], vc_ref[2 * g + half]] for half in range(2)]
        yc_ref[:, sl] = _attend_pair(
            qp, keys, vals, [band, None], [sink_ref[2 * p], sink_ref[2 * p + 1]],
            qz_ref[:, ATT_W + p * LANES:ATT_W + (p + 1) * LANES].astype(F32))


def _lat_attention(ua, cache_k, cache_v, layer, sink, rope_tabs):
    b, t, _ = ua.shape
    past = cache_k.shape[2]
    tq = 256
    nvar = 2 * ATT_KV_HEADS
    cache_spec = pl.BlockSpec((None, None, past, ATT_KV_W), lambda i, j: (i, layer, 0, 0))
    return pl.pallas_call(
        functools.partial(_lat_attn_kernel, t=t, tq=tq),
        grid=(b, t // tq),
        in_specs=[pl.BlockSpec(memory_space=pltpu.SMEM),
                  pl.BlockSpec((None, tq, 2 * ATT_W), lambda i, j: (i, j, 0)),
                  pl.BlockSpec((None, t, 2 * ATT_KV_W), lambda i, j: (i, 0, 2 * ATT_W // (2 * ATT_KV_W))),
                  cache_spec, cache_spec,
                  _const_spec((t, LANES)), _const_spec((t, LANES))],
        out_specs=pl.BlockSpec((None, tq, ATT_W), lambda i, j: (i, j, 0)),
        out_shape=jax.ShapeDtypeStruct((b, t, ATT_W), BF16),
        scratch_shapes=[pltpu.VMEM((nvar, t + 2 * WINDOW, LANES), BF16),
                        pltpu.VMEM((nvar, t + 2 * WINDOW, LANES), BF16),
                        pltpu.VMEM((nvar, past, LANES), BF16),
                        pltpu.VMEM((nvar, past, LANES), BF16)],
        compiler_params=_params(("parallel", "arbitrary")),
        name="lat_attention",
    )(sink, ua, ua, cache_k, cache_v, *rope_tabs)


def _outproj_kernel(x_ref, ya_ref, yb_ref, yc_ref, mg_ref, mod_ref, g_ref,
                    wa_ref, wb_ref, wc_ref, wo_ref, o_ref, *, tm, row0, rows_per_mod):
    row = _mod_row(pl.program_id(0), tm, row0, rows_per_mod)
    merged = None
    for c, (y_ref, w_ref) in enumerate(((ya_ref, wa_ref), (yb_ref, wb_ref), (yc_ref, wc_ref))):
        gate = jax.nn.sigmoid(mg_ref[:, c * D_MODEL:(c + 1) * D_MODEL].astype(F32))
        term = gate * jnp.dot(y_ref[...], w_ref[...], preferred_element_type=F32)
        merged = term if merged is None else merged + term
    out = jnp.dot(merged.astype(BF16), wo_ref[...], preferred_element_type=F32)
    ms = jnp.mean(out * out, axis=-1, keepdims=True)
    normed = out * lax.rsqrt(ms + EPS) * g_ref[...]
    gate = mod_ref[pl.ds(row, 1), 2 * D_MODEL:3 * D_MODEL]
    o_ref[...] = x_ref[...] + gate * normed


def _outproj(x2, ya, yb, yc, mg, mod_l, g_post, wa, wb, wc, wo, *, row0, rows_per_mod):
    n = x2.shape[0]
    tm = 512
    row_spec = lambda w: pl.BlockSpec((tm, w), lambda i: (i, 0))
    return pl.pallas_call(
        functools.partial(_outproj_kernel, tm=tm, row0=row0, rows_per_mod=rows_per_mod),
        grid=(n // tm,),
        in_specs=[row_spec(D_MODEL), row_spec(FOURIER_W), row_spec(RET_W), row_spec(ATT_W),
                  row_spec(3 * D_MODEL),
                  _const_spec((MOD_ROWS, 3 * D_MODEL)), _const_spec((1, D_MODEL)),
                  _const_spec((FOURIER_W, D_MODEL)), _const_spec((RET_W, D_MODEL)),
                  _const_spec((ATT_W, D_MODEL)), _const_spec((D_MODEL, D_MODEL))],
        out_specs=row_spec(D_MODEL),
        out_shape=jax.ShapeDtypeStruct((n, D_MODEL), F32),
        compiler_params=_params(("parallel",)),
        name="outproj",
    )(x2, ya, yb, yc, mg, mod_l, g_post.reshape(1, D_MODEL), wa, wb, wc, wo)


def _layer(x, layer, mod_l, wts, *, latent, dft_chan, dft_pos, rope_tabs, cache_k=None, cache_v=None,
           s_ctx=None):
    b, t, _ = x.shape
    n = b * t
    row0, rows_per_mod = (1, t) if latent else (0, n)
    x2 = x.reshape(n, D_MODEL)
    xcs, fz, ur, ua, mg, kv32 = _inproj(x2, mod_l, wts["g_pre"], wts["w_in"], dft_chan,
                                        row0=row0, rows_per_mod=rows_per_mod)
    ya = _fourier(xcs.reshape(b, t, -1), fz.reshape(b, t, -1), dft_pos, wts["w_four"],
                  bg=4, tq=min(t, 512))
    ur3 = ur.reshape(b, t, -1)
    ua3 = ua.reshape(b, t, -1)
    if latent:
        yb, s_fin = _retention(ur3, wts["dec_qk"], wts["dec_att"], wts["gn"], rope_tabs=rope_tabs, s0=s_ctx)
        yc = _lat_attention(ua3, cache_k, cache_v, layer, wts["sink"], rope_tabs)
    else:
        yb, s_fin = _retention(ur3, wts["dec_qk"], wts["dec_att"], wts["gn"])
        yc = _ctx_attention(ua3, wts["sink"])
    out = _outproj(x2, ya.reshape(n, -1), yb.reshape(n, -1), yc.reshape(n, -1), mg, mod_l,
                   wts["g_post"], wts["w_pa"], wts["w_pb"], wts["w_pc"], wts["w_out"],
                   row0=row0, rows_per_mod=rows_per_mod)
    return out.reshape(b, t, D_MODEL), kv32, s_fin


def _block_diag_state(s):
    b = s.shape[0]
    eye = jnp.eye(RET_HEADS, dtype=s.dtype)
    return jnp.einsum("bdhke,hg->bdhkge", s, eye).reshape(b, 2, RET_W, RET_W)


def kernel(x_prompt, x_sample, cache_k, cache_v, state_ret, c, c_ctx, w_mod, b_mod, g_pre, g_post, w_in,
           w_four, ret_decay, ret_gn, attn_sink, w_branch_a, w_branch_b, w_branch_c, w_out):
    batch, seq, _ = x_prompt.shape
    dec_batch, dec_seq, _ = x_sample.shape
    past = cache_k.shape[2]
    assert 1 + dec_batch <= MOD_ROWS

    cv = jnp.zeros((MOD_ROWS, D_MODEL), F32).at[0].set(c_ctx).at[1:1 + dec_batch].set(c)
    mod = _modulation(cv, w_mod, b_mod)

    dft_chan, dft_ctx = _dft_tables(seq)
    _, dft_lat = _dft_tables(dec_seq)
    rope_tabs = _rope_tables(dec_seq)
    ck = cache_k.reshape(dec_batch, DEPTH, past, ATT_KV_W)
    cvv = cache_v.reshape(dec_batch, DEPTH, past, ATT_KV_W)

    layers = []
    for l in range(DEPTH):
        layers.append(dict(
            g_pre=g_pre[l], g_post=g_post[l],
            w_in=w_in[l].astype(BF16), w_four=w_four[l].astype(BF16),
            w_pa=w_branch_a[l].astype(BF16), w_pb=w_branch_b[l].astype(BF16),
            w_pc=w_branch_c[l].astype(BF16), w_out=w_out[l].astype(BF16),
            dec_qk=jnp.repeat(ret_decay[l], RET_DK, axis=1),
            dec_att=jnp.repeat(ret_decay[l], RET_CHUNK, axis=1),
            gn=ret_gn[l].reshape(1, RET_W),
            sink=attn_sink[l].reshape(ATT_Q_HEADS)))

    xp = x_prompt
    ks, vs, ss = [], [], []
    for l in range(DEPTH):
        xp, kv32, s_fin = _layer(xp, l, mod[l], layers[l], latent=False, dft_chan=dft_chan,
                                 dft_pos=dft_ctx, rope_tabs=None)
        ks.append(kv32[:, :ATT_KV_W].reshape(batch, seq, ATT_KV_HEADS, HEAD_DIM))
        vs.append(kv32[:, ATT_KV_W:].reshape(batch, seq, ATT_KV_HEADS, HEAD_DIM))
        ss.append(s_fin)

    xs = x_sample
    for l in range(DEPTH):
        xs, _, _ = _layer(xs, l, mod[l], layers[l], latent=True, dft_chan=dft_chan, dft_pos=dft_lat,
                          rope_tabs=rope_tabs, cache_k=ck, cache_v=cvv,
                          s_ctx=_block_diag_state(state_ret[:, l]))

    return (xp, xs, jnp.stack(ks, axis=1), jnp.stack(vs, axis=1), jnp.stack(ss, axis=1))
```

```python
import functools

import numpy as np
import jax
import jax.numpy as jnp
from jax import lax
from jax.experimental import pallas as pl
from jax.experimental.pallas import tpu as pltpu

F32 = jnp.float32
BF16 = jnp.bfloat16

D_MODEL = 1024
DEPTH = 2
GRID_W = 64
HEAD_DIM = 64
FOURIER_GROUPS = 4
FOURIER_GROUP_W = 64
FOURIER_W = FOURIER_GROUPS * FOURIER_GROUP_W
RET_HEADS = 4
RET_DK = 64
RET_W = RET_HEADS * RET_DK
RET_CHUNK = 128
ATT_Q_HEADS = 8
ATT_KV_HEADS = 2
ATT_W = ATT_Q_HEADS * HEAD_DIM
ATT_KV_W = ATT_KV_HEADS * HEAD_DIM
WINDOW = 128
ROPE_BASE = 10000.0
EPS = 1e-6
MOD_ROWS = 8
LANES = 128
NEG = -1e30
VMEM_LIMIT = 56 * 1024 * 1024

C_FX, C_RQ, C_RZ_END = 0, 512, 1536
C_AQ, C_AK, C_AV, C_AZ, C_MG, C_END = 1536, 2048, 2176, 2304, 2816, 5888


def _sigmoid(x):
    return 0.5 * jnp.tanh(0.5 * x) + 0.5


def _silu(x):
    return x * _sigmoid(x)


def _params(sem):
    return pltpu.CompilerParams(dimension_semantics=sem, vmem_limit_bytes=VMEM_LIMIT)


def _const_spec(shape):
    nd = len(shape)
    return pl.BlockSpec(shape, lambda *_: (0,) * nd, pipeline_mode=pl.Buffered(1))


def _layer_spec(shape, layer):
    nd = len(shape)
    return pl.BlockSpec((None,) + tuple(shape), lambda *_: (layer,) + (0,) * nd,
                        pipeline_mode=pl.Buffered(1))


def _dft_tables(t):
    c = np.arange(FOURIER_GROUP_W)
    ang = 2.0 * np.pi * ((c[:, None] * c[None, :]) % FOURIER_GROUP_W) / FOURIER_GROUP_W
    eye = np.eye(FOURIER_GROUPS)
    s64 = FOURIER_GROUP_W ** -0.5
    chan = np.concatenate([np.kron(eye, np.cos(ang) * s64), np.kron(eye, np.sin(ang) * s64)], axis=1)
    p = np.arange(t)
    angt = 2.0 * np.pi * ((p[:, None] * p[None, :]) % t) / t
    pos = np.concatenate([np.cos(angt), -np.sin(angt)], axis=1) * (t ** -0.5)
    return jnp.asarray(chan, F32).astype(BF16), jnp.asarray(pos, F32).astype(BF16)


def _rope_tables(t):
    quarter = HEAD_DIM // 4
    lane = np.arange(LANES) % HEAD_DIM
    inv = ROPE_BASE ** (-(lane % quarter).astype(np.float64) / quarter)
    n = np.arange(t)
    pos = np.where(lane[None, :] < HEAD_DIM // 2, (n // GRID_W)[:, None], (n % GRID_W)[:, None])
    ang = pos.astype(np.float64) * inv[None, :]
    sign = np.where((lane % (2 * quarter)) < quarter, -1.0, 1.0)
    return jnp.asarray(np.cos(ang), F32), jnp.asarray(np.sin(ang) * sign[None, :], F32)


def _rope(x, cos, sin):
    lane = lax.broadcasted_iota(jnp.int32, x.shape, 1)
    first = (lane & 31) < 16
    partner = jnp.where(first, pltpu.roll(x, LANES - 16, axis=1), pltpu.roll(x, 16, axis=1))
    return x * cos + partner * sin


def _mod_kernel(cv_ref, w_ref, b_ref, o_ref):
    a = _silu(cv_ref[...])
    o_ref[...] = jnp.dot(a, w_ref[...], preferred_element_type=F32,
                         precision=lax.Precision.HIGHEST) + b_ref[...]


def _modulation(cv, w_mod, b_mod):
    tn = 1024
    return pl.pallas_call(
        _mod_kernel,
        grid=(DEPTH, 3 * D_MODEL // tn),
        in_specs=[pl.BlockSpec((MOD_ROWS, D_MODEL), lambda l, j: (0, 0)),
                  pl.BlockSpec((None, D_MODEL, tn), lambda l, j: (l, 0, j)),
                  pl.BlockSpec((None, 1, tn), lambda l, j: (l, 0, j))],
        out_specs=pl.BlockSpec((None, MOD_ROWS, tn), lambda l, j: (l, 0, j)),
        out_shape=jax.ShapeDtypeStruct((DEPTH, MOD_ROWS, 3 * D_MODEL), F32),
        compiler_params=_params(("parallel", "parallel")),
        name="modulation",
    )(cv, w_mod, b_mod.reshape(DEPTH, 1, 3 * D_MODEL))


def _mod_row(i, tm, row0, rows_per_mod):
    return row0 + (i * tm) // rows_per_mod


def _inproj_kernel(x_ref, mod_ref, g_ref, w_ref, dft_ref,
                   xcs_ref, fz_ref, ur_ref, ua_ref, mg_ref, k_ref, v_ref, *, tm, row0, rows_per_mod):
    row = _mod_row(pl.program_id(0), tm, row0, rows_per_mod)
    x = x_ref[...]
    ms = jnp.mean(x * x, axis=-1, keepdims=True)
    y = x * lax.rsqrt(ms + EPS) * g_ref[...]
    shift = mod_ref[pl.ds(row, 1), 0:D_MODEL]
    scale = mod_ref[pl.ds(row, 1), D_MODEL:2 * D_MODEL]
    h = (y * (1.0 + scale) + shift).astype(BF16)

    def mm(c0, c1):
        return jnp.dot(h, w_ref[:, c0:c1], preferred_element_type=F32)

    f = mm(C_FX, C_RQ)
    xcs_ref[...] = jnp.dot(f[:, :FOURIER_W].astype(BF16), dft_ref[...],
                           preferred_element_type=F32).astype(BF16)
    fz_ref[...] = f[:, FOURIER_W:].astype(BF16)
    ur_ref[...] = mm(C_RQ, C_RZ_END).astype(BF16)
    ua_ref[:, 0:ATT_W] = mm(C_AQ, C_AK).astype(BF16)
    ua_ref[:, ATT_W:2 * ATT_W] = mm(C_AZ, C_MG).astype(BF16)
    kv = mm(C_AK, C_AZ)
    k_ref[...] = kv[:, :ATT_KV_W]
    v_ref[...] = kv[:, ATT_KV_W:]
    ua_ref[:, 2 * ATT_W:] = kv.astype(BF16)
    for c in range(3):
        mg_ref[:, c * D_MODEL:(c + 1) * D_MODEL] = mm(
            C_MG + c * D_MODEL, C_MG + (c + 1) * D_MODEL).astype(BF16)


def _inproj(x2, layer, mod, g_pre, w_in, dft, *, row0, rows_per_mod):
    n = x2.shape[0]
    tm = 512
    row_spec = lambda w: pl.BlockSpec((tm, w), lambda i: (i, 0))
    widths = (2 * FOURIER_W, FOURIER_W, 4 * RET_W, 2 * ATT_W + 2 * ATT_KV_W, 3 * D_MODEL)
    out_shape = [jax.ShapeDtypeStruct((n, w), BF16) for w in widths]
    out_shape += [jax.ShapeDtypeStruct((n, ATT_KV_W), F32)] * 2
    return pl.pallas_call(
        functools.partial(_inproj_kernel, tm=tm, row0=row0, rows_per_mod=rows_per_mod),
        grid=(n // tm,),
        in_specs=[row_spec(D_MODEL),
                  _layer_spec((MOD_ROWS, 3 * D_MODEL), layer),
                  _layer_spec((1, D_MODEL), layer),
                  _layer_spec((D_MODEL, C_END), layer),
                  _const_spec((FOURIER_W, 2 * FOURIER_W))],
        out_specs=[row_spec(w) for w in widths] + [row_spec(ATT_KV_W)] * 2,
        out_shape=out_shape,
        compiler_params=_params(("parallel",)),
        name="inproj",
    )(x2, mod, g_pre, w_in, dft)


def _fourier_kernel(ct_ref, xcs_ref, fz_ref, w_ref, ya_ref, xcat_ref, *, bg, t):
    @pl.when(pl.program_id(1) == 0)
    def _gather():
        for b in range(bg):
            cols = slice(b * FOURIER_W, (b + 1) * FOURIER_W)
            xcat_ref[0:t, cols] = xcs_ref[b, :, 0:FOURIER_W]
            xcat_ref[t:2 * t, cols] = xcs_ref[b, :, FOURIER_W:]

    yr = jnp.dot(ct_ref[...], xcat_ref[...], preferred_element_type=F32).astype(BF16)
    for b in range(bg):
        ya = jnp.dot(yr[:, b * FOURIER_W:(b + 1) * FOURIER_W], w_ref[...], preferred_element_type=F32)
        ya_ref[b] = (ya * _silu(fz_ref[b].astype(F32))).astype(BF16)


def _fourier(xcs, fz, ct, w_four, layer, *, bg, tq):
    b, t, _ = xcs.shape
    return pl.pallas_call(
        functools.partial(_fourier_kernel, bg=bg, t=t),
        grid=(b // bg, t // tq),
        in_specs=[pl.BlockSpec((tq, 2 * t), lambda i, j: (j, 0)),
                  pl.BlockSpec((bg, t, 2 * FOURIER_W), lambda i, j: (i, 0, 0)),
                  pl.BlockSpec((bg, tq, FOURIER_W), lambda i, j: (i, j, 0)),
                  _layer_spec((FOURIER_W, FOURIER_W), layer)],
        out_specs=pl.BlockSpec((bg, tq, FOURIER_W), lambda i, j: (i, j, 0)),
        out_shape=jax.ShapeDtypeStruct((b, t, FOURIER_W), BF16),
        scratch_shapes=[pltpu.VMEM((2 * t, bg * FOURIER_W), BF16)],
        compiler_params=_params(("parallel", "arbitrary")),
        name="fourier",
    )(ct, xcs, fz, w_four)


def _log_sigmoid(x):
    return jnp.minimum(x, 0.0) - jnp.log(1.0 + jnp.exp(-jnp.abs(x)))


def _head_blocks(a, width):
    lane = lax.broadcasted_iota(jnp.int32, a.shape, 1)
    zero = jnp.zeros_like(a)
    return jnp.concatenate(
        [jnp.where((lane >= h * width) & (lane < (h + 1) * width), a, zero) for h in range(RET_HEADS)],
        axis=0)


def _ret_kernel(*refs, t, rope, has_s0):
    refs = list(refs)
    ur_ref = refs.pop(0)
    cos_ref = refs.pop(0) if rope else None
    sin_ref = refs.pop(0) if rope else None
    s0_ref = refs.pop(0) if has_s0 else None
    dec_qk_ref, dec_att_ref, gn_ref, yb_ref, sfin_ref, sf_ref, sb_ref, sbs_ref = refs
    c = RET_CHUNK
    nc = t // c

    lgf = _log_sigmoid(dec_qk_ref[0:1, :])
    lgb = _log_sigmoid(dec_qk_ref[1:2, :])
    ri = lax.broadcasted_iota(jnp.int32, (c, RET_W), 0).astype(F32)
    read_f = jnp.exp((ri + 1.0) * lgf)
    read_b = jnp.exp((c - ri) * lgb)
    write_f = jnp.exp((c - 1.0 - ri) * lgf)
    write_b = jnp.exp(ri * lgb)
    carry_f = jnp.exp(c * lgf)
    carry_b = jnp.exp(c * lgb)
    lgf_a = _log_sigmoid(dec_att_ref[0:1, :])
    lgb_a = _log_sigmoid(dec_att_ref[1:2, :])
    ii = lax.broadcasted_iota(jnp.int32, (c, RET_HEADS * c), 0)
    jj = lax.broadcasted_iota(jnp.int32, (c, RET_HEADS * c), 1) & (c - 1)
    diff = (ii - jj).astype(F32)
    decay = (jnp.where(diff >= 0, jnp.exp(jnp.maximum(diff, 0.0) * lgf_a), 0.0)
             + jnp.where(diff <= 0, jnp.exp(jnp.maximum(-diff, 0.0) * lgb_a), 0.0))
    r2 = lax.broadcasted_iota(jnp.int32, (RET_W, RET_W), 0)
    c2 = lax.broadcasted_iota(jnp.int32, (RET_W, RET_W), 1)
    same_head = (r2 // RET_DK) == (c2 // RET_DK)
    group_mean = jnp.where(same_head, 1.0 / RET_DK, 0.0).astype(BF16)

    def load_qk(col, r0):
        a = ur_ref[pl.ds(r0, c), col:col + RET_W].astype(F32)
        if rope:
            cs = cos_ref[pl.ds(r0, c), :]
            sn = sin_ref[pl.ds(r0, c), :]
            a = jnp.concatenate([_rope(a[:, :LANES], cs, sn), _rope(a[:, LANES:], cs, sn)], axis=1)
        return a

    def state_delta(k, v, write):
        kw = (k * write).astype(BF16)
        d = lax.dot_general(kw, v, (((0,), (0,)), ((), ())), preferred_element_type=F32)
        return jnp.where(same_head, d, 0.0)

    sf_ref[...] = jnp.zeros((RET_W, RET_W), F32)
    sb_ref[...] = jnp.zeros((RET_W, RET_W), F32)
    if has_s0:
        for h in range(RET_HEADS):
            sl = slice(h * RET_DK, (h + 1) * RET_DK)
            sf_ref[sl, sl] = s0_ref[0, h]
            sb_ref[sl, sl] = s0_ref[1, h]

    def bwd_step(n, carry):
        ci = nc - 1 - n
        r0 = pl.multiple_of(ci * c, c)
        sbs_ref[ci] = sb_ref[...].astype(BF16)
        k = load_qk(RET_W, r0) * (RET_DK ** -0.5)
        v = ur_ref[pl.ds(r0, c), 2 * RET_W:3 * RET_W]
        sb_ref[...] = carry_b * sb_ref[...] + state_delta(k, v, write_b)
        return carry

    def fwd_step(ci, carry):
        r0 = pl.multiple_of(ci * c, c)
        q = load_qk(0, r0)
        k = load_qk(RET_W, r0) * (RET_DK ** -0.5)
        v = ur_ref[pl.ds(r0, c), 2 * RET_W:3 * RET_W]
        att = lax.dot_general(q.astype(BF16), _head_blocks(k.astype(BF16), RET_DK),
                              (((1,), (1,)), ((), ())), preferred_element_type=F32)
        o = jnp.dot((att * decay).astype(BF16), _head_blocks(v, RET_DK), preferred_element_type=F32)
        o = o + jnp.dot((q * read_f).astype(BF16), sf_ref[...].astype(BF16), preferred_element_type=F32)
        o = o + jnp.dot((q * read_b).astype(BF16), sbs_ref[ci], preferred_element_type=F32)
        sf_ref[...] = carry_f * sf_ref[...] + state_delta(k, v, write_f)
        sq = o * o
        hi = sq.astype(BF16)
        lo = (sq - hi.astype(F32)).astype(BF16)
        ms = (jnp.dot(hi, group_mean, preferred_element_type=F32)
              + jnp.dot(lo, group_mean, preferred_element_type=F32))
        y = o * lax.rsqrt(ms + EPS) * gn_ref[...]
        z = ur_ref[pl.ds(r0, c), 3 * RET_W:4 * RET_W].astype(F32)
        yb_ref[pl.ds(r0, c), :] = (y * _silu(z)).astype(BF16)
        return carry

    lax.fori_loop(0, nc, bwd_step, 0)
    lax.fori_loop(0, nc, fwd_step, 0)
    for h in range(RET_HEADS):
        sl = slice(h * RET_DK, (h + 1) * RET_DK)
        sfin_ref[0, h] = sf_ref[sl, sl]
        sfin_ref[1, h] = sb_ref[sl, sl]


def _retention(ur, layer, dec_qk, dec_att, gn, *, rope_tabs=None, state=None):
    b, t, _ = ur.shape
    nc = t // RET_CHUNK
    rope = rope_tabs is not None
    has_s0 = state is not None
    in_specs = [pl.BlockSpec((None, t, 4 * RET_W), lambda i: (i, 0, 0))]
    args = [ur]
    if rope:
        in_specs += [_const_spec((t, LANES)), _const_spec((t, LANES))]
        args += list(rope_tabs)
    if has_s0:
        in_specs.append(pl.BlockSpec((None, None, 2, RET_HEADS, RET_DK, RET_DK),
                                     lambda i: (i, layer, 0, 0, 0, 0)))
        args.append(state)
    in_specs += [_layer_spec((2, RET_W), layer), _layer_spec((2, RET_HEADS * RET_CHUNK), layer),
                 _layer_spec((1, RET_W), layer)]
    args += [dec_qk, dec_att, gn]
    return pl.pallas_call(
        functools.partial(_ret_kernel, t=t, rope=rope, has_s0=has_s0),
        grid=(b,),
        in_specs=in_specs,
        out_specs=[pl.BlockSpec((None, t, RET_W), lambda i: (i, 0, 0)),
                   pl.BlockSpec((None, 2, RET_HEADS, RET_DK, RET_DK), lambda i: (i, 0, 0, 0, 0))],
        out_shape=[jax.ShapeDtypeStruct((b, t, RET_W), BF16),
                   jax.ShapeDtypeStruct((b, 2, RET_HEADS, RET_DK, RET_DK), F32)],
        scratch_shapes=[pltpu.VMEM((RET_W, RET_W), F32), pltpu.VMEM((RET_W, RET_W), F32),
                        pltpu.VMEM((nc, RET_W, RET_W), BF16)],
        compiler_params=_params(("parallel",)),
        name="retention",
    )(*args)


def _kv_variants(a):
    lane_half = lax.broadcasted_iota(jnp.int32, a.shape, 1) // HEAD_DIM
    swapped = pltpu.roll(a, HEAD_DIM, axis=1)
    out = []
    for g in range(ATT_KV_HEADS):
        out.append([jnp.where(lane_half == half, a if half == g else swapped, 0.0).astype(BF16)
                    for half in range(2)])
    return out


def _attend_pair(qp, keys, values, masks, sinks, zp):
    o = None
    recips = []
    for half in range(2):
        logits = []
        for kpart, mask in zip(keys[half], masks):
            l = lax.dot_general(qp, kpart, (((1,), (1,)), ((), ())), preferred_element_type=F32)
            logits.append(l if mask is None else jnp.where(mask, l, NEG))
        m = sinks[half]
        for l in logits:
            m = jnp.maximum(m, jnp.max(l, axis=-1, keepdims=True))
        den = jnp.exp(sinks[half] - m)
        for l, vpart in zip(logits, values[half]):
            e = jnp.exp(l - m)
            den = den + jnp.sum(e, axis=-1, keepdims=True)
            pv = jnp.dot(e.astype(BF16), vpart, preferred_element_type=F32)
            o = pv if o is None else o + pv
        recips.append(1.0 / den)
    lane = lax.broadcasted_iota(jnp.int32, o.shape, 1)
    r = jnp.where(lane < HEAD_DIM, recips[0], recips[1])
    return (o * r * _silu(zp)).astype(BF16)


def _ctx_attn_kernel(sink_ref, ua_ref, yc_ref, *, layer):
    scale = HEAD_DIM ** -0.5
    kv = ua_ref[:, 2 * ATT_W:].astype(F32)
    kvar = _kv_variants(kv[:, :LANES] * scale)
    vvar = _kv_variants(kv[:, LANES:])
    for p in range(ATT_Q_HEADS // 2):
        g = p // (ATT_Q_HEADS // ATT_KV_HEADS // 2)
        sl = slice(p * LANES, (p + 1) * LANES)
        yc_ref[:, sl] = _attend_pair(
            ua_ref[:, sl], [[kvar[g][0]], [kvar[g][1]]], [[vvar[g][0]], [vvar[g][1]]], [None],
            [sink_ref[layer, 2 * p], sink_ref[layer, 2 * p + 1]],
            ua_ref[:, ATT_W + p * LANES:ATT_W + (p + 1) * LANES].astype(F32))


def _ctx_attention(ua, layer, sink):
    b, t, w = ua.shape
    return pl.pallas_call(
        functools.partial(_ctx_attn_kernel, layer=layer),
        grid=(b,),
        in_specs=[pl.BlockSpec(memory_space=pltpu.SMEM),
                  pl.BlockSpec((None, t, w), lambda i: (i, 0, 0))],
        out_specs=pl.BlockSpec((None, t, ATT_W), lambda i: (i, 0, 0)),
        out_shape=jax.ShapeDtypeStruct((b, t, ATT_W), BF16),
        compiler_params=_params(("parallel",)),
        name="ctx_attention",
    )(sink, ua)


def _lat_attn_kernel(sink_ref, qz_ref, kv_ref, ck_ref, cv_ref, cos_ref, sin_ref, yc_ref,
                     kl_ref, vl_ref, kc_ref, vc_ref, *, t, tq, layer):
    j = pl.program_id(1)
    scale = HEAD_DIM ** -0.5

    @pl.when(j == 0)
    def _prepare():
        pad = jnp.zeros((WINDOW, LANES), BF16)
        kv = kv_ref[...].astype(F32)
        kvar = _kv_variants(_rope(kv[:, :LANES], cos_ref[...], sin_ref[...]) * scale)
        vvar = _kv_variants(kv[:, LANES:])
        cvar = _kv_variants(ck_ref[...] * scale)
        dvar = _kv_variants(cv_ref[...])
        for g in range(ATT_KV_HEADS):
            for half in range(2):
                i = 2 * g + half
                for ref, var in ((kl_ref, kvar), (vl_ref, vvar)):
                    ref[i, 0:WINDOW, :] = pad
                    ref[i, WINDOW:WINDOW + t, :] = var[g][half]
                    ref[i, WINDOW + t:, :] = pad
                kc_ref[i] = cvar[g][half]
                vc_ref[i] = dvar[g][half]

    r0 = pl.multiple_of(j * tq, tq)
    nloc = tq + 2 * WINDOW
    rr = lax.broadcasted_iota(jnp.int32, (tq, nloc), 0)
    ss = lax.broadcasted_iota(jnp.int32, (tq, nloc), 1)
    band = ((ss - rr >= 0) & (ss - rr <= 2 * WINDOW)
            & (ss >= WINDOW - j * tq) & (ss < t + WINDOW - j * tq))
    cs = cos_ref[pl.ds(r0, tq), :]
    sn = sin_ref[pl.ds(r0, tq), :]
    for p in range(ATT_Q_HEADS // 2):
        g = p // (ATT_Q_HEADS // ATT_KV_HEADS // 2)
        sl = slice(p * LANES, (p + 1) * LANES)
        qp = _rope(qz_ref[:, sl].astype(F32), cs, sn).astype(BF16)
        keys = [[kl_ref[2 * g + half, pl.ds(r0, nloc), :], kc_ref[2 * g + half]] for half in range(2)]
        vals = [[vl_ref[2 * g + half, pl.ds(r0, nloc), :], vc_ref[2 * g + half]] for half in range(2)]
        yc_ref[:, sl] = _attend_pair(
            qp, keys, vals, [band, None], [sink_ref[layer, 2 * p], sink_ref[layer, 2 * p + 1]],
            qz_ref[:, ATT_W + p * LANES:ATT_W + (p + 1) * LANES].astype(F32))


def _lat_attention(ua, cache_k, cache_v, layer, sink, rope_tabs):
    b, t, _ = ua.shape
    past = cache_k.shape[2]
    tq = 256
    nvar = 2 * ATT_KV_HEADS
    cache_spec = pl.BlockSpec((None, None, past, ATT_KV_W), lambda i, j: (i, layer, 0, 0))
    return pl.pallas_call(
        functools.partial(_lat_attn_kernel, t=t, tq=tq, layer=layer),
        grid=(b, t // tq),
        in_specs=[pl.BlockSpec(memory_space=pltpu.SMEM),
                  pl.BlockSpec((None, tq, 2 * ATT_W), lambda i, j: (i, j, 0)),
                  pl.BlockSpec((None, t, 2 * ATT_KV_W), lambda i, j: (i, 0, 2 * ATT_W // (2 * ATT_KV_W))),
                  cache_spec, cache_spec,
                  _const_spec((t, LANES)), _const_spec((t, LANES))],
        out_specs=pl.BlockSpec((None, tq, ATT_W), lambda i, j: (i, j, 0)),
        out_shape=jax.ShapeDtypeStruct((b, t, ATT_W), BF16),
        scratch_shapes=[pltpu.VMEM((nvar, t + 2 * WINDOW, LANES), BF16),
                        pltpu.VMEM((nvar, t + 2 * WINDOW, LANES), BF16),
                        pltpu.VMEM((nvar, past, LANES), BF16),
                        pltpu.VMEM((nvar, past, LANES), BF16)],
        compiler_params=_params(("parallel", "arbitrary")),
        name="lat_attention",
    )(sink, ua, ua, cache_k, cache_v, *rope_tabs)


def _outproj_kernel(x_ref, ya_ref, yb_ref, yc_ref, mg_ref, mod_ref, g_ref,
                    wa_ref, wb_ref, wc_ref, wo_ref, o_ref, *, tm, row0, rows_per_mod):
    row = _mod_row(pl.program_id(0), tm, row0, rows_per_mod)
    merged = None
    for c, (y_ref, w_ref) in enumerate(((ya_ref, wa_ref), (yb_ref, wb_ref), (yc_ref, wc_ref))):
        gate = _sigmoid(mg_ref[:, c * D_MODEL:(c + 1) * D_MODEL].astype(F32))
        term = gate * jnp.dot(y_ref[...], w_ref[...], preferred_element_type=F32)
        merged = term if merged is None else merged + term
    out = jnp.dot(merged.astype(BF16), wo_ref[...], preferred_element_type=F32)
    ms = jnp.mean(out * out, axis=-1, keepdims=True)
    normed = out * lax.rsqrt(ms + EPS) * g_ref[...]
    gate = mod_ref[pl.ds(row, 1), 2 * D_MODEL:3 * D_MODEL]
    o_ref[...] = x_ref[...] + gate * normed


def _outproj(x2, ya, yb, yc, mg, layer, mod, g_post, wa, wb, wc, wo, *, row0, rows_per_mod):
    n = x2.shape[0]
    tm = 512
    row_spec = lambda w: pl.BlockSpec((tm, w), lambda i: (i, 0))
    return pl.pallas_call(
        functools.partial(_outproj_kernel, tm=tm, row0=row0, rows_per_mod=rows_per_mod),
        grid=(n // tm,),
        in_specs=[row_spec(D_MODEL), row_spec(FOURIER_W), row_spec(RET_W), row_spec(ATT_W),
                  row_spec(3 * D_MODEL),
                  _layer_spec((MOD_ROWS, 3 * D_MODEL), layer), _layer_spec((1, D_MODEL), layer),
                  _layer_spec((FOURIER_W, D_MODEL), layer), _layer_spec((RET_W, D_MODEL), layer),
                  _layer_spec((ATT_W, D_MODEL), layer), _layer_spec((D_MODEL, D_MODEL), layer)],
        out_specs=row_spec(D_MODEL),
        out_shape=jax.ShapeDtypeStruct((n, D_MODEL), F32),
        compiler_params=_params(("parallel",)),
        name="outproj",
    )(x2, ya, yb, yc, mg, mod, g_post, wa, wb, wc, wo)


def _layer(x, layer, mod, p, *, latent, dft_chan, dft_pos, rope_tabs=None, cache_k=None, cache_v=None,
           state=None):
    b, t, _ = x.shape
    n = b * t
    row0, rows_per_mod = (1, t) if latent else (0, n)
    x2 = x.reshape(n, D_MODEL)
    xcs, fz, ur, ua, mg, k32, v32 = _inproj(x2, layer, mod, p["g_pre"], p["w_in"], dft_chan,
                                            row0=row0, rows_per_mod=rows_per_mod)
    ya = _fourier(xcs.reshape(b, t, -1), fz.reshape(b, t, -1), dft_pos, p["w_four"], layer,
                  bg=4 if latent else 8, tq=min(t, 512))
    ur3 = ur.reshape(b, t, -1)
    ua3 = ua.reshape(b, t, -1)
    if latent:
        yb, s_fin = _retention(ur3, layer, p["dec_qk"], p["dec_att"], p["gn"], rope_tabs=rope_tabs,
                               state=state)
        yc = _lat_attention(ua3, cache_k, cache_v, layer, p["sink"], rope_tabs)
    else:
        yb, s_fin = _retention(ur3, layer, p["dec_qk"], p["dec_att"], p["gn"])
        yc = _ctx_attention(ua3, layer, p["sink"])
    out = _outproj(x2, ya.reshape(n, -1), yb.reshape(n, -1), yc.reshape(n, -1), mg, layer, mod,
                   p["g_post"], p["w_pa"], p["w_pb"], p["w_pc"], p["w_out"],
                   row0=row0, rows_per_mod=rows_per_mod)
    return out.reshape(b, t, D_MODEL), k32, v32, s_fin


def kernel(x_prompt, x_sample, cache_k, cache_v, state_ret, c, c_ctx, w_mod, b_mod, g_pre, g_post, w_in,
           w_four, ret_decay, ret_gn, attn_sink, w_branch_a, w_branch_b, w_branch_c, w_out):
    batch, seq, _ = x_prompt.shape
    dec_batch, dec_seq, _ = x_sample.shape
    past = cache_k.shape[2]
    assert 1 + dec_batch <= MOD_ROWS

    cv = jnp.zeros((MOD_ROWS, D_MODEL), F32).at[0].set(c_ctx).at[1:1 + dec_batch].set(c)
    mod = _modulation(cv, w_mod, b_mod)

    dft_chan, dft_ctx = _dft_tables(seq)
    _, dft_lat = _dft_tables(dec_seq)
    rope_tabs = _rope_tables(dec_seq)
    ck = cache_k.reshape(dec_batch, DEPTH, past, ATT_KV_W)
    cvv = cache_v.reshape(dec_batch, DEPTH, past, ATT_KV_W)

    p = dict(
        g_pre=g_pre.reshape(DEPTH, 1, D_MODEL), g_post=g_post.reshape(DEPTH, 1, D_MODEL),
        w_in=w_in.astype(BF16), w_four=w_four.astype(BF16),
        w_pa=w_branch_a.astype(BF16), w_pb=w_branch_b.astype(BF16),
        w_pc=w_branch_c.astype(BF16), w_out=w_out.astype(BF16),
        dec_qk=jnp.repeat(ret_decay, RET_DK, axis=2),
        dec_att=jnp.repeat(ret_decay, RET_CHUNK, axis=2),
        gn=ret_gn.reshape(DEPTH, 1, RET_W),
        sink=attn_sink.reshape(DEPTH, ATT_Q_HEADS))

    xp = x_prompt
    ks, vs, ss = [], [], []
    for l in range(DEPTH):
        xp, k32, v32, s_fin = _layer(xp, l, mod, p, latent=False, dft_chan=dft_chan, dft_pos=dft_ctx)
        ks.append(k32.reshape(batch, seq, ATT_KV_HEADS, HEAD_DIM))
        vs.append(v32.reshape(batch, seq, ATT_KV_HEADS, HEAD_DIM))
        ss.append(s_fin)

    xs = x_sample
    for l in range(DEPTH):
        xs, _, _, _ = _layer(xs, l, mod, p, latent=True, dft_chan=dft_chan, dft_pos=dft_lat,
                             rope_tabs=rope_tabs, cache_k=ck, cache_v=cvv, state=state_ret)

    return (xp, xs, jnp.stack(ks, axis=1), jnp.stack(vs, axis=1), jnp.stack(ss, axis=1))
```

```python
import functools
import math

import numpy as np
import jax
import jax.numpy as jnp
from jax import lax
from jax.experimental import pallas as pl
from jax.experimental.pallas import tpu as pltpu

F32 = jnp.float32
BF16 = jnp.bfloat16

D_MODEL = 1024
DEPTH = 2
GRID_W = 64
HEAD_DIM = 64
FOURIER_GROUPS = 4
FOURIER_GROUP_W = 64
FOURIER_W = FOURIER_GROUPS * FOURIER_GROUP_W
RET_HEADS = 4
RET_DK = 64
RET_W = RET_HEADS * RET_DK
RET_CHUNK = 128
ATT_Q_HEADS = 8
ATT_KV_HEADS = 2
ATT_W = ATT_Q_HEADS * HEAD_DIM
ATT_KV_W = ATT_KV_HEADS * HEAD_DIM
WINDOW = 128
ROPE_BASE = 10000.0
EPS = 1e-6
MOD_ROWS = 8
LANES = 128
NEG = -1e30
LOG2E = math.log2(math.e)
VMEM_LIMIT = 56 * 1024 * 1024

C_FX, C_RQ, C_RZ_END = 0, 512, 1536
C_AQ, C_AK, C_AV, C_AZ, C_MG, C_END = 1536, 2048, 2176, 2304, 2816, 5888

TM_INPROJ = 1024
TM_OUTPROJ = 512
TQ_LATENT = 256
TQ_FOURIER = 512
NB_CTX_RET = 4
NB_CTX_ATTN = 4
BG_CTX_FOURIER = 8
RET_UNROLL = 4


def _sigmoid(x):
    return 0.5 * jnp.tanh(0.5 * x) + 0.5


def _silu(x):
    return x * _sigmoid(x)


def _params(sem):
    return pltpu.CompilerParams(dimension_semantics=sem, vmem_limit_bytes=VMEM_LIMIT)


def _const_spec(shape):
    nd = len(shape)
    return pl.BlockSpec(shape, lambda *_: (0,) * nd, pipeline_mode=pl.Buffered(1))


def _layer_spec(shape, layer):
    nd = len(shape)
    return pl.BlockSpec((None,) + tuple(shape), lambda *_: (layer,) + (0,) * nd,
                        pipeline_mode=pl.Buffered(1))


def _dft_tables(t):
    c = np.arange(FOURIER_GROUP_W)
    ang = 2.0 * np.pi * ((c[:, None] * c[None, :]) % FOURIER_GROUP_W) / FOURIER_GROUP_W
    eye = np.eye(FOURIER_GROUPS)
    s64 = FOURIER_GROUP_W ** -0.5
    chan = np.concatenate([np.kron(eye, np.cos(ang) * s64), np.kron(eye, np.sin(ang) * s64)], axis=1)
    p = np.arange(t)
    angt = 2.0 * np.pi * ((p[:, None] * p[None, :]) % t) / t
    pos = np.concatenate([np.cos(angt), -np.sin(angt)], axis=1) * (t ** -0.5)
    return jnp.asarray(chan, F32).astype(BF16), jnp.asarray(pos, F32).astype(BF16)


def _rope_tables(t):
    quarter = HEAD_DIM // 4
    lane = np.arange(LANES) % HEAD_DIM
    inv = ROPE_BASE ** (-(lane % quarter).astype(np.float64) / quarter)
    n = np.arange(t)
    pos = np.where(lane[None, :] < HEAD_DIM // 2, (n // GRID_W)[:, None], (n % GRID_W)[:, None])
    ang = pos.astype(np.float64) * inv[None, :]
    sign = np.where((lane % (2 * quarter)) < quarter, -1.0, 1.0)
    return jnp.asarray(np.cos(ang), F32), jnp.asarray(np.sin(ang) * sign[None, :], F32)


def _rope(x, cos, sin):
    lane = lax.broadcasted_iota(jnp.int32, x.shape, 1)
    first = (lane & 31) < 16
    partner = jnp.where(first, pltpu.roll(x, LANES - 16, axis=1), pltpu.roll(x, 16, axis=1))
    return x * cos + partner * sin


def _mod_kernel(cv_ref, w_ref, b_ref, o_ref):
    a = _silu(cv_ref[...])
    o_ref[...] = jnp.dot(a, w_ref[...], preferred_element_type=F32,
                         precision=lax.Precision.HIGHEST) + b_ref[...]


def _modulation(cv, w_mod, b_mod):
    tn = 1024
    return pl.pallas_call(
        _mod_kernel,
        grid=(DEPTH, 3 * D_MODEL // tn),
        in_specs=[pl.BlockSpec((MOD_ROWS, D_MODEL), lambda l, j: (0, 0)),
                  pl.BlockSpec((None, D_MODEL, tn), lambda l, j: (l, 0, j)),
                  pl.BlockSpec((None, 1, tn), lambda l, j: (l, 0, j))],
        out_specs=pl.BlockSpec((None, MOD_ROWS, tn), lambda l, j: (l, 0, j)),
        out_shape=jax.ShapeDtypeStruct((DEPTH, MOD_ROWS, 3 * D_MODEL), F32),
        compiler_params=_params(("parallel", "parallel")),
        name="modulation",
    )(cv, w_mod, b_mod.reshape(DEPTH, 1, 3 * D_MODEL))


def _modulated_norm(x, g, mod_ref, tm, row0, rows_per_mod):
    row = row0 + (pl.program_id(0) * tm) // rows_per_mod
    ms = jnp.mean(x * x, axis=-1, keepdims=True)
    y = x * lax.rsqrt(ms + EPS) * g
    shift = mod_ref[pl.ds(row, 1), 0:D_MODEL]
    scale = mod_ref[pl.ds(row, 1), D_MODEL:2 * D_MODEL]
    return (y * (1.0 + scale) + shift).astype(BF16), row


def _inproj_kernel(x_ref, mod_ref, g_ref, w_ref, dft_ref,
                   xcs_ref, fz_ref, ur_ref, ua_ref, k_ref, v_ref, *, tm, row0, rows_per_mod):
    h, _ = _modulated_norm(x_ref[...], g_ref[...], mod_ref, tm, row0, rows_per_mod)

    def mm(c0, c1):
        return jnp.dot(h, w_ref[:, c0:c1], preferred_element_type=F32)

    f = mm(C_FX, C_RQ)
    xcs_ref[...] = jnp.dot(f[:, :FOURIER_W].astype(BF16), dft_ref[...],
                           preferred_element_type=F32).astype(BF16)
    fz_ref[...] = f[:, FOURIER_W:].astype(BF16)
    ur_ref[...] = mm(C_RQ, C_RZ_END).astype(BF16)
    ua_ref[:, 0:ATT_W] = mm(C_AQ, C_AK).astype(BF16)
    ua_ref[:, ATT_W:2 * ATT_W] = mm(C_AZ, C_MG).astype(BF16)
    kv = mm(C_AK, C_AZ)
    k_ref[...] = kv[:, :ATT_KV_W]
    v_ref[...] = kv[:, ATT_KV_W:]
    ua_ref[:, 2 * ATT_W:] = kv.astype(BF16)


def _inproj(x2, layer, mod, g_pre, w_main, dft, *, row0, rows_per_mod):
    n = x2.shape[0]
    tm = TM_INPROJ
    row_spec = lambda w: pl.BlockSpec((tm, w), lambda i: (i, 0))
    widths = (2 * FOURIER_W, FOURIER_W, 4 * RET_W, 2 * ATT_W + 2 * ATT_KV_W)
    out_shape = [jax.ShapeDtypeStruct((n, w), BF16) for w in widths]
    out_shape += [jax.ShapeDtypeStruct((n, ATT_KV_W), F32)] * 2
    return pl.pallas_call(
        functools.partial(_inproj_kernel, tm=tm, row0=row0, rows_per_mod=rows_per_mod),
        grid=(n // tm,),
        in_specs=[row_spec(D_MODEL),
                  _layer_spec((MOD_ROWS, 3 * D_MODEL), layer),
                  _layer_spec((1, D_MODEL), layer),
                  _layer_spec((D_MODEL, C_MG), layer),
                  _const_spec((FOURIER_W, 2 * FOURIER_W))],
        out_specs=[row_spec(w) for w in widths] + [row_spec(ATT_KV_W)] * 2,
        out_shape=out_shape,
        compiler_params=_params(("parallel",)),
        name="inproj",
    )(x2, mod, g_pre, w_main, dft)


def _fourier_kernel(ct_ref, xcs_ref, fz_ref, w_ref, ya_ref, xcat_ref, *, bg, t):
    @pl.when(pl.program_id(1) == 0)
    def _gather():
        for b in range(bg):
            cols = slice(b * FOURIER_W, (b + 1) * FOURIER_W)
            xcat_ref[0:t, cols] = xcs_ref[b, :, 0:FOURIER_W]
            xcat_ref[t:2 * t, cols] = xcs_ref[b, :, FOURIER_W:]

    yr = jnp.dot(ct_ref[...], xcat_ref[...], preferred_element_type=F32).astype(BF16)
    for b in range(bg):
        ya = jnp.dot(yr[:, b * FOURIER_W:(b + 1) * FOURIER_W], w_ref[...], preferred_element_type=F32)
        ya_ref[b] = (ya * _silu(fz_ref[b].astype(F32))).astype(BF16)


def _fourier(xcs, fz, ct, w_four, layer, *, bg, tq):
    b, t, _ = xcs.shape
    return pl.pallas_call(
        functools.partial(_fourier_kernel, bg=bg, t=t),
        grid=(b // bg, t // tq),
        in_specs=[pl.BlockSpec((tq, 2 * t), lambda i, j: (j, 0)),
                  pl.BlockSpec((bg, t, 2 * FOURIER_W), lambda i, j: (i, 0, 0)),
                  pl.BlockSpec((bg, tq, FOURIER_W), lambda i, j: (i, j, 0)),
                  _layer_spec((FOURIER_W, FOURIER_W), layer)],
        out_specs=pl.BlockSpec((bg, tq, FOURIER_W), lambda i, j: (i, j, 0)),
        out_shape=jax.ShapeDtypeStruct((b, t, FOURIER_W), BF16),
        scratch_shapes=[pltpu.VMEM((2 * t, bg * FOURIER_W), BF16)],
        compiler_params=_params(("parallel", "arbitrary")),
        name="fourier",
    )(ct, xcs, fz, w_four)


def _log_sigmoid(x):
    return jnp.minimum(x, 0.0) - jnp.log(1.0 + jnp.exp(-jnp.abs(x)))


def _head_blocks(a, width):
    lane = lax.broadcasted_iota(jnp.int32, a.shape, 1)
    zero = jnp.zeros_like(a)
    return jnp.concatenate(
        [jnp.where((lane >= h * width) & (lane < (h + 1) * width), a, zero) for h in range(RET_HEADS)],
        axis=0)


def _ret_kernel(*refs, nb, t, rope, has_s0):
    refs = list(refs)
    ur_ref = refs.pop(0)
    cos_ref = refs.pop(0) if rope else None
    sin_ref = refs.pop(0) if rope else None
    s0_ref = refs.pop(0) if has_s0 else None
    (dec_qk_ref, dec_att_ref, gn_ref, yb_ref, sfin_ref,
     kr_ref, dsf_ref, dsb_ref, sfs_ref, sbs_ref, st_ref) = refs
    c = RET_CHUNK
    nc = t // c
    nch = nb * nc

    lgf = _log_sigmoid(dec_qk_ref[0:1, :])
    lgb = _log_sigmoid(dec_qk_ref[1:2, :])
    ri = lax.broadcasted_iota(jnp.int32, (c, RET_W), 0).astype(F32)
    read_f = jnp.exp((ri + 1.0) * lgf)
    read_b = jnp.exp((c - ri) * lgb)
    write_f = jnp.exp((c - 1.0 - ri) * lgf)
    write_b = jnp.exp(ri * lgb)
    carry_f = jnp.exp(c * lgf)
    carry_b = jnp.exp(c * lgb)
    lgf_a = _log_sigmoid(dec_att_ref[0:1, :])
    lgb_a = _log_sigmoid(dec_att_ref[1:2, :])
    ii = lax.broadcasted_iota(jnp.int32, (c, RET_HEADS * c), 0)
    jj = lax.broadcasted_iota(jnp.int32, (c, RET_HEADS * c), 1) & (c - 1)
    diff = (ii - jj).astype(F32)
    decay = (jnp.where(diff >= 0, jnp.exp(jnp.maximum(diff, 0.0) * lgf_a), 0.0)
             + jnp.where(diff <= 0, jnp.exp(jnp.maximum(-diff, 0.0) * lgb_a), 0.0))
    r2 = lax.broadcasted_iota(jnp.int32, (RET_W, RET_W), 0)
    c2 = lax.broadcasted_iota(jnp.int32, (RET_W, RET_W), 1)
    same_head = (r2 // RET_DK) == (c2 // RET_DK)
    group_mean = jnp.where(same_head, 1.0 / RET_DK, 0.0).astype(BF16)

    def chunk_pos(ci):
        return ci // nc, pl.multiple_of((ci % nc) * c, c)

    def load_qk(col, b, r0):
        a = ur_ref[b, pl.ds(r0, c), col:col + RET_W].astype(F32)
        if rope:
            cs = cos_ref[pl.ds(r0, c), :]
            sn = sin_ref[pl.ds(r0, c), :]
            a = jnp.concatenate([_rope(a[:, :LANES], cs, sn), _rope(a[:, LANES:], cs, sn)], axis=1)
        return a

    def increments(ci, carry):
        b, r0 = chunk_pos(ci)
        k = load_qk(RET_W, b, r0) * (RET_DK ** -0.5)
        v = ur_ref[b, pl.ds(r0, c), 2 * RET_W:3 * RET_W]
        kr_ref[pl.ds(pl.multiple_of(ci * c, c), c), :] = k.astype(BF16)
        kw = jnp.concatenate([(k * write_f).astype(BF16), (k * write_b).astype(BF16)], axis=1)
        d = lax.dot_general(kw, v, (((0,), (0,)), ((), ())), preferred_element_type=F32)
        dsf_ref[ci] = jnp.where(same_head, d[:RET_W], 0.0)
        dsb_ref[ci] = jnp.where(same_head, d[RET_W:], 0.0)
        return carry

    lax.fori_loop(0, nch, increments, 0, unroll=RET_UNROLL)

    def scan(b, direction, ds_ref, out_ref, carry_decay):
        st_ref[...] = jnp.zeros((RET_W, RET_W), F32)
        if has_s0:
            for h in range(RET_HEADS):
                sl = slice(h * RET_DK, (h + 1) * RET_DK)
                st_ref[sl, sl] = s0_ref[b, direction, h]

        def step(n, carry):
            ci = b * nc + (n if direction == 0 else nc - 1 - n)
            out_ref[ci] = st_ref[...].astype(BF16)
            st_ref[...] = carry_decay * st_ref[...] + ds_ref[ci]
            return carry

        lax.fori_loop(0, nc, step, 0, unroll=min(nc, 4))
        for h in range(RET_HEADS):
            sl = slice(h * RET_DK, (h + 1) * RET_DK)
            sfin_ref[b, direction, h] = st_ref[sl, sl]

    for b in range(nb):
        scan(b, 0, dsf_ref, sfs_ref, carry_f)
        scan(b, 1, dsb_ref, sbs_ref, carry_b)

    def outputs(ci, carry):
        b, r0 = chunk_pos(ci)
        q = load_qk(0, b, r0)
        kb = kr_ref[pl.ds(pl.multiple_of(ci * c, c), c), :]
        v = ur_ref[b, pl.ds(r0, c), 2 * RET_W:3 * RET_W]
        att = lax.dot_general(q.astype(BF16), _head_blocks(kb, RET_DK),
                              (((1,), (1,)), ((), ())), preferred_element_type=F32)
        o = jnp.dot((att * decay).astype(BF16), _head_blocks(v, RET_DK), preferred_element_type=F32)
        o = o + jnp.dot((q * read_f).astype(BF16), sfs_ref[ci], preferred_element_type=F32)
        o = o + jnp.dot((q * read_b).astype(BF16), sbs_ref[ci], preferred_element_type=F32)
        sq = o * o
        hi = sq.astype(BF16)
        lo = (sq - hi.astype(F32)).astype(BF16)
        ms = (jnp.dot(hi, group_mean, preferred_element_type=F32)
              + jnp.dot(lo, group_mean, preferred_element_type=F32))
        y = o * lax.rsqrt(ms + EPS) * gn_ref[...]
        z = ur_ref[b, pl.ds(r0, c), 3 * RET_W:4 * RET_W].astype(F32)
        yb_ref[b, pl.ds(r0, c), :] = (y * _silu(z)).astype(BF16)
        return carry

    lax.fori_loop(0, nch, outputs, 0, unroll=RET_UNROLL)


def _retention(ur, layer, dec_qk, dec_att, gn, *, nb, rope_tabs=None, state=None):
    b, t, _ = ur.shape
    nch = nb * (t // RET_CHUNK)
    rope = rope_tabs is not None
    has_s0 = state is not None
    in_specs = [pl.BlockSpec((nb, t, 4 * RET_W), lambda i: (i, 0, 0))]
    args = [ur]
    if rope:
        in_specs += [_const_spec((t, LANES)), _const_spec((t, LANES))]
        args += list(rope_tabs)
    if has_s0:
        in_specs.append(pl.BlockSpec((nb, None, 2, RET_HEADS, RET_DK, RET_DK),
                                     lambda i: (i, layer, 0, 0, 0, 0)))
        args.append(state)
    in_specs += [_layer_spec((2, RET_W), layer), _layer_spec((2, RET_HEADS * RET_CHUNK), layer),
                 _layer_spec((1, RET_W), layer)]
    args += [dec_qk, dec_att, gn]
    state_scratch = lambda dt: pltpu.VMEM((nch, RET_W, RET_W), dt)
    return pl.pallas_call(
        functools.partial(_ret_kernel, nb=nb, t=t, rope=rope, has_s0=has_s0),
        grid=(b // nb,),
        in_specs=in_specs,
        out_specs=[pl.BlockSpec((nb, t, RET_W), lambda i: (i, 0, 0)),
                   pl.BlockSpec((nb, 2, RET_HEADS, RET_DK, RET_DK), lambda i: (i, 0, 0, 0, 0))],
        out_shape=[jax.ShapeDtypeStruct((b, t, RET_W), BF16),
                   jax.ShapeDtypeStruct((b, 2, RET_HEADS, RET_DK, RET_DK), F32)],
        scratch_shapes=[pltpu.VMEM((nb * t, RET_W), BF16),
                        state_scratch(F32), state_scratch(F32), state_scratch(BF16), state_scratch(BF16),
                        pltpu.VMEM((RET_W, RET_W), F32)],
        compiler_params=_params(("parallel",)),
        name="retention",
    )(*args)


def _kv_variants(a, ones_block):
    lane_half = lax.broadcasted_iota(jnp.int32, a.shape, 1) // HEAD_DIM
    swapped = pltpu.roll(a, HEAD_DIM, axis=1)
    out = []
    for g in range(ATT_KV_HEADS):
        row = []
        for half in range(2):
            var = jnp.where(lane_half == half, a if half == g else swapped, 0.0).astype(BF16)
            if ones_block:
                ones = jnp.where(lane_half == half, 1.0, 0.0).astype(BF16)
                var = jnp.concatenate([var, ones], axis=1)
            row.append(var)
        out.append(row)
    return out


def _attend_group(q2, keys, values, masks, sink_ref, layer, g, z2):
    rows = q2.shape[0]
    upper = lax.broadcasted_iota(jnp.int32, (rows, 1), 0) < rows // 2
    acc = None
    sink_terms = []
    for half in range(2):
        sink = jnp.where(upper, sink_ref[layer, 4 * g + half], sink_ref[layer, 4 * g + 2 + half]) * LOG2E
        logits = []
        for kpart, mask in zip(keys[half], masks):
            l = lax.dot_general(q2, kpart, (((1,), (1,)), ((), ())), preferred_element_type=F32)
            logits.append(l if mask is None else jnp.where(mask, l, NEG))
        m = sink
        for l in logits:
            m = jnp.maximum(m, jnp.max(l, axis=-1, keepdims=True))
        for l, vpart in zip(logits, values[half]):
            pv = jnp.dot(jnp.exp2(l - m).astype(BF16), vpart, preferred_element_type=F32)
            acc = pv if acc is None else acc + pv
        sink_terms.append(jnp.exp2(sink - m))
    lane = lax.broadcasted_iota(jnp.int32, (rows, LANES), 1)
    den = acc[:, LANES:] + jnp.where(lane < HEAD_DIM, sink_terms[0], sink_terms[1])
    return (acc[:, :LANES] * (1.0 / den) * _silu(z2)).astype(BF16)


def _pair_rows(ref_slice, g):
    return jnp.concatenate([ref_slice(2 * g), ref_slice(2 * g + 1)], axis=0)


def _ctx_attn_kernel(sink_ref, ua_ref, yc_ref, *, nb, t, layer):
    for b in range(nb):
        kv = ua_ref[b, :, 2 * ATT_W:].astype(F32)
        kvar = _kv_variants(kv[:, :LANES] * (HEAD_DIM ** -0.5 * LOG2E), False)
        vvar = _kv_variants(kv[:, LANES:], True)
        for g in range(ATT_KV_HEADS):
            q2 = _pair_rows(lambda p: ua_ref[b, :, p * LANES:(p + 1) * LANES], g)
            z2 = _pair_rows(lambda p: ua_ref[b, :, ATT_W + p * LANES:ATT_W + (p + 1) * LANES], g)
            o = _attend_group(q2, [[kvar[g][0]], [kvar[g][1]]], [[vvar[g][0]], [vvar[g][1]]], [None],
                              sink_ref, layer, g, z2.astype(F32))
            yc_ref[b, :, 2 * g * LANES:(2 * g + 1) * LANES] = o[:t]
            yc_ref[b, :, (2 * g + 1) * LANES:(2 * g + 2) * LANES] = o[t:]


def _ctx_attention(ua, layer, sink):
    b, t, w = ua.shape
    nb = NB_CTX_ATTN
    return pl.pallas_call(
        functools.partial(_ctx_attn_kernel, nb=nb, t=t, layer=layer),
        grid=(b // nb,),
        in_specs=[pl.BlockSpec(memory_space=pltpu.SMEM),
                  pl.BlockSpec((nb, t, w), lambda i: (i, 0, 0))],
        out_specs=pl.BlockSpec((nb, t, ATT_W), lambda i: (i, 0, 0)),
        out_shape=jax.ShapeDtypeStruct((b, t, ATT_W), BF16),
        compiler_params=_params(("parallel",)),
        name="ctx_attention",
    )(sink, ua)


def _lat_attn_kernel(sink_ref, qz_ref, kv_ref, ck_ref, cv_ref, cos_ref, sin_ref, yc_ref,
                     kl_ref, vl_ref, kc_ref, vc_ref, *, t, tq, layer):
    j = pl.program_id(1)
    scale = HEAD_DIM ** -0.5 * LOG2E

    @pl.when(j == 0)
    def _prepare():
        kv = kv_ref[...].astype(F32)
        kvar = _kv_variants(_rope(kv[:, :LANES], cos_ref[...], sin_ref[...]) * scale, False)
        vvar = _kv_variants(kv[:, LANES:], True)
        cvar = _kv_variants(ck_ref[...] * scale, False)
        dvar = _kv_variants(cv_ref[...], True)
        for g in range(ATT_KV_HEADS):
            for half in range(2):
                i = 2 * g + half
                for ref, var in ((kl_ref, kvar), (vl_ref, vvar)):
                    pad = jnp.zeros((WINDOW, ref.shape[-1]), BF16)
                    ref[i, 0:WINDOW, :] = pad
                    ref[i, WINDOW:WINDOW + t, :] = var[g][half]
                    ref[i, WINDOW + t:, :] = pad
                kc_ref[i] = cvar[g][half]
                vc_ref[i] = dvar[g][half]

    r0 = pl.multiple_of(j * tq, tq)
    nloc = tq + 2 * WINDOW
    rr = lax.broadcasted_iota(jnp.int32, (2 * tq, nloc), 0) & (tq - 1)
    ss = lax.broadcasted_iota(jnp.int32, (2 * tq, nloc), 1)
    band = ((ss - rr >= 0) & (ss - rr <= 2 * WINDOW)
            & (ss >= WINDOW - j * tq) & (ss < t + WINDOW - j * tq))
    cs = cos_ref[pl.ds(r0, tq), :]
    sn = sin_ref[pl.ds(r0, tq), :]
    for g in range(ATT_KV_HEADS):
        q2 = _pair_rows(
            lambda p: _rope(qz_ref[:, p * LANES:(p + 1) * LANES].astype(F32), cs, sn).astype(BF16), g)
        z2 = _pair_rows(lambda p: qz_ref[:, ATT_W + p * LANES:ATT_W + (p + 1) * LANES], g)
        keys = [[kl_ref[2 * g + half, pl.ds(r0, nloc), :], kc_ref[2 * g + half]] for half in range(2)]
        vals = [[vl_ref[2 * g + half, pl.ds(r0, nloc), :], vc_ref[2 * g + half]] for half in range(2)]
        o = _attend_group(q2, keys, vals, [band, None], sink_ref, layer, g, z2.astype(F32))
        yc_ref[:, 2 * g * LANES:(2 * g + 1) * LANES] = o[:tq]
        yc_ref[:, (2 * g + 1) * LANES:(2 * g + 2) * LANES] = o[tq:]


def _lat_attention(ua, cache_k, cache_v, layer, sink, rope_tabs):
    b, t, _ = ua.shape
    past = cache_k.shape[2]
    tq = TQ_LATENT
    nvar = 2 * ATT_KV_HEADS
    cache_spec = pl.BlockSpec((None, None, past, ATT_KV_W), lambda i, j: (i, layer, 0, 0))
    return pl.pallas_call(
        functools.partial(_lat_attn_kernel, t=t, tq=tq, layer=layer),
        grid=(b, t // tq),
        in_specs=[pl.BlockSpec(memory_space=pltpu.SMEM),
                  pl.BlockSpec((None, tq, 2 * ATT_W), lambda i, j: (i, j, 0)),
                  pl.BlockSpec((None, t, 2 * ATT_KV_W), lambda i, j: (i, 0, 2 * ATT_W // (2 * ATT_KV_W))),
                  cache_spec, cache_spec,
                  _const_spec((t, LANES)), _const_spec((t, LANES))],
        out_specs=pl.BlockSpec((None, tq, ATT_W), lambda i, j: (i, j, 0)),
        out_shape=jax.ShapeDtypeStruct((b, t, ATT_W), BF16),
        scratch_shapes=[pltpu.VMEM((nvar, t + 2 * WINDOW, LANES), BF16),
                        pltpu.VMEM((nvar, t + 2 * WINDOW, 2 * LANES), BF16),
                        pltpu.VMEM((nvar, past, LANES), BF16),
                        pltpu.VMEM((nvar, past, 2 * LANES), BF16)],
        compiler_params=_params(("parallel", "arbitrary")),
        name="lat_attention",
    )(sink, ua, ua, cache_k, cache_v, *rope_tabs)


def _outproj_kernel(x_ref, ya_ref, yb_ref, yc_ref, mod_ref, gpre_ref, gpost_ref,
                    wmg_ref, wa_ref, wb_ref, wc_ref, wo_ref, o_ref, *, tm, row0, rows_per_mod):
    x = x_ref[...]
    h, row = _modulated_norm(x, gpre_ref[...], mod_ref, tm, row0, rows_per_mod)
    merged = None
    for c, (y_ref, w_ref) in enumerate(((ya_ref, wa_ref), (yb_ref, wb_ref), (yc_ref, wc_ref))):
        mg = jnp.dot(h, wmg_ref[:, c * D_MODEL:(c + 1) * D_MODEL], preferred_element_type=F32)
        term = _sigmoid(mg) * jnp.dot(y_ref[...], w_ref[...], preferred_element_type=F32)
        merged = term if merged is None else merged + term
    out = jnp.dot(merged.astype(BF16), wo_ref[...], preferred_element_type=F32)
    ms = jnp.mean(out * out, axis=-1, keepdims=True)
    normed = out * lax.rsqrt(ms + EPS) * gpost_ref[...]
    gate = mod_ref[pl.ds(row, 1), 2 * D_MODEL:3 * D_MODEL]
    o_ref[...] = x + gate * normed


def _outproj(x2, ya, yb, yc, layer, mod, g_pre, g_post, w_mg, wa, wb, wc, wo, *, row0, rows_per_mod):
    n = x2.shape[0]
    tm = TM_OUTPROJ
    row_spec = lambda w: pl.BlockSpec((tm, w), lambda i: (i, 0))
    return pl.pallas_call(
        functools.partial(_outproj_kernel, tm=tm, row0=row0, rows_per_mod=rows_per_mod),
        grid=(n // tm,),
        in_specs=[row_spec(D_MODEL), row_spec(FOURIER_W), row_spec(RET_W), row_spec(ATT_W),
                  _layer_spec((MOD_ROWS, 3 * D_MODEL), layer),
                  _layer_spec((1, D_MODEL), layer), _layer_spec((1, D_MODEL), layer),
                  _layer_spec((D_MODEL, 3 * D_MODEL), layer),
                  _layer_spec((FOURIER_W, D_MODEL), layer), _layer_spec((RET_W, D_MODEL), layer),
                  _layer_spec((ATT_W, D_MODEL), layer), _layer_spec((D_MODEL, D_MODEL), layer)],
        out_specs=row_spec(D_MODEL),
        out_shape=jax.ShapeDtypeStruct((n, D_MODEL), F32),
        compiler_params=_params(("parallel",)),
        name="outproj",
    )(x2, ya, yb, yc, mod, g_pre, g_post, w_mg, wa, wb, wc, wo)


def _layer(x, layer, mod, p, *, latent, dft_chan, dft_pos, rope_tabs=None, cache_k=None, cache_v=None,
           state=None):
    b, t, _ = x.shape
    n = b * t
    row0, rows_per_mod = (1, t) if latent else (0, n)
    x2 = x.reshape(n, D_MODEL)
    xcs, fz, ur, ua, k32, v32 = _inproj(x2, layer, mod, p["g_pre"], p["w_main"], dft_chan,
                                        row0=row0, rows_per_mod=rows_per_mod)
    ya = _fourier(xcs.reshape(b, t, -1), fz.reshape(b, t, -1), dft_pos, p["w_four"], layer,
                  bg=b if latent else BG_CTX_FOURIER, tq=min(t, TQ_FOURIER))
    ur3 = ur.reshape(b, t, -1)
    ua3 = ua.reshape(b, t, -1)
    if latent:
        yb, s_fin = _retention(ur3, layer, p["dec_qk"], p["dec_att"], p["gn"], nb=1,
                               rope_tabs=rope_tabs, state=state)
        yc = _lat_attention(ua3, cache_k, cache_v, layer, p["sink"], rope_tabs)
    else:
        yb, s_fin = _retention(ur3, layer, p["dec_qk"], p["dec_att"], p["gn"], nb=NB_CTX_RET)
        yc = _ctx_attention(ua3, layer, p["sink"])
    out = _outproj(x2, ya.reshape(n, -1), yb.reshape(n, -1), yc.reshape(n, -1), layer, mod,
                   p["g_pre"], p["g_post"], p["w_mg"], p["w_pa"], p["w_pb"], p["w_pc"], p["w_out"],
                   row0=row0, rows_per_mod=rows_per_mod)
    return out.reshape(b, t, D_MODEL), k32, v32, s_fin


def kernel(x_prompt, x_sample, cache_k, cache_v, state_ret, c, c_ctx, w_mod, b_mod, g_pre, g_post, w_in,
           w_four, ret_decay, ret_gn, attn_sink, w_branch_a, w_branch_b, w_branch_c, w_out):
    batch, seq, _ = x_prompt.shape
    dec_batch, dec_seq, _ = x_sample.shape
    past = cache_k.shape[2]
    assert 1 + dec_batch <= MOD_ROWS

    cv = jnp.zeros((MOD_ROWS, D_MODEL), F32).at[0].set(c_ctx).at[1:1 + dec_batch].set(c)
    mod = _modulation(cv, w_mod, b_mod)

    dft_chan, dft_ctx = _dft_tables(seq)
    _, dft_lat = _dft_tables(dec_seq)
    rope_tabs = _rope_tables(dec_seq)
    ck = cache_k.reshape(dec_batch, DEPTH, past, ATT_KV_W)
    cvv = cache_v.reshape(dec_batch, DEPTH, past, ATT_KV_W)

    p = dict(
        g_pre=g_pre.reshape(DEPTH, 1, D_MODEL), g_post=g_post.reshape(DEPTH, 1, D_MODEL),
        w_main=w_in[:, :, :C_MG].astype(BF16), w_mg=w_in[:, :, C_MG:].astype(BF16),
        w_four=w_four.astype(BF16),
        w_pa=w_branch_a.astype(BF16), w_pb=w_branch_b.astype(BF16),
        w_pc=w_branch_c.astype(BF16), w_out=w_out.astype(BF16),
        dec_qk=jnp.repeat(ret_decay, RET_DK, axis=2),
        dec_att=jnp.repeat(ret_decay, RET_CHUNK, axis=2),
        gn=ret_gn.reshape(DEPTH, 1, RET_W),
        sink=attn_sink.reshape(DEPTH, ATT_Q_HEADS))

    xp = x_prompt
    ks, vs, ss = [], [], []
    for l in range(DEPTH):
        xp, k32, v32, s_fin = _layer(xp, l, mod, p, latent=False, dft_chan=dft_chan, dft_pos=dft_ctx)
        ks.append(k32.reshape(batch, seq, ATT_KV_HEADS, HEAD_DIM))
        vs.append(v32.reshape(batch, seq, ATT_KV_HEADS, HEAD_DIM))
        ss.append(s_fin)

    xs = x_sample
    for l in range(DEPTH):
        xs, _, _, _ = _layer(xs, l, mod, p, latent=True, dft_chan=dft_chan, dft_pos=dft_lat,
                             rope_tabs=rope_tabs, cache_k=ck, cache_v=cvv, state=state_ret)

    return (xp, xs, jnp.stack(ks, axis=1), jnp.stack(vs, axis=1), jnp.stack(ss, axis=1))
```

```python
import functools
import math

import numpy as np
import jax
import jax.numpy as jnp
from jax import lax
from jax.experimental import pallas as pl
from jax.experimental.pallas import tpu as pltpu

F32 = jnp.float32
BF16 = jnp.bfloat16

D_MODEL = 1024
DEPTH = 2
GRID_W = 64
HEAD_DIM = 64
FOURIER_GROUPS = 4
FOURIER_GROUP_W = 64
FOURIER_W = FOURIER_GROUPS * FOURIER_GROUP_W
RET_HEADS = 4
RET_DK = 64
RET_W = RET_HEADS * RET_DK
RET_CHUNK = 128
ATT_Q_HEADS = 8
ATT_KV_HEADS = 2
ATT_W = ATT_Q_HEADS * HEAD_DIM
ATT_KV_W = ATT_KV_HEADS * HEAD_DIM
WINDOW = 128
ROPE_BASE = 10000.0
EPS = 1e-6
MOD_ROWS = 8
LANES = 128
NEG = -1e30
LOG2E = math.log2(math.e)
VMEM_LIMIT = 56 * 1024 * 1024

C_FX, C_RQ, C_RZ_END = 0, 512, 1536
C_AQ, C_AK, C_AV, C_AZ, C_MG, C_END = 1536, 2048, 2176, 2304, 2816, 5888

TM_INPROJ = 1024
TM_OUTPROJ = 512
TQ_LATENT = 256
TQ_FOURIER = 512
NB_CTX_RET = 8
NB_CTX_ATTN = 4
BG_CTX_FOURIER = 8
RET_UNROLL = 4


def _sigmoid(x):
    return 0.5 * jnp.tanh(0.5 * x) + 0.5


def _silu(x):
    return x * _sigmoid(x)


def _params(sem):
    return pltpu.CompilerParams(dimension_semantics=sem, vmem_limit_bytes=VMEM_LIMIT)


def _const_spec(shape):
    nd = len(shape)
    return pl.BlockSpec(shape, lambda *_: (0,) * nd, pipeline_mode=pl.Buffered(1))


def _layer_spec(shape, layer):
    nd = len(shape)
    return pl.BlockSpec((None,) + tuple(shape), lambda *_: (layer,) + (0,) * nd,
                        pipeline_mode=pl.Buffered(1))


def _dft_tables(t):
    c = np.arange(FOURIER_GROUP_W)
    ang = 2.0 * np.pi * ((c[:, None] * c[None, :]) % FOURIER_GROUP_W) / FOURIER_GROUP_W
    eye = np.eye(FOURIER_GROUPS)
    s64 = FOURIER_GROUP_W ** -0.5
    chan = np.concatenate([np.kron(eye, np.cos(ang) * s64), np.kron(eye, np.sin(ang) * s64)], axis=1)
    p = np.arange(t)
    angt = 2.0 * np.pi * ((p[:, None] * p[None, :]) % t) / t
    pos = np.concatenate([np.cos(angt), -np.sin(angt)], axis=1) * (t ** -0.5)
    return jnp.asarray(chan, F32).astype(BF16), jnp.asarray(pos, F32).astype(BF16)


def _rope_tables(t):
    quarter = HEAD_DIM // 4
    lane = np.arange(LANES) % HEAD_DIM
    inv = ROPE_BASE ** (-(lane % quarter).astype(np.float64) / quarter)
    n = np.arange(t)
    pos = np.where(lane[None, :] < HEAD_DIM // 2, (n // GRID_W)[:, None], (n % GRID_W)[:, None])
    ang = pos.astype(np.float64) * inv[None, :]
    sign = np.where((lane % (2 * quarter)) < quarter, -1.0, 1.0)
    return jnp.asarray(np.cos(ang), F32), jnp.asarray(np.sin(ang) * sign[None, :], F32)


def _rope(x, cos, sin):
    lane = lax.broadcasted_iota(jnp.int32, x.shape, 1)
    first = (lane & 31) < 16
    partner = jnp.where(first, pltpu.roll(x, LANES - 16, axis=1), pltpu.roll(x, 16, axis=1))
    return x * cos + partner * sin


def _split_bf16(x):
    hi = x.astype(BF16)
    return hi, (x - hi.astype(F32)).astype(BF16)


def _mod_kernel(cv_ref, w_ref, b_ref, o_ref):
    a_hi, a_lo = _split_bf16(_silu(cv_ref[...]))
    w_hi, w_lo = _split_bf16(w_ref[...])
    dot = functools.partial(jnp.dot, preferred_element_type=F32)
    o_ref[...] = dot(a_hi, w_hi) + (dot(a_lo, w_hi) + dot(a_hi, w_lo)) + b_ref[...]


def _modulation(cv, w_mod, b_mod):
    tn = 1024
    return pl.pallas_call(
        _mod_kernel,
        grid=(DEPTH, 3 * D_MODEL // tn),
        in_specs=[pl.BlockSpec((MOD_ROWS, D_MODEL), lambda l, j: (0, 0)),
                  pl.BlockSpec((None, D_MODEL, tn), lambda l, j: (l, 0, j)),
                  pl.BlockSpec((None, 1, tn), lambda l, j: (l, 0, j))],
        out_specs=pl.BlockSpec((None, MOD_ROWS, tn), lambda l, j: (l, 0, j)),
        out_shape=jax.ShapeDtypeStruct((DEPTH, MOD_ROWS, 3 * D_MODEL), F32),
        compiler_params=_params(("parallel", "parallel")),
        name="modulation",
    )(cv, w_mod, b_mod.reshape(DEPTH, 1, 3 * D_MODEL))


def _modulated_norm(x, g, mod_ref, tm, row0, rows_per_mod):
    row = row0 + (pl.program_id(0) * tm) // rows_per_mod
    ms = jnp.mean(x * x, axis=-1, keepdims=True)
    y = x * lax.rsqrt(ms + EPS) * g
    shift = mod_ref[pl.ds(row, 1), 0:D_MODEL]
    scale = mod_ref[pl.ds(row, 1), D_MODEL:2 * D_MODEL]
    return (y * (1.0 + scale) + shift).astype(BF16), row


def _inproj_kernel(x_ref, mod_ref, g_ref, w_ref, dft_ref,
                   xcs_ref, fz_ref, ur_ref, ua_ref, k_ref, v_ref, *, tm, row0, rows_per_mod):
    h, _ = _modulated_norm(x_ref[...], g_ref[...], mod_ref, tm, row0, rows_per_mod)

    def mm(c0, c1):
        return jnp.dot(h, w_ref[:, c0:c1], preferred_element_type=F32)

    f = mm(C_FX, C_RQ)
    xcs_ref[...] = jnp.dot(f[:, :FOURIER_W].astype(BF16), dft_ref[...],
                           preferred_element_type=F32).astype(BF16)
    fz_ref[...] = f[:, FOURIER_W:].astype(BF16)
    ur_ref[...] = mm(C_RQ, C_RZ_END).astype(BF16)
    ua_ref[:, 0:ATT_W] = mm(C_AQ, C_AK).astype(BF16)
    ua_ref[:, ATT_W:2 * ATT_W] = mm(C_AZ, C_MG).astype(BF16)
    kv = mm(C_AK, C_AZ)
    k_ref[...] = kv[:, :ATT_KV_W]
    v_ref[...] = kv[:, ATT_KV_W:]
    ua_ref[:, 2 * ATT_W:] = kv.astype(BF16)


def _inproj(x2, layer, mod, g_pre, w_in, dft, *, row0, rows_per_mod):
    n = x2.shape[0]
    tm = TM_INPROJ
    row_spec = lambda w: pl.BlockSpec((tm, w), lambda i: (i, 0))
    widths = (2 * FOURIER_W, FOURIER_W, 4 * RET_W, 2 * ATT_W + 2 * ATT_KV_W)
    out_shape = [jax.ShapeDtypeStruct((n, w), BF16) for w in widths]
    out_shape += [jax.ShapeDtypeStruct((n, ATT_KV_W), F32)] * 2
    return pl.pallas_call(
        functools.partial(_inproj_kernel, tm=tm, row0=row0, rows_per_mod=rows_per_mod),
        grid=(n // tm,),
        in_specs=[row_spec(D_MODEL),
                  _layer_spec((MOD_ROWS, 3 * D_MODEL), layer),
                  _layer_spec((1, D_MODEL), layer),
                  _layer_spec((D_MODEL, C_END), layer),
                  _const_spec((FOURIER_W, 2 * FOURIER_W))],
        out_specs=[row_spec(w) for w in widths] + [row_spec(ATT_KV_W)] * 2,
        out_shape=out_shape,
        compiler_params=_params(("parallel",)),
        name="inproj",
    )(x2, mod, g_pre, w_in, dft)


def _fourier_kernel(ct_ref, xcs_ref, fz_ref, w_ref, ya_ref, xcat_ref, *, bg, t):
    @pl.when(pl.program_id(1) == 0)
    def _gather():
        for b in range(bg):
            cols = slice(b * FOURIER_W, (b + 1) * FOURIER_W)
            xcat_ref[0:t, cols] = xcs_ref[b, :, 0:FOURIER_W]
            xcat_ref[t:2 * t, cols] = xcs_ref[b, :, FOURIER_W:]

    yr = jnp.dot(ct_ref[...], xcat_ref[...], preferred_element_type=F32).astype(BF16)
    for b in range(bg):
        ya = jnp.dot(yr[:, b * FOURIER_W:(b + 1) * FOURIER_W], w_ref[...], preferred_element_type=F32)
        ya_ref[b] = (ya * _silu(fz_ref[b].astype(F32))).astype(BF16)


def _fourier(xcs, fz, ct, w_four, layer, *, bg, tq):
    b, t, _ = xcs.shape
    return pl.pallas_call(
        functools.partial(_fourier_kernel, bg=bg, t=t),
        grid=(b // bg, t // tq),
        in_specs=[pl.BlockSpec((tq, 2 * t), lambda i, j: (j, 0)),
                  pl.BlockSpec((bg, t, 2 * FOURIER_W), lambda i, j: (i, 0, 0)),
                  pl.BlockSpec((bg, tq, FOURIER_W), lambda i, j: (i, j, 0)),
                  _layer_spec((FOURIER_W, FOURIER_W), layer)],
        out_specs=pl.BlockSpec((bg, tq, FOURIER_W), lambda i, j: (i, j, 0)),
        out_shape=jax.ShapeDtypeStruct((b, t, FOURIER_W), BF16),
        scratch_shapes=[pltpu.VMEM((2 * t, bg * FOURIER_W), BF16)],
        compiler_params=_params(("parallel", "arbitrary")),
        name="fourier",
    )(ct, xcs, fz, w_four)


def _log_sigmoid(x):
    return jnp.minimum(x, 0.0) - jnp.log(1.0 + jnp.exp(-jnp.abs(x)))


def _head_blocks(a, width):
    lane = lax.broadcasted_iota(jnp.int32, a.shape, 1)
    zero = jnp.zeros_like(a)
    return jnp.concatenate(
        [jnp.where((lane >= h * width) & (lane < (h + 1) * width), a, zero) for h in range(RET_HEADS)],
        axis=0)


def _per_head(dec_ref, layer, direction, head_of):
    base = direction * RET_HEADS
    out = jnp.full(head_of.shape, dec_ref[layer, base + RET_HEADS - 1], F32)
    for h in range(RET_HEADS - 2, -1, -1):
        out = jnp.where(head_of == h, dec_ref[layer, base + h], out)
    return out


def _ret_kernel(*refs, nb, t, layer, rope, has_s0):
    refs = list(refs)
    dec_ref = refs.pop(0)
    ur_ref = refs.pop(0)
    cos_ref = refs.pop(0) if rope else None
    sin_ref = refs.pop(0) if rope else None
    s0_ref = refs.pop(0) if has_s0 else None
    (gn_ref, yb_ref, sfin_ref, kr_ref, dsf_ref, dsb_ref, sfs_ref, sbs_ref, st_ref) = refs
    c = RET_CHUNK
    nc = t // c
    nch = nb * nc
    group = RET_UNROLL

    lane_head = lax.broadcasted_iota(jnp.int32, (1, RET_W), 1) // RET_DK
    lgf = _log_sigmoid(_per_head(dec_ref, layer, 0, lane_head))
    lgb = _log_sigmoid(_per_head(dec_ref, layer, 1, lane_head))
    ri = lax.broadcasted_iota(jnp.int32, (c, RET_W), 0).astype(F32)
    read_f = jnp.exp((ri + 1.0) * lgf)
    read_b = jnp.exp((c - ri) * lgb)
    write_f = jnp.exp((c - 1.0 - ri) * lgf)
    write_b = jnp.exp(ri * lgb)
    carry_f = jnp.exp(c * lgf)
    carry_b = jnp.exp(c * lgb)
    ii = lax.broadcasted_iota(jnp.int32, (RET_HEADS * c, c), 0)
    jj = lax.broadcasted_iota(jnp.int32, (RET_HEADS * c, c), 1)
    row_head = ii // c
    diff = ((ii & (c - 1)) - jj).astype(F32)
    decay = (jnp.where(diff >= 0, jnp.exp(jnp.maximum(diff, 0.0)
                                          * _log_sigmoid(_per_head(dec_ref, layer, 0, row_head))), 0.0)
             + jnp.where(diff <= 0, jnp.exp(jnp.maximum(-diff, 0.0)
                                            * _log_sigmoid(_per_head(dec_ref, layer, 1, row_head))), 0.0))
    r2 = lax.broadcasted_iota(jnp.int32, (RET_W, RET_W), 0)
    c2 = lax.broadcasted_iota(jnp.int32, (RET_W, RET_W), 1)
    same_head = (r2 // RET_DK) == (c2 // RET_DK)
    group_mean = jnp.where(same_head, 1.0 / RET_DK, 0.0).astype(BF16)
    out_head = lax.broadcasted_iota(jnp.int32, (c, RET_W), 1) // RET_DK

    def chunk_pos(ci):
        return ci // nc, pl.multiple_of((ci % nc) * c, c)

    def load_qk(col, b, r0):
        a = ur_ref[b, pl.ds(r0, c), col:col + RET_W].astype(F32)
        if rope:
            cs = cos_ref[pl.ds(r0, c), :]
            sn = sin_ref[pl.ds(r0, c), :]
            a = jnp.concatenate([_rope(a[:, :LANES], cs, sn), _rope(a[:, LANES:], cs, sn)], axis=1)
        return a

    def increments(ci, carry):
        b, r0 = chunk_pos(ci)
        k = load_qk(RET_W, b, r0) * (RET_DK ** -0.5)
        v = ur_ref[b, pl.ds(r0, c), 2 * RET_W:3 * RET_W]
        kr_ref[pl.ds(pl.multiple_of(ci * c, c), c), :] = k.astype(BF16)
        kw = jnp.concatenate([(k * write_f).astype(BF16), (k * write_b).astype(BF16)], axis=1)
        d = lax.dot_general(kw, v, (((0,), (0,)), ((), ())), preferred_element_type=F32)
        dsf_ref[ci] = jnp.where(same_head, d[:RET_W], 0.0)
        dsb_ref[ci] = jnp.where(same_head, d[RET_W:], 0.0)
        return carry

    lax.fori_loop(0, nch, increments, 0, unroll=group)

    def scan(b, direction, ds_ref, out_ref, carry_decay):
        st_ref[...] = jnp.zeros((RET_W, RET_W), F32)
        if has_s0:
            for h in range(RET_HEADS):
                sl = slice(h * RET_DK, (h + 1) * RET_DK)
                st_ref[sl, sl] = s0_ref[b, direction, h]

        def step(n, carry):
            ci = b * nc + (n if direction == 0 else nc - 1 - n)
            out_ref[ci] = st_ref[...].astype(BF16)
            st_ref[...] = carry_decay * st_ref[...] + ds_ref[ci]
            return carry

        lax.fori_loop(0, nc, step, 0, unroll=min(nc, 4))
        for h in range(RET_HEADS):
            sl = slice(h * RET_DK, (h + 1) * RET_DK)
            sfin_ref[b, direction, h] = st_ref[sl, sl]

    for b in range(nb):
        scan(b, 0, dsf_ref, sfs_ref, carry_f)
        scan(b, 1, dsb_ref, sbs_ref, carry_b)

    def chunk_output(ci, b, r0):
        q = load_qk(0, b, r0)
        kb = kr_ref[pl.ds(pl.multiple_of(ci * c, c), c), :]
        v = ur_ref[b, pl.ds(r0, c), 2 * RET_W:3 * RET_W]
        att = lax.dot_general(_head_blocks(q.astype(BF16), RET_DK), kb,
                              (((1,), (1,)), ((), ())), preferred_element_type=F32)
        o4 = jnp.dot((att * decay).astype(BF16), v, preferred_element_type=F32)
        o = o4[(RET_HEADS - 1) * c:]
        for h in range(RET_HEADS - 2, -1, -1):
            o = jnp.where(out_head == h, o4[h * c:(h + 1) * c], o)
        o = o + jnp.dot((q * read_f).astype(BF16), sfs_ref[ci], preferred_element_type=F32)
        return o + jnp.dot((q * read_b).astype(BF16), sbs_ref[ci], preferred_element_type=F32)

    def outputs(it, carry):
        pos = [chunk_pos(it * group + u) for u in range(group)]
        o = jnp.concatenate([chunk_output(it * group + u, *pos[u]) for u in range(group)], axis=0)
        hi, lo = _split_bf16(o * o)
        ms = (jnp.dot(hi, group_mean, preferred_element_type=F32)
              + jnp.dot(lo, group_mean, preferred_element_type=F32))
        y = o * lax.rsqrt(ms + EPS) * gn_ref[...]
        for u, (b, r0) in enumerate(pos):
            z = ur_ref[b, pl.ds(r0, c), 3 * RET_W:4 * RET_W].astype(F32)
            yb_ref[b, pl.ds(r0, c), :] = (y[u * c:(u + 1) * c] * _silu(z)).astype(BF16)
        return carry

    lax.fori_loop(0, nch // group, outputs, 0)


def _retention(ur, layer, dec, gn, *, nb, rope_tabs=None, state=None):
    b, t, _ = ur.shape
    nch = nb * (t // RET_CHUNK)
    assert nch % RET_UNROLL == 0
    rope = rope_tabs is not None
    has_s0 = state is not None
    in_specs = [pl.BlockSpec(memory_space=pltpu.SMEM),
                pl.BlockSpec((nb, t, 4 * RET_W), lambda i: (i, 0, 0))]
    args = [dec, ur]
    if rope:
        in_specs += [_const_spec((t, LANES)), _const_spec((t, LANES))]
        args += list(rope_tabs)
    if has_s0:
        in_specs.append(pl.BlockSpec((nb, None, 2, RET_HEADS, RET_DK, RET_DK),
                                     lambda i: (i, layer, 0, 0, 0, 0)))
        args.append(state)
    in_specs.append(_layer_spec((1, RET_W), layer))
    args.append(gn)
    state_scratch = lambda dt: pltpu.VMEM((nch, RET_W, RET_W), dt)
    return pl.pallas_call(
        functools.partial(_ret_kernel, nb=nb, t=t, layer=layer, rope=rope, has_s0=has_s0),
        grid=(b // nb,),
        in_specs=in_specs,
        out_specs=[pl.BlockSpec((nb, t, RET_W), lambda i: (i, 0, 0)),
                   pl.BlockSpec((nb, 2, RET_HEADS, RET_DK, RET_DK), lambda i: (i, 0, 0, 0, 0))],
        out_shape=[jax.ShapeDtypeStruct((b, t, RET_W), BF16),
                   jax.ShapeDtypeStruct((b, 2, RET_HEADS, RET_DK, RET_DK), F32)],
        scratch_shapes=[pltpu.VMEM((nb * t, RET_W), BF16),
                        state_scratch(F32), state_scratch(F32), state_scratch(BF16), state_scratch(BF16),
                        pltpu.VMEM((RET_W, RET_W), F32)],
        compiler_params=_params(("parallel",)),
        name="retention",
    )(*args)


def _kv_variants(a, ones_block):
    lane_half = lax.broadcasted_iota(jnp.int32, a.shape, 1) // HEAD_DIM
    swapped = pltpu.roll(a, HEAD_DIM, axis=1)
    out = []
    for g in range(ATT_KV_HEADS):
        row = []
        for half in range(2):
            var = jnp.where(lane_half == half, a if half == g else swapped, 0.0).astype(BF16)
            if ones_block:
                ones = jnp.where(lane_half == half, 1.0, 0.0).astype(BF16)
                var = jnp.concatenate([var, ones], axis=1)
            row.append(var)
        out.append(row)
    return out


def _attend_group(q2, keys, values, masks, sink_ref, layer, g, z2):
    rows = q2.shape[0]
    upper = lax.broadcasted_iota(jnp.int32, (rows, 1), 0) < rows // 2
    acc = None
    sink_terms = []
    for half in range(2):
        sink = jnp.where(upper, sink_ref[layer, 4 * g + half], sink_ref[layer, 4 * g + 2 + half]) * LOG2E
        logits = []
        for kpart, mask in zip(keys[half], masks):
            l = lax.dot_general(q2, kpart, (((1,), (1,)), ((), ())), preferred_element_type=F32)
            logits.append(l if mask is None else jnp.where(mask, l, NEG))
        m = sink
        for l in logits:
            m = jnp.maximum(m, jnp.max(l, axis=-1, keepdims=True))
        for l, vpart in zip(logits, values[half]):
            pv = jnp.dot(jnp.exp2(l - m).astype(BF16), vpart, preferred_element_type=F32)
            acc = pv if acc is None else acc + pv
        sink_terms.append(jnp.exp2(sink - m))
    lane = lax.broadcasted_iota(jnp.int32, (rows, LANES), 1)
    den = acc[:, LANES:] + jnp.where(lane < HEAD_DIM, sink_terms[0], sink_terms[1])
    return (acc[:, :LANES] * (1.0 / den) * _silu(z2)).astype(BF16)


def _pair_rows(ref_slice, g):
    return jnp.concatenate([ref_slice(2 * g), ref_slice(2 * g + 1)], axis=0)


def _ctx_attn_kernel(sink_ref, ua_ref, yc_ref, *, nb, t, layer):
    for b in range(nb):
        kv = ua_ref[b, :, 2 * ATT_W:].astype(F32)
        kvar = _kv_variants(kv[:, :LANES] * (HEAD_DIM ** -0.5 * LOG2E), False)
        vvar = _kv_variants(kv[:, LANES:], True)
        for g in range(ATT_KV_HEADS):
            q2 = _pair_rows(lambda p: ua_ref[b, :, p * LANES:(p + 1) * LANES], g)
            z2 = _pair_rows(lambda p: ua_ref[b, :, ATT_W + p * LANES:ATT_W + (p + 1) * LANES], g)
            o = _attend_group(q2, [[kvar[g][0]], [kvar[g][1]]], [[vvar[g][0]], [vvar[g][1]]], [None],
                              sink_ref, layer, g, z2.astype(F32))
            yc_ref[b, :, 2 * g * LANES:(2 * g + 1) * LANES] = o[:t]
            yc_ref[b, :, (2 * g + 1) * LANES:(2 * g + 2) * LANES] = o[t:]


def _ctx_attention(ua, layer, sink):
    b, t, w = ua.shape
    nb = NB_CTX_ATTN
    return pl.pallas_call(
        functools.partial(_ctx_attn_kernel, nb=nb, t=t, layer=layer),
        grid=(b // nb,),
        in_specs=[pl.BlockSpec(memory_space=pltpu.SMEM),
                  pl.BlockSpec((nb, t, w), lambda i: (i, 0, 0))],
        out_specs=pl.BlockSpec((nb, t, ATT_W), lambda i: (i, 0, 0)),
        out_shape=jax.ShapeDtypeStruct((b, t, ATT_W), BF16),
        compiler_params=_params(("parallel",)),
        name="ctx_attention",
    )(sink, ua)


def _lat_attn_kernel(sink_ref, qz_ref, kv_ref, ck_ref, cv_ref, cos_ref, sin_ref, yc_ref,
                     kl_ref, vl_ref, kc_ref, vc_ref, *, t, tq, layer):
    j = pl.program_id(1)
    scale = HEAD_DIM ** -0.5 * LOG2E

    @pl.when(j == 0)
    def _prepare():
        kv = kv_ref[...].astype(F32)
        kvar = _kv_variants(_rope(kv[:, :LANES], cos_ref[...], sin_ref[...]) * scale, False)
        vvar = _kv_variants(kv[:, LANES:], True)
        cvar = _kv_variants(ck_ref[...] * scale, False)
        dvar = _kv_variants(cv_ref[...], True)
        for g in range(ATT_KV_HEADS):
            for half in range(2):
                i = 2 * g + half
                for ref, var in ((kl_ref, kvar), (vl_ref, vvar)):
                    pad = jnp.zeros((WINDOW, ref.shape[-1]), BF16)
                    ref[i, 0:WINDOW, :] = pad
                    ref[i, WINDOW:WINDOW + t, :] = var[g][half]
                    ref[i, WINDOW + t:, :] = pad
                kc_ref[i] = cvar[g][half]
                vc_ref[i] = dvar[g][half]

    r0 = pl.multiple_of(j * tq, tq)
    nloc = tq + 2 * WINDOW
    rr = lax.broadcasted_iota(jnp.int32, (2 * tq, nloc), 0) & (tq - 1)
    ss = lax.broadcasted_iota(jnp.int32, (2 * tq, nloc), 1)
    band = ((ss - rr >= 0) & (ss - rr <= 2 * WINDOW)
            & (ss >= WINDOW - j * tq) & (ss < t + WINDOW - j * tq))
    cs = cos_ref[pl.ds(r0, tq), :]
    sn = sin_ref[pl.ds(r0, tq), :]
    for g in range(ATT_KV_HEADS):
        q2 = _pair_rows(
            lambda p: _rope(qz_ref[:, p * LANES:(p + 1) * LANES].astype(F32), cs, sn).astype(BF16), g)
        z2 = _pair_rows(lambda p: qz_ref[:, ATT_W + p * LANES:ATT_W + (p + 1) * LANES], g)
        keys = [[kl_ref[2 * g + half, pl.ds(r0, nloc), :], kc_ref[2 * g + half]] for half in range(2)]
        vals = [[vl_ref[2 * g + half, pl.ds(r0, nloc), :], vc_ref[2 * g + half]] for half in range(2)]
        o = _attend_group(q2, keys, vals, [band, None], sink_ref, layer, g, z2.astype(F32))
        yc_ref[:, 2 * g * LANES:(2 * g + 1) * LANES] = o[:tq]
        yc_ref[:, (2 * g + 1) * LANES:(2 * g + 2) * LANES] = o[tq:]


def _lat_attention(ua, cache_k, cache_v, layer, sink, rope_tabs):
    b, t, _ = ua.shape
    past = cache_k.shape[2]
    tq = TQ_LATENT
    nvar = 2 * ATT_KV_HEADS
    cache_spec = pl.BlockSpec((None, None, past, ATT_KV_W), lambda i, j: (i, layer, 0, 0))
    return pl.pallas_call(
        functools.partial(_lat_attn_kernel, t=t, tq=tq, layer=layer),
        grid=(b, t // tq),
        in_specs=[pl.BlockSpec(memory_space=pltpu.SMEM),
                  pl.BlockSpec((None, tq, 2 * ATT_W), lambda i, j: (i, j, 0)),
                  pl.BlockSpec((None, t, 2 * ATT_KV_W), lambda i, j: (i, 0, 2 * ATT_W // (2 * ATT_KV_W))),
                  cache_spec, cache_spec,
                  _const_spec((t, LANES)), _const_spec((t, LANES))],
        out_specs=pl.BlockSpec((None, tq, ATT_W), lambda i, j: (i, j, 0)),
        out_shape=jax.ShapeDtypeStruct((b, t, ATT_W), BF16),
        scratch_shapes=[pltpu.VMEM((nvar, t + 2 * WINDOW, LANES), BF16),
                        pltpu.VMEM((nvar, t + 2 * WINDOW, 2 * LANES), BF16),
                        pltpu.VMEM((nvar, past, LANES), BF16),
                        pltpu.VMEM((nvar, past, 2 * LANES), BF16)],
        compiler_params=_params(("parallel", "arbitrary")),
        name="lat_attention",
    )(sink, ua, ua, cache_k, cache_v, *rope_tabs)


def _outproj_kernel(x_ref, ya_ref, yb_ref, yc_ref, mod_ref, gpre_ref, gpost_ref,
                    win_ref, wa_ref, wb_ref, wc_ref, wo_ref, o_ref, *, tm, row0, rows_per_mod):
    x = x_ref[...]
    h, row = _modulated_norm(x, gpre_ref[...], mod_ref, tm, row0, rows_per_mod)
    merged = None
    for c, (y_ref, w_ref) in enumerate(((ya_ref, wa_ref), (yb_ref, wb_ref), (yc_ref, wc_ref))):
        mg = jnp.dot(h, win_ref[:, C_MG + c * D_MODEL:C_MG + (c + 1) * D_MODEL], preferred_element_type=F32)
        term = _sigmoid(mg) * jnp.dot(y_ref[...], w_ref[...], preferred_element_type=F32)
        merged = term if merged is None else merged + term
    out = jnp.dot(merged.astype(BF16), wo_ref[...], preferred_element_type=F32)
    ms = jnp.mean(out * out, axis=-1, keepdims=True)
    normed = out * lax.rsqrt(ms + EPS) * gpost_ref[...]
    gate = mod_ref[pl.ds(row, 1), 2 * D_MODEL:3 * D_MODEL]
    o_ref[...] = x + gate * normed


def _outproj(x2, ya, yb, yc, layer, mod, g_pre, g_post, w_in, wa, wb, wc, wo, *, row0, rows_per_mod):
    n = x2.shape[0]
    tm = TM_OUTPROJ
    row_spec = lambda w: pl.BlockSpec((tm, w), lambda i: (i, 0))
    return pl.pallas_call(
        functools.partial(_outproj_kernel, tm=tm, row0=row0, rows_per_mod=rows_per_mod),
        grid=(n // tm,),
        in_specs=[row_spec(D_MODEL), row_spec(FOURIER_W), row_spec(RET_W), row_spec(ATT_W),
                  _layer_spec((MOD_ROWS, 3 * D_MODEL), layer),
                  _layer_spec((1, D_MODEL), layer), _layer_spec((1, D_MODEL), layer),
                  _layer_spec((D_MODEL, C_END), layer),
                  _layer_spec((FOURIER_W, D_MODEL), layer), _layer_spec((RET_W, D_MODEL), layer),
                  _layer_spec((ATT_W, D_MODEL), layer), _layer_spec((D_MODEL, D_MODEL), layer)],
        out_specs=row_spec(D_MODEL),
        out_shape=jax.ShapeDtypeStruct((n, D_MODEL), F32),
        compiler_params=_params(("parallel",)),
        name="outproj",
    )(x2, ya, yb, yc, mod, g_pre, g_post, w_in, wa, wb, wc, wo)


def _layer(x, layer, mod, p, *, latent, dft_chan, dft_pos, rope_tabs=None, cache_k=None, cache_v=None,
           state=None):
    b, t, _ = x.shape
    n = b * t
    row0, rows_per_mod = (1, t) if latent else (0, n)
    x2 = x.reshape(n, D_MODEL)
    xcs, fz, ur, ua, k32, v32 = _inproj(x2, layer, mod, p["g_pre"], p["w_in"], dft_chan,
                                        row0=row0, rows_per_mod=rows_per_mod)
    ya = _fourier(xcs.reshape(b, t, -1), fz.reshape(b, t, -1), dft_pos, p["w_four"], layer,
                  bg=b if latent else BG_CTX_FOURIER, tq=min(t, TQ_FOURIER))
    ur3 = ur.reshape(b, t, -1)
    ua3 = ua.reshape(b, t, -1)
    if latent:
        yb, s_fin = _retention(ur3, layer, p["dec"], p["gn"], nb=1, rope_tabs=rope_tabs, state=state)
        yc = _lat_attention(ua3, cache_k, cache_v, layer, p["sink"], rope_tabs)
    else:
        yb, s_fin = _retention(ur3, layer, p["dec"], p["gn"], nb=NB_CTX_RET)
        yc = _ctx_attention(ua3, layer, p["sink"])
    out = _outproj(x2, ya.reshape(n, -1), yb.reshape(n, -1), yc.reshape(n, -1), layer, mod,
                   p["g_pre"], p["g_post"], p["w_in"], p["w_pa"], p["w_pb"], p["w_pc"], p["w_out"],
                   row0=row0, rows_per_mod=rows_per_mod)
    return out.reshape(b, t, D_MODEL), k32, v32, s_fin


def kernel(x_prompt, x_sample, cache_k, cache_v, state_ret, c, c_ctx, w_mod, b_mod, g_pre, g_post, w_in,
           w_four, ret_decay, ret_gn, attn_sink, w_branch_a, w_branch_b, w_branch_c, w_out):
    batch, seq, _ = x_prompt.shape
    dec_batch, dec_seq, _ = x_sample.shape
    past = cache_k.shape[2]
    assert 1 + dec_batch <= MOD_ROWS

    cv = jnp.zeros((MOD_ROWS, D_MODEL), F32).at[0].set(c_ctx).at[1:1 + dec_batch].set(c)
    mod = _modulation(cv, w_mod, b_mod)

    dft_chan, dft_ctx = _dft_tables(seq)
    _, dft_lat = _dft_tables(dec_seq)
    rope_tabs = _rope_tables(dec_seq)
    ck = cache_k.reshape(dec_batch, DEPTH, past, ATT_KV_W)
    cvv = cache_v.reshape(dec_batch, DEPTH, past, ATT_KV_W)

    p = dict(
        g_pre=g_pre.reshape(DEPTH, 1, D_MODEL), g_post=g_post.reshape(DEPTH, 1, D_MODEL),
        w_in=w_in.astype(BF16), w_four=w_four.astype(BF16),
        w_pa=w_branch_a.astype(BF16), w_pb=w_branch_b.astype(BF16),
        w_pc=w_branch_c.astype(BF16), w_out=w_out.astype(BF16),
        dec=ret_decay.reshape(DEPTH, 2 * RET_HEADS),
        gn=ret_gn.reshape(DEPTH, 1, RET_W),
        sink=attn_sink.reshape(DEPTH, ATT_Q_HEADS))

    xp = x_prompt
    ks, vs, ss = [], [], []
    for l in range(DEPTH):
        xp, k32, v32, s_fin = _layer(xp, l, mod, p, latent=False, dft_chan=dft_chan, dft_pos=dft_ctx)
        ks.append(k32.reshape(batch, seq, ATT_KV_HEADS, HEAD_DIM))
        vs.append(v32.reshape(batch, seq, ATT_KV_HEADS, HEAD_DIM))
        ss.append(s_fin)

    xs = x_sample
    for l in range(DEPTH):
        xs, _, _, _ = _layer(xs, l, mod, p, latent=True, dft_chan=dft_chan, dft_pos=dft_lat,
                             rope_tabs=rope_tabs, cache_k=ck, cache_v=cvv, state=state_ret)

    return (xp, xs, jnp.stack(ks, axis=1), jnp.stack(vs, axis=1), jnp.stack(ss, axis=1))
```

```python
import functools
import math

import numpy as np
import jax
import jax.numpy as jnp
from jax import lax
from jax.experimental import pallas as pl
from jax.experimental.pallas import tpu as pltpu

F32 = jnp.float32
BF16 = jnp.bfloat16

D_MODEL = 1024
DEPTH = 2
GRID_W = 64
HEAD_DIM = 64
FOURIER_GROUPS = 4
FOURIER_GROUP_W = 64
FOURIER_W = FOURIER_GROUPS * FOURIER_GROUP_W
RET_HEADS = 4
RET_DK = 64
RET_W = RET_HEADS * RET_DK
RET_CHUNK = 128
ATT_Q_HEADS = 8
ATT_KV_HEADS = 2
ATT_W = ATT_Q_HEADS * HEAD_DIM
ATT_KV_W = ATT_KV_HEADS * HEAD_DIM
WINDOW = 128
ROPE_BASE = 10000.0
EPS = 1e-6
MOD_ROWS = 8
LANES = 128
NEG = -1e30
LOG2E = math.log2(math.e)
VMEM_LIMIT = 56 * 1024 * 1024

C_FX, C_RQ, C_RZ_END = 0, 512, 1536
C_AQ, C_AK, C_AV, C_AZ, C_MG, C_END = 1536, 2048, 2176, 2304, 2816, 5888

TM_INPROJ = 1024
TM_OUTPROJ = 512
TQ_LATENT = 256
TQ_FOURIER = 512
NB_CTX_RET = 8
NB_CTX_ATTN = 8
BG_CTX_FOURIER = 8
RET_UNROLL = 4


def _sigmoid(x):
    return 0.5 * jnp.tanh(0.5 * x) + 0.5


def _silu(x):
    return x * _sigmoid(x)


def _params(sem):
    return pltpu.CompilerParams(dimension_semantics=sem, vmem_limit_bytes=VMEM_LIMIT)


def _const_spec(shape):
    nd = len(shape)
    return pl.BlockSpec(shape, lambda *_: (0,) * nd, pipeline_mode=pl.Buffered(1))


def _layer_spec(shape, layer, col_block=0):
    idx = (layer,) + (0,) * (len(shape) - 1) + (col_block,)
    return pl.BlockSpec((None,) + tuple(shape), lambda *_: idx, pipeline_mode=pl.Buffered(1))


def _dft_tables(t):
    c = np.arange(FOURIER_GROUP_W)
    ang = 2.0 * np.pi * ((c[:, None] * c[None, :]) % FOURIER_GROUP_W) / FOURIER_GROUP_W
    eye = np.eye(FOURIER_GROUPS)
    s64 = FOURIER_GROUP_W ** -0.5
    chan = np.concatenate([np.kron(eye, np.cos(ang) * s64), np.kron(eye, np.sin(ang) * s64)], axis=1)
    p = np.arange(t)
    angt = 2.0 * np.pi * ((p[:, None] * p[None, :]) % t) / t
    pos = np.concatenate([np.cos(angt), -np.sin(angt)], axis=1) * (t ** -0.5)
    return jnp.asarray(chan, F32).astype(BF16), jnp.asarray(pos, F32).astype(BF16)


def _rope_tables(t):
    quarter = HEAD_DIM // 4
    lane = np.arange(LANES) % HEAD_DIM
    inv = ROPE_BASE ** (-(lane % quarter).astype(np.float64) / quarter)
    n = np.arange(t)
    pos = np.where(lane[None, :] < HEAD_DIM // 2, (n // GRID_W)[:, None], (n % GRID_W)[:, None])
    ang = pos.astype(np.float64) * inv[None, :]
    sign = np.where((lane % (2 * quarter)) < quarter, -1.0, 1.0)
    return jnp.asarray(np.cos(ang), F32), jnp.asarray(np.sin(ang) * sign[None, :], F32)


def _rope(x, cos, sin):
    lane = lax.broadcasted_iota(jnp.int32, x.shape, 1)
    first = (lane & 31) < 16
    partner = jnp.where(first, pltpu.roll(x, LANES - 16, axis=1), pltpu.roll(x, 16, axis=1))
    return x * cos + partner * sin


def _split_bf16(x):
    hi = x.astype(BF16)
    return hi, (x - hi.astype(F32)).astype(BF16)


def _mod_kernel(cv_ref, w_ref, b_ref, o_ref):
    a_hi, a_lo = _split_bf16(_silu(cv_ref[...]))
    w_hi, w_lo = _split_bf16(w_ref[...])
    dot = functools.partial(jnp.dot, preferred_element_type=F32)
    o_ref[...] = dot(a_hi, w_hi) + (dot(a_lo, w_hi) + dot(a_hi, w_lo)) + b_ref[...]


def _modulation(cv, w_mod, b_mod):
    tn = 1024
    return pl.pallas_call(
        _mod_kernel,
        grid=(DEPTH, 3 * D_MODEL // tn),
        in_specs=[pl.BlockSpec((MOD_ROWS, D_MODEL), lambda l, j: (0, 0)),
                  pl.BlockSpec((None, D_MODEL, tn), lambda l, j: (l, 0, j)),
                  pl.BlockSpec((None, 1, tn), lambda l, j: (l, 0, j))],
        out_specs=pl.BlockSpec((None, MOD_ROWS, tn), lambda l, j: (l, 0, j)),
        out_shape=jax.ShapeDtypeStruct((DEPTH, MOD_ROWS, 3 * D_MODEL), F32),
        compiler_params=_params(("parallel", "parallel")),
        name="modulation",
    )(cv, w_mod, b_mod.reshape(DEPTH, 1, 3 * D_MODEL))


def _modulated_norm(x, g, mod_ref, tm, row0, rows_per_mod):
    row = row0 + (pl.program_id(0) * tm) // rows_per_mod
    ms = jnp.mean(x * x, axis=-1, keepdims=True)
    y = x * lax.rsqrt(ms + EPS) * g
    shift = mod_ref[pl.ds(row, 1), 0:D_MODEL]
    scale = mod_ref[pl.ds(row, 1), D_MODEL:2 * D_MODEL]
    return (y * (1.0 + scale) + shift).astype(BF16), row


def _inproj_kernel(x_ref, mod_ref, g_ref, w_ref, dft_ref,
                   xcs_ref, fz_ref, ur_ref, ua_ref, *kv_refs, tm, row0, rows_per_mod):
    h, _ = _modulated_norm(x_ref[...], g_ref[...], mod_ref, tm, row0, rows_per_mod)

    def mm(c0, c1):
        return jnp.dot(h, w_ref[:, c0:c1].astype(BF16), preferred_element_type=F32)

    f = mm(C_FX, C_RQ)
    xcs_ref[...] = jnp.dot(f[:, :FOURIER_W].astype(BF16), dft_ref[...],
                           preferred_element_type=F32).astype(BF16)
    fz_ref[...] = f[:, FOURIER_W:].astype(BF16)
    ur_ref[...] = mm(C_RQ, C_RZ_END).astype(BF16)
    ua_ref[:, 0:ATT_W] = mm(C_AQ, C_AK).astype(BF16)
    ua_ref[:, ATT_W:2 * ATT_W] = mm(C_AZ, C_MG).astype(BF16)
    kv = mm(C_AK, C_AZ)
    if kv_refs:
        kv_refs[0][...] = kv[:, :ATT_KV_W]
        kv_refs[1][...] = kv[:, ATT_KV_W:]
    ua_ref[:, 2 * ATT_W:] = kv.astype(BF16)


def _inproj(x2, layer, mod, g_pre, w_in, dft, *, row0, rows_per_mod, emit_kv):
    n = x2.shape[0]
    tm = TM_INPROJ
    row_spec = lambda w: pl.BlockSpec((tm, w), lambda i: (i, 0))
    widths = (2 * FOURIER_W, FOURIER_W, 4 * RET_W, 2 * ATT_W + 2 * ATT_KV_W)
    out_shape = [jax.ShapeDtypeStruct((n, w), BF16) for w in widths]
    out_shape += [jax.ShapeDtypeStruct((n, ATT_KV_W), F32)] * (2 if emit_kv else 0)
    return pl.pallas_call(
        functools.partial(_inproj_kernel, tm=tm, row0=row0, rows_per_mod=rows_per_mod),
        grid=(n // tm,),
        in_specs=[row_spec(D_MODEL),
                  _layer_spec((MOD_ROWS, 3 * D_MODEL), layer),
                  _layer_spec((1, D_MODEL), layer),
                  _layer_spec((D_MODEL, C_MG), layer),
                  _const_spec((FOURIER_W, 2 * FOURIER_W))],
        out_specs=[row_spec(w) for w in widths] + [row_spec(ATT_KV_W)] * (2 if emit_kv else 0),
        out_shape=out_shape,
        compiler_params=_params(("parallel",)),
        name="inproj",
    )(x2, mod, g_pre, w_in, dft)


def _fourier_kernel(ct_ref, xcs_ref, fz_ref, w_ref, ya_ref, xcat_ref, *, bg, t):
    @pl.when(pl.program_id(1) == 0)
    def _gather():
        for b in range(bg):
            cols = slice(b * FOURIER_W, (b + 1) * FOURIER_W)
            xcat_ref[0:t, cols] = xcs_ref[b, :, 0:FOURIER_W]
            xcat_ref[t:2 * t, cols] = xcs_ref[b, :, FOURIER_W:]

    yr = jnp.dot(ct_ref[...], xcat_ref[...], preferred_element_type=F32).astype(BF16)
    w = w_ref[...].astype(BF16)
    for b in range(bg):
        ya = jnp.dot(yr[:, b * FOURIER_W:(b + 1) * FOURIER_W], w, preferred_element_type=F32)
        ya_ref[b] = (ya * _silu(fz_ref[b].astype(F32))).astype(BF16)


def _fourier(xcs, fz, ct, w_four, layer, *, bg, tq):
    b, t, _ = xcs.shape
    return pl.pallas_call(
        functools.partial(_fourier_kernel, bg=bg, t=t),
        grid=(b // bg, t // tq),
        in_specs=[pl.BlockSpec((tq, 2 * t), lambda i, j: (j, 0)),
                  pl.BlockSpec((bg, t, 2 * FOURIER_W), lambda i, j: (i, 0, 0)),
                  pl.BlockSpec((bg, tq, FOURIER_W), lambda i, j: (i, j, 0)),
                  _layer_spec((FOURIER_W, FOURIER_W), layer)],
        out_specs=pl.BlockSpec((bg, tq, FOURIER_W), lambda i, j: (i, j, 0)),
        out_shape=jax.ShapeDtypeStruct((b, t, FOURIER_W), BF16),
        scratch_shapes=[pltpu.VMEM((2 * t, bg * FOURIER_W), BF16)],
        compiler_params=_params(("parallel", "arbitrary")),
        name="fourier",
    )(ct, xcs, fz, w_four)


def _log_sigmoid(x):
    return jnp.minimum(x, 0.0) - jnp.log(1.0 + jnp.exp(-jnp.abs(x)))


def _head_blocks(a, width):
    lane = lax.broadcasted_iota(jnp.int32, a.shape, 1)
    zero = jnp.zeros_like(a)
    return jnp.concatenate(
        [jnp.where((lane >= h * width) & (lane < (h + 1) * width), a, zero) for h in range(RET_HEADS)],
        axis=0)


def _per_head(dec_ref, layer, direction, head_of):
    base = direction * RET_HEADS
    out = jnp.full(head_of.shape, dec_ref[layer, base + RET_HEADS - 1], F32)
    for h in range(RET_HEADS - 2, -1, -1):
        out = jnp.where(head_of == h, dec_ref[layer, base + h], out)
    return out


def _ret_kernel(*refs, nb, t, layer, rope, has_s0):
    refs = list(refs)
    dec_ref = refs.pop(0)
    ur_ref = refs.pop(0)
    cos_ref = refs.pop(0) if rope else None
    sin_ref = refs.pop(0) if rope else None
    s0_ref = refs.pop(0) if has_s0 else None
    (gn_ref, yb_ref, sfin_ref, kr_ref, dsf_ref, dsb_ref, sfs_ref, sbs_ref, st_ref) = refs
    c = RET_CHUNK
    nc = t // c
    nch = nb * nc
    group = RET_UNROLL

    lane_head = lax.broadcasted_iota(jnp.int32, (1, RET_W), 1) // RET_DK
    lgf = _log_sigmoid(_per_head(dec_ref, layer, 0, lane_head))
    lgb = _log_sigmoid(_per_head(dec_ref, layer, 1, lane_head))
    ri = lax.broadcasted_iota(jnp.int32, (c, RET_W), 0).astype(F32)
    read_f = jnp.exp((ri + 1.0) * lgf)
    read_b = jnp.exp((c - ri) * lgb)
    write_f = jnp.exp((c - 1.0 - ri) * lgf)
    write_b = jnp.exp(ri * lgb)
    carry_f = jnp.exp(c * lgf)
    carry_b = jnp.exp(c * lgb)
    ii = lax.broadcasted_iota(jnp.int32, (RET_HEADS * c, c), 0)
    jj = lax.broadcasted_iota(jnp.int32, (RET_HEADS * c, c), 1)
    row_head = ii // c
    diff = ((ii & (c - 1)) - jj).astype(F32)
    decay = (jnp.where(diff >= 0, jnp.exp(jnp.maximum(diff, 0.0)
                                          * _log_sigmoid(_per_head(dec_ref, layer, 0, row_head))), 0.0)
             + jnp.where(diff <= 0, jnp.exp(jnp.maximum(-diff, 0.0)
                                            * _log_sigmoid(_per_head(dec_ref, layer, 1, row_head))), 0.0))
    r2 = lax.broadcasted_iota(jnp.int32, (RET_W, RET_W), 0)
    c2 = lax.broadcasted_iota(jnp.int32, (RET_W, RET_W), 1)
    same_head = (r2 // RET_DK) == (c2 // RET_DK)
    group_mean = jnp.where(same_head, 1.0 / RET_DK, 0.0).astype(BF16)
    out_head = lax.broadcasted_iota(jnp.int32, (c, RET_W), 1) // RET_DK

    def chunk_pos(ci):
        return ci // nc, pl.multiple_of((ci % nc) * c, c)

    def load_qk(col, b, r0):
        a = ur_ref[b, pl.ds(r0, c), col:col + RET_W].astype(F32)
        if rope:
            cs = cos_ref[pl.ds(r0, c), :]
            sn = sin_ref[pl.ds(r0, c), :]
            a = jnp.concatenate([_rope(a[:, :LANES], cs, sn), _rope(a[:, LANES:], cs, sn)], axis=1)
        return a

    def increments(ci, carry):
        b, r0 = chunk_pos(ci)
        k = load_qk(RET_W, b, r0) * (RET_DK ** -0.5)
        v = ur_ref[b, pl.ds(r0, c), 2 * RET_W:3 * RET_W]
        kr_ref[pl.ds(pl.multiple_of(ci * c, c), c), :] = k.astype(BF16)
        kw = jnp.concatenate([(k * write_f).astype(BF16), (k * write_b).astype(BF16)], axis=1)
        d = lax.dot_general(kw, v, (((0,), (0,)), ((), ())), preferred_element_type=F32)
        dsf_ref[ci] = jnp.where(same_head, d[:RET_W], 0.0)
        dsb_ref[ci] = jnp.where(same_head, d[RET_W:], 0.0)
        return carry

    lax.fori_loop(0, nch, increments, 0, unroll=group)

    def scan(b, direction, ds_ref, out_ref, carry_decay):
        st_ref[...] = jnp.zeros((RET_W, RET_W), F32)
        if has_s0:
            for h in range(RET_HEADS):
                sl = slice(h * RET_DK, (h + 1) * RET_DK)
                st_ref[sl, sl] = s0_ref[b, direction, h]

        def step(n, carry):
            ci = b * nc + (n if direction == 0 else nc - 1 - n)
            out_ref[ci] = st_ref[...].astype(BF16)
            st_ref[...] = carry_decay * st_ref[...] + ds_ref[ci]
            return carry

        lax.fori_loop(0, nc, step, 0, unroll=min(nc, 4))
        for h in range(RET_HEADS):
            sl = slice(h * RET_DK, (h + 1) * RET_DK)
            sfin_ref[b, direction, h] = st_ref[sl, sl]

    for b in range(nb):
        scan(b, 0, dsf_ref, sfs_ref, carry_f)
        scan(b, 1, dsb_ref, sbs_ref, carry_b)

    def chunk_output(ci, b, r0):
        q = load_qk(0, b, r0)
        kb = kr_ref[pl.ds(pl.multiple_of(ci * c, c), c), :]
        v = ur_ref[b, pl.ds(r0, c), 2 * RET_W:3 * RET_W]
        att = lax.dot_general(_head_blocks(q.astype(BF16), RET_DK), kb,
                              (((1,), (1,)), ((), ())), preferred_element_type=F32)
        o4 = jnp.dot((att * decay).astype(BF16), v, preferred_element_type=F32)
        o = o4[(RET_HEADS - 1) * c:]
        for h in range(RET_HEADS - 2, -1, -1):
            o = jnp.where(out_head == h, o4[h * c:(h + 1) * c], o)
        o = o + jnp.dot((q * read_f).astype(BF16), sfs_ref[ci], preferred_element_type=F32)
        return o + jnp.dot((q * read_b).astype(BF16), sbs_ref[ci], preferred_element_type=F32)

    def outputs(it, carry):
        pos = [chunk_pos(it * group + u) for u in range(group)]
        o = jnp.concatenate([chunk_output(it * group + u, *pos[u]) for u in range(group)], axis=0)
        hi, lo = _split_bf16(o * o)
        ms = (jnp.dot(hi, group_mean, preferred_element_type=F32)
              + jnp.dot(lo, group_mean, preferred_element_type=F32))
        y = o * lax.rsqrt(ms + EPS) * gn_ref[...]
        for u, (b, r0) in enumerate(pos):
            z = ur_ref[b, pl.ds(r0, c), 3 * RET_W:4 * RET_W].astype(F32)
            yb_ref[b, pl.ds(r0, c), :] = (y[u * c:(u + 1) * c] * _silu(z)).astype(BF16)
        return carry

    lax.fori_loop(0, nch // group, outputs, 0)


def _retention(ur, layer, dec, gn, *, nb, rope_tabs=None, state=None):
    b, t, _ = ur.shape
    nch = nb * (t // RET_CHUNK)
    assert nch % RET_UNROLL == 0
    rope = rope_tabs is not None
    has_s0 = state is not None
    in_specs = [pl.BlockSpec(memory_space=pltpu.SMEM),
                pl.BlockSpec((nb, t, 4 * RET_W), lambda i: (i, 0, 0))]
    args = [dec, ur]
    if rope:
        in_specs += [_const_spec((t, LANES)), _const_spec((t, LANES))]
        args += list(rope_tabs)
    if has_s0:
        in_specs.append(pl.BlockSpec((nb, None, 2, RET_HEADS, RET_DK, RET_DK),
                                     lambda i: (i, layer, 0, 0, 0, 0)))
        args.append(state)
    in_specs.append(_layer_spec((1, RET_W), layer))
    args.append(gn)
    state_scratch = lambda dt: pltpu.VMEM((nch, RET_W, RET_W), dt)
    return pl.pallas_call(
        functools.partial(_ret_kernel, nb=nb, t=t, layer=layer, rope=rope, has_s0=has_s0),
        grid=(b // nb,),
        in_specs=in_specs,
        out_specs=[pl.BlockSpec((nb, t, RET_W), lambda i: (i, 0, 0)),
                   pl.BlockSpec((nb, 2, RET_HEADS, RET_DK, RET_DK), lambda i: (i, 0, 0, 0, 0))],
        out_shape=[jax.ShapeDtypeStruct((b, t, RET_W), BF16),
                   jax.ShapeDtypeStruct((b, 2, RET_HEADS, RET_DK, RET_DK), F32)],
        scratch_shapes=[pltpu.VMEM((nb * t, RET_W), BF16),
                        state_scratch(F32), state_scratch(F32), state_scratch(BF16), state_scratch(BF16),
                        pltpu.VMEM((RET_W, RET_W), F32)],
        compiler_params=_params(("parallel",)),
        name="retention",
    )(*args)


def _kv_variants(a, ones_block):
    lane_half = lax.broadcasted_iota(jnp.int32, a.shape, 1) // HEAD_DIM
    swapped = pltpu.roll(a, HEAD_DIM, axis=1)
    out = []
    for g in range(ATT_KV_HEADS):
        row = []
        for half in range(2):
            var = jnp.where(lane_half == half, a if half == g else swapped, 0.0).astype(BF16)
            if ones_block:
                ones = jnp.where(lane_half == half, 1.0, 0.0).astype(BF16)
                var = jnp.concatenate([var, ones], axis=1)
            row.append(var)
        out.append(row)
    return out


def _attend_group(q2, keys, values, masks, sink_ref, layer, g, z2):
    rows = q2.shape[0]
    upper = lax.broadcasted_iota(jnp.int32, (rows, 1), 0) < rows // 2
    acc = None
    sink_terms = []
    for half in range(2):
        sink = jnp.where(upper, sink_ref[layer, 4 * g + half], sink_ref[layer, 4 * g + 2 + half]) * LOG2E
        logits = []
        for kpart, mask in zip(keys[half], masks):
            l = lax.dot_general(q2, kpart, (((1,), (1,)), ((), ())), preferred_element_type=F32)
            logits.append(l if mask is None else jnp.where(mask, l, NEG))
        m = sink
        for l in logits:
            m = jnp.maximum(m, jnp.max(l, axis=-1, keepdims=True))
        for l, vpart in zip(logits, values[half]):
            pv = jnp.dot(jnp.exp2(l - m).astype(BF16), vpart, preferred_element_type=F32)
            acc = pv if acc is None else acc + pv
        sink_terms.append(jnp.exp2(sink - m))
    lane = lax.broadcasted_iota(jnp.int32, (rows, LANES), 1)
    den = acc[:, LANES:] + jnp.where(lane < HEAD_DIM, sink_terms[0], sink_terms[1])
    return (acc[:, :LANES] * (1.0 / den) * _silu(z2)).astype(BF16)


def _pair_rows(ref_slice, g):
    return jnp.concatenate([ref_slice(2 * g), ref_slice(2 * g + 1)], axis=0)


def _ctx_attn_kernel(sink_ref, ua_ref, yc_ref, *, nb, t, layer):
    for b in range(nb):
        kv = ua_ref[b, :, 2 * ATT_W:].astype(F32)
        kvar = _kv_variants(kv[:, :LANES] * (HEAD_DIM ** -0.5 * LOG2E), False)
        vvar = _kv_variants(kv[:, LANES:], True)
        for g in range(ATT_KV_HEADS):
            q2 = _pair_rows(lambda p: ua_ref[b, :, p * LANES:(p + 1) * LANES], g)
            z2 = _pair_rows(lambda p: ua_ref[b, :, ATT_W + p * LANES:ATT_W + (p + 1) * LANES], g)
            o = _attend_group(q2, [[kvar[g][0]], [kvar[g][1]]], [[vvar[g][0]], [vvar[g][1]]], [None],
                              sink_ref, layer, g, z2.astype(F32))
            yc_ref[b, :, 2 * g * LANES:(2 * g + 1) * LANES] = o[:t]
            yc_ref[b, :, (2 * g + 1) * LANES:(2 * g + 2) * LANES] = o[t:]


def _ctx_attention(ua, layer, sink):
    b, t, w = ua.shape
    nb = NB_CTX_ATTN
    return pl.pallas_call(
        functools.partial(_ctx_attn_kernel, nb=nb, t=t, layer=layer),
        grid=(b // nb,),
        in_specs=[pl.BlockSpec(memory_space=pltpu.SMEM),
                  pl.BlockSpec((nb, t, w), lambda i: (i, 0, 0))],
        out_specs=pl.BlockSpec((nb, t, ATT_W), lambda i: (i, 0, 0)),
        out_shape=jax.ShapeDtypeStruct((b, t, ATT_W), BF16),
        compiler_params=_params(("parallel",)),
        name="ctx_attention",
    )(sink, ua)


def _lat_attn_kernel(sink_ref, qz_ref, kv_ref, ck_ref, cv_ref, cos_ref, sin_ref, yc_ref,
                     kl_ref, vl_ref, kc_ref, vc_ref, *, t, tq, layer):
    j = pl.program_id(1)
    scale = HEAD_DIM ** -0.5 * LOG2E

    @pl.when(j == 0)
    def _prepare():
        kv = kv_ref[...].astype(F32)
        kvar = _kv_variants(_rope(kv[:, :LANES], cos_ref[...], sin_ref[...]) * scale, False)
        vvar = _kv_variants(kv[:, LANES:], True)
        cvar = _kv_variants(ck_ref[...] * scale, False)
        dvar = _kv_variants(cv_ref[...], True)
        for g in range(ATT_KV_HEADS):
            for half in range(2):
                i = 2 * g + half
                for ref, var in ((kl_ref, kvar), (vl_ref, vvar)):
                    pad = jnp.zeros((WINDOW, ref.shape[-1]), BF16)
                    ref[i, 0:WINDOW, :] = pad
                    ref[i, WINDOW:WINDOW + t, :] = var[g][half]
                    ref[i, WINDOW + t:, :] = pad
                kc_ref[i] = cvar[g][half]
                vc_ref[i] = dvar[g][half]

    r0 = pl.multiple_of(j * tq, tq)
    nloc = tq + 2 * WINDOW
    rr = lax.broadcasted_iota(jnp.int32, (2 * tq, nloc), 0) & (tq - 1)
    ss = lax.broadcasted_iota(jnp.int32, (2 * tq, nloc), 1)
    band = ((ss - rr >= 0) & (ss - rr <= 2 * WINDOW)
            & (ss >= WINDOW - j * tq) & (ss < t + WINDOW - j * tq))
    cs = cos_ref[pl.ds(r0, tq), :]
    sn = sin_ref[pl.ds(r0, tq), :]
    for g in range(ATT_KV_HEADS):
        q2 = _pair_rows(
            lambda p: _rope(qz_ref[:, p * LANES:(p + 1) * LANES].astype(F32), cs, sn).astype(BF16), g)
        z2 = _pair_rows(lambda p: qz_ref[:, ATT_W + p * LANES:ATT_W + (p + 1) * LANES], g)
        keys = [[kl_ref[2 * g + half, pl.ds(r0, nloc), :], kc_ref[2 * g + half]] for half in range(2)]
        vals = [[vl_ref[2 * g + half, pl.ds(r0, nloc), :], vc_ref[2 * g + half]] for half in range(2)]
        o = _attend_group(q2, keys, vals, [band, None], sink_ref, layer, g, z2.astype(F32))
        yc_ref[:, 2 * g * LANES:(2 * g + 1) * LANES] = o[:tq]
        yc_ref[:, (2 * g + 1) * LANES:(2 * g + 2) * LANES] = o[tq:]


def _lat_attention(ua, cache_k, cache_v, layer, sink, rope_tabs):
    b, t, _ = ua.shape
    past = cache_k.shape[2]
    tq = TQ_LATENT
    nvar = 2 * ATT_KV_HEADS
    cache_spec = pl.BlockSpec((None, None, past, ATT_KV_W), lambda i, j: (i, layer, 0, 0))
    return pl.pallas_call(
        functools.partial(_lat_attn_kernel, t=t, tq=tq, layer=layer),
        grid=(b, t // tq),
        in_specs=[pl.BlockSpec(memory_space=pltpu.SMEM),
                  pl.BlockSpec((None, tq, 2 * ATT_W), lambda i, j: (i, j, 0)),
                  pl.BlockSpec((None, t, 2 * ATT_KV_W), lambda i, j: (i, 0, 2 * ATT_W // (2 * ATT_KV_W))),
                  cache_spec, cache_spec,
                  _const_spec((t, LANES)), _const_spec((t, LANES))],
        out_specs=pl.BlockSpec((None, tq, ATT_W), lambda i, j: (i, j, 0)),
        out_shape=jax.ShapeDtypeStruct((b, t, ATT_W), BF16),
        scratch_shapes=[pltpu.VMEM((nvar, t + 2 * WINDOW, LANES), BF16),
                        pltpu.VMEM((nvar, t + 2 * WINDOW, 2 * LANES), BF16),
                        pltpu.VMEM((nvar, past, LANES), BF16),
                        pltpu.VMEM((nvar, past, 2 * LANES), BF16)],
        compiler_params=_params(("parallel", "arbitrary")),
        name="lat_attention",
    )(sink, ua, ua, cache_k, cache_v, *rope_tabs)


def _outproj_kernel(x_ref, ya_ref, yb_ref, yc_ref, mod_ref, gpre_ref, gpost_ref,
                    wg_ref, wgt_ref, wa_ref, wb_ref, wc_ref, wo_ref, o_ref, *, tm, row0, rows_per_mod):
    x = x_ref[...]
    h, row = _modulated_norm(x, gpre_ref[...], mod_ref, tm, row0, rows_per_mod)

    def gate_logits(c):
        lo, hi = c * D_MODEL, min((c + 1) * D_MODEL, C_MG)
        mg = jnp.dot(h, wg_ref[:, lo:hi].astype(BF16), preferred_element_type=F32)
        if hi - lo < D_MODEL:
            tail = jnp.dot(h, wgt_ref[...].astype(BF16), preferred_element_type=F32)
            mg = jnp.concatenate([mg, tail], axis=1)
        return mg

    merged = None
    for c, (y_ref, w_ref) in enumerate(((ya_ref, wa_ref), (yb_ref, wb_ref), (yc_ref, wc_ref))):
        term = _sigmoid(gate_logits(c)) * jnp.dot(y_ref[...], w_ref[...].astype(BF16),
                                                  preferred_element_type=F32)
        merged = term if merged is None else merged + term
    out = jnp.dot(merged.astype(BF16), wo_ref[...].astype(BF16), preferred_element_type=F32)
    ms = jnp.mean(out * out, axis=-1, keepdims=True)
    normed = out * lax.rsqrt(ms + EPS) * gpost_ref[...]
    gate = mod_ref[pl.ds(row, 1), 2 * D_MODEL:3 * D_MODEL]
    o_ref[...] = x + gate * normed


def _outproj(x2, ya, yb, yc, layer, mod, g_pre, g_post, w_in, wa, wb, wc, wo, *, row0, rows_per_mod):
    n = x2.shape[0]
    tm = TM_OUTPROJ
    row_spec = lambda w: pl.BlockSpec((tm, w), lambda i: (i, 0))
    return pl.pallas_call(
        functools.partial(_outproj_kernel, tm=tm, row0=row0, rows_per_mod=rows_per_mod),
        grid=(n // tm,),
        in_specs=[row_spec(D_MODEL), row_spec(FOURIER_W), row_spec(RET_W), row_spec(ATT_W),
                  _layer_spec((MOD_ROWS, 3 * D_MODEL), layer),
                  _layer_spec((1, D_MODEL), layer), _layer_spec((1, D_MODEL), layer),
                  _layer_spec((D_MODEL, C_MG), layer, col_block=1),
                  _layer_spec((D_MODEL, C_END - 2 * C_MG), layer, col_block=2 * C_MG // (C_END - 2 * C_MG)),
                  _layer_spec((FOURIER_W, D_MODEL), layer), _layer_spec((RET_W, D_MODEL), layer),
                  _layer_spec((ATT_W, D_MODEL), layer), _layer_spec((D_MODEL, D_MODEL), layer)],
        out_specs=row_spec(D_MODEL),
        out_shape=jax.ShapeDtypeStruct((n, D_MODEL), F32),
        compiler_params=_params(("parallel",)),
        name="outproj",
    )(x2, ya, yb, yc, mod, g_pre, g_post, w_in, w_in, wa, wb, wc, wo)


def _layer(x, layer, mod, p, *, latent, dft_chan, dft_pos, rope_tabs=None, cache_k=None, cache_v=None,
           state=None):
    b, t, _ = x.shape
    n = b * t
    row0, rows_per_mod = (1, t) if latent else (0, n)
    x2 = x.reshape(n, D_MODEL)
    xcs, fz, ur, ua, *kv32 = _inproj(x2, layer, mod, p["g_pre"], p["w_in"], dft_chan,
                                     row0=row0, rows_per_mod=rows_per_mod, emit_kv=not latent)
    ya = _fourier(xcs.reshape(b, t, -1), fz.reshape(b, t, -1), dft_pos, p["w_four"], layer,
                  bg=b if latent else BG_CTX_FOURIER, tq=min(t, TQ_FOURIER))
    ur3 = ur.reshape(b, t, -1)
    ua3 = ua.reshape(b, t, -1)
    if latent:
        yb, s_fin = _retention(ur3, layer, p["dec"], p["gn"], nb=1, rope_tabs=rope_tabs, state=state)
        yc = _lat_attention(ua3, cache_k, cache_v, layer, p["sink"], rope_tabs)
    else:
        yb, s_fin = _retention(ur3, layer, p["dec"], p["gn"], nb=NB_CTX_RET)
        yc = _ctx_attention(ua3, layer, p["sink"])
    out = _outproj(x2, ya.reshape(n, -1), yb.reshape(n, -1), yc.reshape(n, -1), layer, mod,
                   p["g_pre"], p["g_post"], p["w_in"], p["w_pa"], p["w_pb"], p["w_pc"], p["w_out"],
                   row0=row0, rows_per_mod=rows_per_mod)
    return out.reshape(b, t, D_MODEL), kv32, s_fin


def kernel(x_prompt, x_sample, cache_k, cache_v, state_ret, c, c_ctx, w_mod, b_mod, g_pre, g_post, w_in,
           w_four, ret_decay, ret_gn, attn_sink, w_branch_a, w_branch_b, w_branch_c, w_out):
    batch, seq, _ = x_prompt.shape
    dec_batch, dec_seq, _ = x_sample.shape
    past = cache_k.shape[2]
    assert 1 + dec_batch <= MOD_ROWS

    cv = jnp.zeros((MOD_ROWS, D_MODEL), F32).at[0].set(c_ctx).at[1:1 + dec_batch].set(c)
    mod = _modulation(cv, w_mod, b_mod)

    dft_chan, dft_ctx = _dft_tables(seq)
    _, dft_lat = _dft_tables(dec_seq)
    rope_tabs = _rope_tables(dec_seq)
    ck = cache_k.reshape(dec_batch, DEPTH, past, ATT_KV_W)
    cvv = cache_v.reshape(dec_batch, DEPTH, past, ATT_KV_W)

    p = dict(
        g_pre=g_pre.reshape(DEPTH, 1, D_MODEL), g_post=g_post.reshape(DEPTH, 1, D_MODEL),
        w_in=w_in, w_four=w_four, w_pa=w_branch_a, w_pb=w_branch_b, w_pc=w_branch_c, w_out=w_out,
        dec=ret_decay.reshape(DEPTH, 2 * RET_HEADS),
        gn=ret_gn.reshape(DEPTH, 1, RET_W),
        sink=attn_sink.reshape(DEPTH, ATT_Q_HEADS))

    xp = x_prompt
    ks, vs, ss = [], [], []
    for l in range(DEPTH):
        xp, (k32, v32), s_fin = _layer(xp, l, mod, p, latent=False, dft_chan=dft_chan, dft_pos=dft_ctx)
        ks.append(k32.reshape(batch, seq, ATT_KV_HEADS, HEAD_DIM))
        vs.append(v32.reshape(batch, seq, ATT_KV_HEADS, HEAD_DIM))
        ss.append(s_fin)

    xs = x_sample
    for l in range(DEPTH):
        xs, _, _ = _layer(xs, l, mod, p, latent=True, dft_chan=dft_chan, dft_pos=dft_lat,
                             rope_tabs=rope_tabs, cache_k=ck, cache_v=cvv, state=state_ret)

    return (xp, xs, jnp.stack(ks, axis=1), jnp.stack(vs, axis=1), jnp.stack(ss, axis=1))
```

```python
import functools
import math

import numpy as np
import jax
import jax.numpy as jnp
from jax import lax
from jax.experimental import pallas as pl
from jax.experimental.pallas import tpu as pltpu

F32 = jnp.float32
BF16 = jnp.bfloat16

D_MODEL = 1024
DEPTH = 2
GRID_W = 64
HEAD_DIM = 64
FOURIER_GROUPS = 4
FOURIER_GROUP_W = 64
FOURIER_W = FOURIER_GROUPS * FOURIER_GROUP_W
RET_HEADS = 4
RET_DK = 64
RET_W = RET_HEADS * RET_DK
RET_CHUNK = 128
ATT_Q_HEADS = 8
ATT_KV_HEADS = 2
ATT_W = ATT_Q_HEADS * HEAD_DIM
ATT_KV_W = ATT_KV_HEADS * HEAD_DIM
WINDOW = 128
ROPE_BASE = 10000.0
EPS = 1e-6
MOD_ROWS = 8
LANES = 128
NEG = -1e30
LOG2E = math.log2(math.e)
VMEM_LIMIT = 56 * 1024 * 1024

C_FX, C_RQ, C_RZ_END = 0, 512, 1536
C_AQ, C_AK, C_AV, C_AZ, C_MG, C_END = 1536, 2048, 2176, 2304, 2816, 5888

TM_INPROJ = 1024
TM_OUTPROJ = 1024
SUB_ROWS = 512
TQ_LATENT = 256
TQ_FOURIER = 512
NB_CTX_RET = 8
NB_CTX_ATTN = 8
BG_CTX_FOURIER = 8
RET_UNROLL = 4


def _sigmoid(x):
    return 0.5 * jnp.tanh(0.5 * x) + 0.5


def _silu(x):
    return x * _sigmoid(x)


def _params(sem):
    return pltpu.CompilerParams(dimension_semantics=sem, vmem_limit_bytes=VMEM_LIMIT)


def _const_spec(shape):
    nd = len(shape)
    return pl.BlockSpec(shape, lambda *_: (0,) * nd, pipeline_mode=pl.Buffered(1))


def _layer_spec(shape, layer, col_block=0):
    idx = (layer,) + (0,) * (len(shape) - 1) + (col_block,)
    return pl.BlockSpec((None,) + tuple(shape), lambda *_: idx, pipeline_mode=pl.Buffered(1))


def _dft_tables(t):
    c = np.arange(FOURIER_GROUP_W)
    ang = 2.0 * np.pi * ((c[:, None] * c[None, :]) % FOURIER_GROUP_W) / FOURIER_GROUP_W
    eye = np.eye(FOURIER_GROUPS)
    s64 = FOURIER_GROUP_W ** -0.5
    chan = np.concatenate([np.kron(eye, np.cos(ang) * s64), np.kron(eye, np.sin(ang) * s64)], axis=1)
    p = np.arange(t)
    angt = 2.0 * np.pi * ((p[:, None] * p[None, :]) % t) / t
    pos = np.concatenate([np.cos(angt), -np.sin(angt)], axis=1) * (t ** -0.5)
    return jnp.asarray(chan, F32).astype(BF16), jnp.asarray(pos, F32).astype(BF16)


def _rope_tables(t):
    quarter = HEAD_DIM // 4
    lane = np.arange(LANES) % HEAD_DIM
    inv = ROPE_BASE ** (-(lane % quarter).astype(np.float64) / quarter)
    n = np.arange(t)
    pos = np.where(lane[None, :] < HEAD_DIM // 2, (n // GRID_W)[:, None], (n % GRID_W)[:, None])
    ang = pos.astype(np.float64) * inv[None, :]
    sign = np.where((lane % (2 * quarter)) < quarter, -1.0, 1.0)
    return jnp.asarray(np.cos(ang), F32), jnp.asarray(np.sin(ang) * sign[None, :], F32)


def _rope(x, cos, sin):
    lane = lax.broadcasted_iota(jnp.int32, x.shape, 1)
    first = (lane & 31) < 16
    partner = jnp.where(first, pltpu.roll(x, LANES - 16, axis=1), pltpu.roll(x, 16, axis=1))
    return x * cos + partner * sin


def _split_bf16(x):
    hi = x.astype(BF16)
    return hi, (x - hi.astype(F32)).astype(BF16)


def _mod_kernel(cv_ref, w_ref, b_ref, o_ref):
    a_hi, a_lo = _split_bf16(_silu(cv_ref[...]))
    w_hi, w_lo = _split_bf16(w_ref[...])
    dot = functools.partial(jnp.dot, preferred_element_type=F32)
    bias = b_ref[pl.ds(pl.program_id(0), 1), :]
    o_ref[...] = dot(a_hi, w_hi) + (dot(a_lo, w_hi) + dot(a_hi, w_lo)) + bias


def _modulation(cv, w_mod, b_mod):
    tn = 1024
    return pl.pallas_call(
        _mod_kernel,
        grid=(DEPTH, 3 * D_MODEL // tn),
        in_specs=[pl.BlockSpec((MOD_ROWS, D_MODEL), lambda l, j: (0, 0)),
                  pl.BlockSpec((None, D_MODEL, tn), lambda l, j: (l, 0, j)),
                  pl.BlockSpec((DEPTH, tn), lambda l, j: (0, j))],
        out_specs=pl.BlockSpec((None, MOD_ROWS, tn), lambda l, j: (l, 0, j)),
        out_shape=jax.ShapeDtypeStruct((DEPTH, MOD_ROWS, 3 * D_MODEL), F32),
        compiler_params=_params(("parallel", "parallel")),
        name="modulation",
    )(cv, w_mod, b_mod)


def _modulated_norm(x, g, mod_ref, tm, row0, rows_per_mod):
    row = row0 + (pl.program_id(0) * tm) // rows_per_mod
    ms = jnp.mean(x * x, axis=-1, keepdims=True)
    y = x * lax.rsqrt(ms + EPS) * g
    shift = mod_ref[pl.ds(row, 1), 0:D_MODEL]
    scale = mod_ref[pl.ds(row, 1), D_MODEL:2 * D_MODEL]
    return (y * (1.0 + scale) + shift).astype(BF16), row


def _inproj_kernel(x_ref, mod_ref, g_ref, w_ref, dft_ref,
                   xcs_ref, fz_ref, ur_ref, ua_ref, *kv_refs, layer, tm, row0, rows_per_mod):
    for r in range(tm // SUB_ROWS):
        rows = slice(r * SUB_ROWS, (r + 1) * SUB_ROWS)
        h, _ = _modulated_norm(x_ref[rows, :], g_ref[layer:layer + 1, :], mod_ref, tm, row0, rows_per_mod)

        def mm(c0, c1):
            return jnp.dot(h, w_ref[:, c0:c1].astype(BF16), preferred_element_type=F32)

        f = mm(C_FX, C_RQ)
        xcs_ref[rows, :] = jnp.dot(f[:, :FOURIER_W].astype(BF16), dft_ref[...],
                                   preferred_element_type=F32).astype(BF16)
        fz_ref[rows, :] = f[:, FOURIER_W:].astype(BF16)
        ur_ref[rows, :] = mm(C_RQ, C_RZ_END).astype(BF16)
        ua_ref[rows, 0:ATT_W] = mm(C_AQ, C_AK).astype(BF16)
        ua_ref[rows, ATT_W:2 * ATT_W] = mm(C_AZ, C_MG).astype(BF16)
        kv = mm(C_AK, C_AZ)
        if kv_refs:
            kv_refs[0][rows, :] = kv[:, :ATT_KV_W]
            kv_refs[1][rows, :] = kv[:, ATT_KV_W:]
        ua_ref[rows, 2 * ATT_W:] = kv.astype(BF16)


def _inproj(x2, layer, mod, g_pre, w_in, dft, *, row0, rows_per_mod, emit_kv):
    n = x2.shape[0]
    tm = TM_INPROJ
    row_spec = lambda w: pl.BlockSpec((tm, w), lambda i: (i, 0))
    widths = (2 * FOURIER_W, FOURIER_W, 4 * RET_W, 2 * ATT_W + 2 * ATT_KV_W)
    out_shape = [jax.ShapeDtypeStruct((n, w), BF16) for w in widths]
    out_shape += [jax.ShapeDtypeStruct((n, ATT_KV_W), F32)] * (2 if emit_kv else 0)
    return pl.pallas_call(
        functools.partial(_inproj_kernel, layer=layer, tm=tm, row0=row0, rows_per_mod=rows_per_mod),
        grid=(n // tm,),
        in_specs=[row_spec(D_MODEL),
                  _layer_spec((MOD_ROWS, 3 * D_MODEL), layer),
                  _const_spec((DEPTH, D_MODEL)),
                  _layer_spec((D_MODEL, C_MG), layer),
                  _const_spec((FOURIER_W, 2 * FOURIER_W))],
        out_specs=[row_spec(w) for w in widths] + [row_spec(ATT_KV_W)] * (2 if emit_kv else 0),
        out_shape=out_shape,
        compiler_params=_params(("parallel",)),
        name="inproj",
    )(x2, mod, g_pre, w_in, dft)


def _fourier_kernel(ct_ref, xcs_ref, fz_ref, w_ref, ya_ref, xcat_ref, *, bg, t):
    @pl.when(pl.program_id(1) == 0)
    def _gather():
        for b in range(bg):
            cols = slice(b * FOURIER_W, (b + 1) * FOURIER_W)
            xcat_ref[0:t, cols] = xcs_ref[b, :, 0:FOURIER_W]
            xcat_ref[t:2 * t, cols] = xcs_ref[b, :, FOURIER_W:]

    yr = jnp.dot(ct_ref[...], xcat_ref[...], preferred_element_type=F32).astype(BF16)
    w = w_ref[...].astype(BF16)
    for b in range(bg):
        ya = jnp.dot(yr[:, b * FOURIER_W:(b + 1) * FOURIER_W], w, preferred_element_type=F32)
        ya_ref[b] = (ya * _silu(fz_ref[b].astype(F32))).astype(BF16)


def _fourier(xcs, fz, ct, w_four, layer, *, bg, tq):
    b, t, _ = xcs.shape
    return pl.pallas_call(
        functools.partial(_fourier_kernel, bg=bg, t=t),
        grid=(b // bg, t // tq),
        in_specs=[pl.BlockSpec((tq, 2 * t), lambda i, j: (j, 0)),
                  pl.BlockSpec((bg, t, 2 * FOURIER_W), lambda i, j: (i, 0, 0)),
                  pl.BlockSpec((bg, tq, FOURIER_W), lambda i, j: (i, j, 0)),
                  _layer_spec((FOURIER_W, FOURIER_W), layer)],
        out_specs=pl.BlockSpec((bg, tq, FOURIER_W), lambda i, j: (i, j, 0)),
        out_shape=jax.ShapeDtypeStruct((b, t, FOURIER_W), BF16),
        scratch_shapes=[pltpu.VMEM((2 * t, bg * FOURIER_W), BF16)],
        compiler_params=_params(("parallel", "arbitrary")),
        name="fourier",
    )(ct, xcs, fz, w_four)


def _log_sigmoid(x):
    return jnp.minimum(x, 0.0) - jnp.log(1.0 + jnp.exp(-jnp.abs(x)))


def _head_blocks(a, width):
    lane = lax.broadcasted_iota(jnp.int32, a.shape, 1)
    zero = jnp.zeros_like(a)
    return jnp.concatenate(
        [jnp.where((lane >= h * width) & (lane < (h + 1) * width), a, zero) for h in range(RET_HEADS)],
        axis=0)


def _per_head(dec_ref, layer, direction, head_of):
    out = jnp.full(head_of.shape, dec_ref[layer, direction, RET_HEADS - 1], F32)
    for h in range(RET_HEADS - 2, -1, -1):
        out = jnp.where(head_of == h, dec_ref[layer, direction, h], out)
    return out


def _ret_kernel(*refs, nb, t, layer, rope, has_s0):
    refs = list(refs)
    dec_ref = refs.pop(0)
    ur_ref = refs.pop(0)
    cos_ref = refs.pop(0) if rope else None
    sin_ref = refs.pop(0) if rope else None
    s0_ref = refs.pop(0) if has_s0 else None
    (gn_ref, yb_ref, sfin_ref, kr_ref, dsf_ref, dsb_ref, sfs_ref, sbs_ref, st_ref) = refs
    c = RET_CHUNK
    nc = t // c
    nch = nb * nc
    group = RET_UNROLL

    lane_head = lax.broadcasted_iota(jnp.int32, (1, RET_W), 1) // RET_DK
    lgf = _log_sigmoid(_per_head(dec_ref, layer, 0, lane_head))
    lgb = _log_sigmoid(_per_head(dec_ref, layer, 1, lane_head))
    ri = lax.broadcasted_iota(jnp.int32, (c, RET_W), 0).astype(F32)
    read_f = jnp.exp((ri + 1.0) * lgf)
    read_b = jnp.exp((c - ri) * lgb)
    write_f = jnp.exp((c - 1.0 - ri) * lgf)
    write_b = jnp.exp(ri * lgb)
    carry_f = jnp.exp(c * lgf)
    carry_b = jnp.exp(c * lgb)
    ii = lax.broadcasted_iota(jnp.int32, (RET_HEADS * c, c), 0)
    jj = lax.broadcasted_iota(jnp.int32, (RET_HEADS * c, c), 1)
    row_head = ii // c
    diff = ((ii & (c - 1)) - jj).astype(F32)
    decay = (jnp.where(diff >= 0, jnp.exp(jnp.maximum(diff, 0.0)
                                          * _log_sigmoid(_per_head(dec_ref, layer, 0, row_head))), 0.0)
             + jnp.where(diff <= 0, jnp.exp(jnp.maximum(-diff, 0.0)
                                            * _log_sigmoid(_per_head(dec_ref, layer, 1, row_head))), 0.0))
    r2 = lax.broadcasted_iota(jnp.int32, (RET_W, RET_W), 0)
    c2 = lax.broadcasted_iota(jnp.int32, (RET_W, RET_W), 1)
    same_head = (r2 // RET_DK) == (c2 // RET_DK)
    group_mean = jnp.where(same_head, 1.0 / RET_DK, 0.0).astype(BF16)
    out_head = lax.broadcasted_iota(jnp.int32, (c, RET_W), 1) // RET_DK

    def chunk_pos(ci):
        return ci // nc, pl.multiple_of((ci % nc) * c, c)

    def load_qk(col, b, r0):
        a = ur_ref[b, pl.ds(r0, c), col:col + RET_W].astype(F32)
        if rope:
            cs = cos_ref[pl.ds(r0, c), :]
            sn = sin_ref[pl.ds(r0, c), :]
            a = jnp.concatenate([_rope(a[:, :LANES], cs, sn), _rope(a[:, LANES:], cs, sn)], axis=1)
        return a

    def increments(ci, carry):
        b, r0 = chunk_pos(ci)
        k = load_qk(RET_W, b, r0) * (RET_DK ** -0.5)
        v = ur_ref[b, pl.ds(r0, c), 2 * RET_W:3 * RET_W]
        kr_ref[pl.ds(pl.multiple_of(ci * c, c), c), :] = k.astype(BF16)
        kw = jnp.concatenate([(k * write_f).astype(BF16), (k * write_b).astype(BF16)], axis=1)
        d = lax.dot_general(kw, v, (((0,), (0,)), ((), ())), preferred_element_type=F32)
        dsf_ref[ci] = jnp.where(same_head, d[:RET_W], 0.0)
        dsb_ref[ci] = jnp.where(same_head, d[RET_W:], 0.0)
        return carry

    lax.fori_loop(0, nch, increments, 0, unroll=group)

    def scan(b, direction, ds_ref, out_ref, carry_decay):
        st_ref[...] = jnp.zeros((RET_W, RET_W), F32)
        if has_s0:
            for h in range(RET_HEADS):
                sl = slice(h * RET_DK, (h + 1) * RET_DK)
                st_ref[sl, sl] = s0_ref[b, direction, h]

        def step(n, carry):
            ci = b * nc + (n if direction == 0 else nc - 1 - n)
            out_ref[ci] = st_ref[...].astype(BF16)
            st_ref[...] = carry_decay * st_ref[...] + ds_ref[ci]
            return carry

        lax.fori_loop(0, nc, step, 0, unroll=min(nc, 4))
        for h in range(RET_HEADS):
            sl = slice(h * RET_DK, (h + 1) * RET_DK)
            sfin_ref[b, direction, h] = st_ref[sl, sl]

    for b in range(nb):
        scan(b, 0, dsf_ref, sfs_ref, carry_f)
        scan(b, 1, dsb_ref, sbs_ref, carry_b)

    def chunk_output(ci, b, r0):
        q = load_qk(0, b, r0)
        kb = kr_ref[pl.ds(pl.multiple_of(ci * c, c), c), :]
        v = ur_ref[b, pl.ds(r0, c), 2 * RET_W:3 * RET_W]
        att = lax.dot_general(_head_blocks(q.astype(BF16), RET_DK), kb,
                              (((1,), (1,)), ((), ())), preferred_element_type=F32)
        o4 = jnp.dot((att * decay).astype(BF16), v, preferred_element_type=F32)
        o = o4[(RET_HEADS - 1) * c:]
        for h in range(RET_HEADS - 2, -1, -1):
            o = jnp.where(out_head == h, o4[h * c:(h + 1) * c], o)
        o = o + jnp.dot((q * read_f).astype(BF16), sfs_ref[ci], preferred_element_type=F32)
        return o + jnp.dot((q * read_b).astype(BF16), sbs_ref[ci], preferred_element_type=F32)

    def outputs(it, carry):
        pos = [chunk_pos(it * group + u) for u in range(group)]
        o = jnp.concatenate([chunk_output(it * group + u, *pos[u]) for u in range(group)], axis=0)
        hi, lo = _split_bf16(o * o)
        ms = (jnp.dot(hi, group_mean, preferred_element_type=F32)
              + jnp.dot(lo, group_mean, preferred_element_type=F32))
        y = o * lax.rsqrt(ms + EPS) * gn_ref[layer:layer + 1, :]
        for u, (b, r0) in enumerate(pos):
            z = ur_ref[b, pl.ds(r0, c), 3 * RET_W:4 * RET_W].astype(F32)
            yb_ref[b, pl.ds(r0, c), :] = (y[u * c:(u + 1) * c] * _silu(z)).astype(BF16)
        return carry

    lax.fori_loop(0, nch // group, outputs, 0)


def _retention(ur, layer, dec, gn, *, nb, rope_tabs=None, state=None):
    b, t, _ = ur.shape
    nch = nb * (t // RET_CHUNK)
    assert nch % RET_UNROLL == 0
    rope = rope_tabs is not None
    has_s0 = state is not None
    in_specs = [pl.BlockSpec(memory_space=pltpu.SMEM),
                pl.BlockSpec((nb, t, 4 * RET_W), lambda i: (i, 0, 0))]
    args = [dec, ur]
    if rope:
        in_specs += [_const_spec((t, LANES)), _const_spec((t, LANES))]
        args += list(rope_tabs)
    if has_s0:
        in_specs.append(pl.BlockSpec((nb, None, 2, RET_HEADS, RET_DK, RET_DK),
                                     lambda i: (i, layer, 0, 0, 0, 0)))
        args.append(state)
    in_specs.append(_const_spec((DEPTH, RET_W)))
    args.append(gn)
    state_scratch = lambda dt: pltpu.VMEM((nch, RET_W, RET_W), dt)
    return pl.pallas_call(
        functools.partial(_ret_kernel, nb=nb, t=t, layer=layer, rope=rope, has_s0=has_s0),
        grid=(b // nb,),
        in_specs=in_specs,
        out_specs=[pl.BlockSpec((nb, t, RET_W), lambda i: (i, 0, 0)),
                   pl.BlockSpec((nb, 2, RET_HEADS, RET_DK, RET_DK), lambda i: (i, 0, 0, 0, 0))],
        out_shape=[jax.ShapeDtypeStruct((b, t, RET_W), BF16),
                   jax.ShapeDtypeStruct((b, 2, RET_HEADS, RET_DK, RET_DK), F32)],
        scratch_shapes=[pltpu.VMEM((nb * t, RET_W), BF16),
                        state_scratch(F32), state_scratch(F32), state_scratch(BF16), state_scratch(BF16),
                        pltpu.VMEM((RET_W, RET_W), F32)],
        compiler_params=_params(("parallel",)),
        name="retention",
    )(*args)


def _kv_variants(a, ones_block):
    lane_half = lax.broadcasted_iota(jnp.int32, a.shape, 1) // HEAD_DIM
    swapped = pltpu.roll(a, HEAD_DIM, axis=1)
    out = []
    for g in range(ATT_KV_HEADS):
        row = []
        for half in range(2):
            var = jnp.where(lane_half == half, a if half == g else swapped, 0.0).astype(BF16)
            if ones_block:
                ones = jnp.where(lane_half == half, 1.0, 0.0).astype(BF16)
                var = jnp.concatenate([var, ones], axis=1)
            row.append(var)
        out.append(row)
    return out


def _attend_group(q2, keys, values, masks, sink_ref, layer, g, z2):
    rows = q2.shape[0]
    upper = lax.broadcasted_iota(jnp.int32, (rows, 1), 0) < rows // 2
    acc = None
    sink_terms = []
    for half in range(2):
        sink = jnp.where(upper, sink_ref[layer, g, half], sink_ref[layer, g, 2 + half]) * LOG2E
        logits = []
        for kpart, mask in zip(keys[half], masks):
            l = lax.dot_general(q2, kpart, (((1,), (1,)), ((), ())), preferred_element_type=F32)
            logits.append(l if mask is None else jnp.where(mask, l, NEG))
        m = sink
        for l in logits:
            m = jnp.maximum(m, jnp.max(l, axis=-1, keepdims=True))
        for l, vpart in zip(logits, values[half]):
            pv = jnp.dot(jnp.exp2(l - m).astype(BF16), vpart, preferred_element_type=F32)
            acc = pv if acc is None else acc + pv
        sink_terms.append(jnp.exp2(sink - m))
    lane = lax.broadcasted_iota(jnp.int32, (rows, LANES), 1)
    den = acc[:, LANES:] + jnp.where(lane < HEAD_DIM, sink_terms[0], sink_terms[1])
    return (acc[:, :LANES] * (1.0 / den) * _silu(z2)).astype(BF16)


def _pair_rows(ref_slice, g):
    return jnp.concatenate([ref_slice(2 * g), ref_slice(2 * g + 1)], axis=0)


def _ctx_attn_kernel(sink_ref, ua_ref, yc_ref, *, nb, t, layer):
    for b in range(nb):
        kv = ua_ref[b, :, 2 * ATT_W:].astype(F32)
        kvar = _kv_variants(kv[:, :LANES] * (HEAD_DIM ** -0.5 * LOG2E), False)
        vvar = _kv_variants(kv[:, LANES:], True)
        for g in range(ATT_KV_HEADS):
            q2 = _pair_rows(lambda p: ua_ref[b, :, p * LANES:(p + 1) * LANES], g)
            z2 = _pair_rows(lambda p: ua_ref[b, :, ATT_W + p * LANES:ATT_W + (p + 1) * LANES], g)
            o = _attend_group(q2, [[kvar[g][0]], [kvar[g][1]]], [[vvar[g][0]], [vvar[g][1]]], [None],
                              sink_ref, layer, g, z2.astype(F32))
            yc_ref[b, :, 2 * g * LANES:(2 * g + 1) * LANES] = o[:t]
            yc_ref[b, :, (2 * g + 1) * LANES:(2 * g + 2) * LANES] = o[t:]


def _ctx_attention(ua, layer, sink):
    b, t, w = ua.shape
    nb = NB_CTX_ATTN
    return pl.pallas_call(
        functools.partial(_ctx_attn_kernel, nb=nb, t=t, layer=layer),
        grid=(b // nb,),
        in_specs=[pl.BlockSpec(memory_space=pltpu.SMEM),
                  pl.BlockSpec((nb, t, w), lambda i: (i, 0, 0))],
        out_specs=pl.BlockSpec((nb, t, ATT_W), lambda i: (i, 0, 0)),
        out_shape=jax.ShapeDtypeStruct((b, t, ATT_W), BF16),
        compiler_params=_params(("parallel",)),
        name="ctx_attention",
    )(sink, ua)


def _lat_attn_kernel(sink_ref, qz_ref, kv_ref, ck_ref, cv_ref, cos_ref, sin_ref, yc_ref,
                     kl_ref, vl_ref, kc_ref, vc_ref, *, t, tq, layer):
    j = pl.program_id(1)
    scale = HEAD_DIM ** -0.5 * LOG2E

    @pl.when(j == 0)
    def _prepare():
        kv = kv_ref[...].astype(F32)
        kvar = _kv_variants(_rope(kv[:, :LANES], cos_ref[...], sin_ref[...]) * scale, False)
        vvar = _kv_variants(kv[:, LANES:], True)
        cvar = _kv_variants(ck_ref[...] * scale, False)
        dvar = _kv_variants(cv_ref[...], True)
        for g in range(ATT_KV_HEADS):
            for half in range(2):
                i = 2 * g + half
                for ref, var in ((kl_ref, kvar), (vl_ref, vvar)):
                    pad = jnp.zeros((WINDOW, ref.shape[-1]), BF16)
                    ref[i, 0:WINDOW, :] = pad
                    ref[i, WINDOW:WINDOW + t, :] = var[g][half]
                    ref[i, WINDOW + t:, :] = pad
                kc_ref[i] = cvar[g][half]
                vc_ref[i] = dvar[g][half]

    r0 = pl.multiple_of(j * tq, tq)
    nloc = tq + 2 * WINDOW
    rr = lax.broadcasted_iota(jnp.int32, (2 * tq, nloc), 0) & (tq - 1)
    ss = lax.broadcasted_iota(jnp.int32, (2 * tq, nloc), 1)
    band = ((ss - rr >= 0) & (ss - rr <= 2 * WINDOW)
            & (ss >= WINDOW - j * tq) & (ss < t + WINDOW - j * tq))
    cs = cos_ref[pl.ds(r0, tq), :]
    sn = sin_ref[pl.ds(r0, tq), :]
    for g in range(ATT_KV_HEADS):
        q2 = _pair_rows(
            lambda p: _rope(qz_ref[:, p * LANES:(p + 1) * LANES].astype(F32), cs, sn).astype(BF16), g)
        z2 = _pair_rows(lambda p: qz_ref[:, ATT_W + p * LANES:ATT_W + (p + 1) * LANES], g)
        keys = [[kl_ref[2 * g + half, pl.ds(r0, nloc), :], kc_ref[2 * g + half]] for half in range(2)]
        vals = [[vl_ref[2 * g + half, pl.ds(r0, nloc), :], vc_ref[2 * g + half]] for half in range(2)]
        o = _attend_group(q2, keys, vals, [band, None], sink_ref, layer, g, z2.astype(F32))
        yc_ref[:, 2 * g * LANES:(2 * g + 1) * LANES] = o[:tq]
        yc_ref[:, (2 * g + 1) * LANES:(2 * g + 2) * LANES] = o[tq:]


def _lat_attention(ua, cache_k, cache_v, layer, sink, rope_tabs):
    b, t, _ = ua.shape
    past = cache_k.shape[2]
    tq = TQ_LATENT
    nvar = 2 * ATT_KV_HEADS
    cache_spec = pl.BlockSpec((None, None, past, ATT_KV_W), lambda i, j: (i, layer, 0, 0))
    return pl.pallas_call(
        functools.partial(_lat_attn_kernel, t=t, tq=tq, layer=layer),
        grid=(b, t // tq),
        in_specs=[pl.BlockSpec(memory_space=pltpu.SMEM),
                  pl.BlockSpec((None, tq, 2 * ATT_W), lambda i, j: (i, j, 0)),
                  pl.BlockSpec((None, t, 2 * ATT_KV_W), lambda i, j: (i, 0, 2 * ATT_W // (2 * ATT_KV_W))),
                  cache_spec, cache_spec,
                  _const_spec((t, LANES)), _const_spec((t, LANES))],
        out_specs=pl.BlockSpec((None, tq, ATT_W), lambda i, j: (i, j, 0)),
        out_shape=jax.ShapeDtypeStruct((b, t, ATT_W), BF16),
        scratch_shapes=[pltpu.VMEM((nvar, t + 2 * WINDOW, LANES), BF16),
                        pltpu.VMEM((nvar, t + 2 * WINDOW, 2 * LANES), BF16),
                        pltpu.VMEM((nvar, past, LANES), BF16),
                        pltpu.VMEM((nvar, past, 2 * LANES), BF16)],
        compiler_params=_params(("parallel", "arbitrary")),
        name="lat_attention",
    )(sink, ua, ua, cache_k, cache_v, *rope_tabs)


def _outproj_kernel(x_ref, ya_ref, yb_ref, yc_ref, mod_ref, gpre_ref, gpost_ref,
                    wg_ref, wgt_ref, wa_ref, wb_ref, wc_ref, wo_ref, o_ref, *, layer, tm, row0, rows_per_mod):
    for r in range(tm // SUB_ROWS):
        rows = slice(r * SUB_ROWS, (r + 1) * SUB_ROWS)
        x = x_ref[rows, :]
        h, row = _modulated_norm(x, gpre_ref[layer:layer + 1, :], mod_ref, tm, row0, rows_per_mod)

        def gate_logits(c):
            lo, hi = c * D_MODEL, min((c + 1) * D_MODEL, C_MG)
            mg = jnp.dot(h, wg_ref[:, lo:hi].astype(BF16), preferred_element_type=F32)
            if hi - lo < D_MODEL:
                tail = jnp.dot(h, wgt_ref[...].astype(BF16), preferred_element_type=F32)
                mg = jnp.concatenate([mg, tail], axis=1)
            return mg

        merged = None
        for c, (y_ref, w_ref) in enumerate(((ya_ref, wa_ref), (yb_ref, wb_ref), (yc_ref, wc_ref))):
            term = _sigmoid(gate_logits(c)) * jnp.dot(y_ref[rows, :], w_ref[...].astype(BF16),
                                                      preferred_element_type=F32)
            merged = term if merged is None else merged + term
        out = jnp.dot(merged.astype(BF16), wo_ref[...].astype(BF16), preferred_element_type=F32)
        ms = jnp.mean(out * out, axis=-1, keepdims=True)
        normed = out * lax.rsqrt(ms + EPS) * gpost_ref[layer:layer + 1, :]
        gate = mod_ref[pl.ds(row, 1), 2 * D_MODEL:3 * D_MODEL]
        o_ref[rows, :] = x + gate * normed


def _outproj(x2, ya, yb, yc, layer, mod, g_pre, g_post, w_in, wa, wb, wc, wo, *, row0, rows_per_mod):
    n = x2.shape[0]
    tm = TM_OUTPROJ
    row_spec = lambda w: pl.BlockSpec((tm, w), lambda i: (i, 0))
    return pl.pallas_call(
        functools.partial(_outproj_kernel, layer=layer, tm=tm, row0=row0, rows_per_mod=rows_per_mod),
        grid=(n // tm,),
        in_specs=[row_spec(D_MODEL), row_spec(FOURIER_W), row_spec(RET_W), row_spec(ATT_W),
                  _layer_spec((MOD_ROWS, 3 * D_MODEL), layer),
                  _const_spec((DEPTH, D_MODEL)), _const_spec((DEPTH, D_MODEL)),
                  _layer_spec((D_MODEL, C_MG), layer, col_block=1),
                  _layer_spec((D_MODEL, C_END - 2 * C_MG), layer, col_block=2 * C_MG // (C_END - 2 * C_MG)),
                  _layer_spec((FOURIER_W, D_MODEL), layer), _layer_spec((RET_W, D_MODEL), layer),
                  _layer_spec((ATT_W, D_MODEL), layer), _layer_spec((D_MODEL, D_MODEL), layer)],
        out_specs=row_spec(D_MODEL),
        out_shape=jax.ShapeDtypeStruct((n, D_MODEL), F32),
        compiler_params=_params(("parallel",)),
        name="outproj",
    )(x2, ya, yb, yc, mod, g_pre, g_post, w_in, w_in, wa, wb, wc, wo)


def _layer(x, layer, mod, p, *, latent, dft_chan, dft_pos, rope_tabs=None, cache_k=None, cache_v=None,
           state=None):
    b, t, _ = x.shape
    n = b * t
    row0, rows_per_mod = (1, t) if latent else (0, n)
    x2 = x.reshape(n, D_MODEL)
    xcs, fz, ur, ua, *kv32 = _inproj(x2, layer, mod, p["g_pre"], p["w_in"], dft_chan,
                                     row0=row0, rows_per_mod=rows_per_mod, emit_kv=not latent)
    ya = _fourier(xcs.reshape(b, t, -1), fz.reshape(b, t, -1), dft_pos, p["w_four"], layer,
                  bg=b if latent else BG_CTX_FOURIER, tq=min(t, TQ_FOURIER))
    ur3 = ur.reshape(b, t, -1)
    ua3 = ua.reshape(b, t, -1)
    if latent:
        yb, s_fin = _retention(ur3, layer, p["dec"], p["gn"], nb=1, rope_tabs=rope_tabs, state=state)
        yc = _lat_attention(ua3, cache_k, cache_v, layer, p["sink"], rope_tabs)
    else:
        yb, s_fin = _retention(ur3, layer, p["dec"], p["gn"], nb=NB_CTX_RET)
        yc = _ctx_attention(ua3, layer, p["sink"])
    out = _outproj(x2, ya.reshape(n, -1), yb.reshape(n, -1), yc.reshape(n, -1), layer, mod,
                   p["g_pre"], p["g_post"], p["w_in"], p["w_pa"], p["w_pb"], p["w_pc"], p["w_out"],
                   row0=row0, rows_per_mod=rows_per_mod)
    return out.reshape(b, t, D_MODEL), kv32, s_fin


def kernel(x_prompt, x_sample, cache_k, cache_v, state_ret, c, c_ctx, w_mod, b_mod, g_pre, g_post, w_in,
           w_four, ret_decay, ret_gn, attn_sink, w_branch_a, w_branch_b, w_branch_c, w_out):
    batch, seq, _ = x_prompt.shape
    dec_batch, dec_seq, _ = x_sample.shape
    past = cache_k.shape[2]
    assert 1 + dec_batch <= MOD_ROWS

    cv = jnp.zeros((MOD_ROWS, D_MODEL), F32).at[0].set(c_ctx).at[1:1 + dec_batch].set(c)
    mod = _modulation(cv, w_mod, b_mod)

    dft_chan, dft_ctx = _dft_tables(seq)
    _, dft_lat = _dft_tables(dec_seq)
    rope_tabs = _rope_tables(dec_seq)
    ck = cache_k.reshape(dec_batch, DEPTH, past, ATT_KV_W)
    cvv = cache_v.reshape(dec_batch, DEPTH, past, ATT_KV_W)

    p = dict(
        g_pre=g_pre, g_post=g_post,
        w_in=w_in, w_four=w_four, w_pa=w_branch_a, w_pb=w_branch_b, w_pc=w_branch_c, w_out=w_out,
        dec=ret_decay, gn=ret_gn, sink=attn_sink)

    xp = x_prompt
    ks, vs, ss = [], [], []
    for l in range(DEPTH):
        xp, (k32, v32), s_fin = _layer(xp, l, mod, p, latent=False, dft_chan=dft_chan, dft_pos=dft_ctx)
        ks.append(k32.reshape(batch, seq, ATT_KV_HEADS, HEAD_DIM))
        vs.append(v32.reshape(batch, seq, ATT_KV_HEADS, HEAD_DIM))
        ss.append(s_fin)

    xs = x_sample
    for l in range(DEPTH):
        xs, _, _ = _layer(xs, l, mod, p, latent=True, dft_chan=dft_chan, dft_pos=dft_lat,
                             rope_tabs=rope_tabs, cache_k=ck, cache_v=cvv, state=state_ret)

    return (xp, xs, jnp.stack(ks, axis=1), jnp.stack(vs, axis=1), jnp.stack(ss, axis=1))
```

```python
import functools
import math

import numpy as np
import jax
import jax.numpy as jnp
from jax import lax
from jax.experimental import pallas as pl
from jax.experimental.pallas import tpu as pltpu

F32 = jnp.float32
BF16 = jnp.bfloat16

D_MODEL = 1024
DEPTH = 2
GRID_W = 64
HEAD_DIM = 64
FOURIER_GROUPS = 4
FOURIER_GROUP_W = 64
FOURIER_W = FOURIER_GROUPS * FOURIER_GROUP_W
RET_HEADS = 4
RET_DK = 64
RET_W = RET_HEADS * RET_DK
RET_CHUNK = 128
ATT_Q_HEADS = 8
ATT_KV_HEADS = 2
ATT_W = ATT_Q_HEADS * HEAD_DIM
ATT_KV_W = ATT_KV_HEADS * HEAD_DIM
WINDOW = 128
ROPE_BASE = 10000.0
EPS = 1e-6
MOD_ROWS = 8
LANES = 128
NEG = -1e30
LOG2E = math.log2(math.e)
VMEM_LIMIT = 56 * 1024 * 1024

C_FX, C_RQ, C_RZ_END = 0, 512, 1536
C_AQ, C_AK, C_AV, C_AZ, C_MG, C_END = 1536, 2048, 2176, 2304, 2816, 5888

TM_INPROJ = 1024
TM_OUTPROJ = 1024
SUB_ROWS = 512
TQ_LATENT = 256
TQ_FOURIER = 512
NB_CTX_RET = 8
NB_CTX_ATTN = 4
BG_CTX_FOURIER = 8
RET_UNROLL = 4


def _sigmoid(x):
    return 0.5 * jnp.tanh(0.5 * x) + 0.5


def _silu(x):
    return x * _sigmoid(x)


def _params(sem):
    return pltpu.CompilerParams(dimension_semantics=sem, vmem_limit_bytes=VMEM_LIMIT)


def _const_spec(shape):
    nd = len(shape)
    return pl.BlockSpec(shape, lambda *_: (0,) * nd, pipeline_mode=pl.Buffered(1))


def _layer_spec(shape, layer, col_block=0):
    idx = (layer,) + (0,) * (len(shape) - 1) + (col_block,)
    return pl.BlockSpec((None,) + tuple(shape), lambda *_: idx, pipeline_mode=pl.Buffered(1))


def _carry_args(carry, first_input, first_output):
    specs = [pl.BlockSpec(memory_space=pl.ANY)] * len(carry)
    aliases = {first_input + k: first_output + k for k in range(len(carry))}
    return specs, list(carry), aliases


def _dft_tables(t):
    c = np.arange(FOURIER_GROUP_W)
    ang = 2.0 * np.pi * ((c[:, None] * c[None, :]) % FOURIER_GROUP_W) / FOURIER_GROUP_W
    eye = np.eye(FOURIER_GROUPS)
    s64 = FOURIER_GROUP_W ** -0.5
    chan = np.concatenate([np.kron(eye, np.cos(ang) * s64), np.kron(eye, np.sin(ang) * s64)], axis=1)
    p = np.arange(t)
    angt = 2.0 * np.pi * ((p[:, None] * p[None, :]) % t) / t
    pos = np.concatenate([np.cos(angt), -np.sin(angt)], axis=1) * (t ** -0.5)
    return jnp.asarray(chan, F32).astype(BF16), jnp.asarray(pos, F32).astype(BF16)


def _rope_tables(t):
    quarter = HEAD_DIM // 4
    lane = np.arange(LANES) % HEAD_DIM
    inv = ROPE_BASE ** (-(lane % quarter).astype(np.float64) / quarter)
    n = np.arange(t)
    pos = np.where(lane[None, :] < HEAD_DIM // 2, (n // GRID_W)[:, None], (n % GRID_W)[:, None])
    ang = pos.astype(np.float64) * inv[None, :]
    sign = np.where((lane % (2 * quarter)) < quarter, -1.0, 1.0)
    return jnp.asarray(np.cos(ang), F32), jnp.asarray(np.sin(ang) * sign[None, :], F32)


def _rope(x, cos, sin):
    lane = lax.broadcasted_iota(jnp.int32, x.shape, 1)
    first = (lane & 31) < 16
    partner = jnp.where(first, pltpu.roll(x, LANES - 16, axis=1), pltpu.roll(x, 16, axis=1))
    return x * cos + partner * sin


def _split_bf16(x):
    hi = x.astype(BF16)
    return hi, (x - hi.astype(F32)).astype(BF16)


def _mod_kernel(cv_ref, w_ref, b_ref, o_ref):
    a_hi, a_lo = _split_bf16(_silu(cv_ref[...]))
    w_hi, w_lo = _split_bf16(w_ref[...])
    dot = functools.partial(jnp.dot, preferred_element_type=F32)
    bias = b_ref[pl.ds(pl.program_id(0), 1), :]
    o_ref[...] = dot(a_hi, w_hi) + (dot(a_lo, w_hi) + dot(a_hi, w_lo)) + bias


def _modulation(cv, w_mod, b_mod):
    tn = 1024
    return pl.pallas_call(
        _mod_kernel,
        grid=(DEPTH, 3 * D_MODEL // tn),
        in_specs=[pl.BlockSpec((MOD_ROWS, D_MODEL), lambda l, j: (0, 0)),
                  pl.BlockSpec((None, D_MODEL, tn), lambda l, j: (l, 0, j)),
                  pl.BlockSpec((DEPTH, tn), lambda l, j: (0, j))],
        out_specs=pl.BlockSpec((None, MOD_ROWS, tn), lambda l, j: (l, 0, j)),
        out_shape=jax.ShapeDtypeStruct((DEPTH, MOD_ROWS, 3 * D_MODEL), F32),
        compiler_params=_params(("parallel", "parallel")),
        name="modulation",
    )(cv, w_mod, b_mod)


def _modulated_norm(x, g, mod_ref, tm, row0, rows_per_mod):
    row = row0 + (pl.program_id(0) * tm) // rows_per_mod
    ms = jnp.mean(x * x, axis=-1, keepdims=True)
    y = x * lax.rsqrt(ms + EPS) * g
    shift = mod_ref[pl.ds(row, 1), 0:D_MODEL]
    scale = mod_ref[pl.ds(row, 1), D_MODEL:2 * D_MODEL]
    return (y * (1.0 + scale) + shift).astype(BF16), row


def _inproj_kernel(x_ref, mod_ref, g_ref, w_ref, dft_ref, *refs, layer, tm, row0, rows_per_mod, n_carry):
    xcs_ref, fz_ref, ur_ref, ua_ref, *kv_refs = refs[n_carry:]
    for r in range(tm // SUB_ROWS):
        rows = slice(r * SUB_ROWS, (r + 1) * SUB_ROWS)
        h, _ = _modulated_norm(x_ref[rows, :], g_ref[layer:layer + 1, :], mod_ref, tm, row0, rows_per_mod)

        def mm(c0, c1):
            return jnp.dot(h, w_ref[:, c0:c1].astype(BF16), preferred_element_type=F32)

        f = mm(C_FX, C_RQ)
        xcs_ref[rows, :] = jnp.dot(f[:, :FOURIER_W].astype(BF16), dft_ref[...],
                                   preferred_element_type=F32).astype(BF16)
        fz_ref[rows, :] = f[:, FOURIER_W:].astype(BF16)
        ur_ref[rows, :] = mm(C_RQ, C_RZ_END).astype(BF16)
        ua_ref[rows, 0:ATT_W] = mm(C_AQ, C_AK).astype(BF16)
        ua_ref[rows, ATT_W:2 * ATT_W] = mm(C_AZ, C_MG).astype(BF16)
        kv = mm(C_AK, C_AZ)
        if kv_refs:
            kv_refs[0][rows, :] = kv[:, :ATT_KV_W]
            kv_refs[1][rows, :] = kv[:, ATT_KV_W:]
        ua_ref[rows, 2 * ATT_W:] = kv.astype(BF16)


def _inproj(x2, layer, mod, g_pre, w_in, dft, *, row0, rows_per_mod, emit_kv, kv_carry=()):
    n = x2.shape[0]
    tm = TM_INPROJ
    row_spec = lambda w: pl.BlockSpec((tm, w), lambda i: (i, 0))
    widths = (2 * FOURIER_W, FOURIER_W, 4 * RET_W, 2 * ATT_W + 2 * ATT_KV_W)
    out_shape = [jax.ShapeDtypeStruct((n, w), BF16) for w in widths]
    out_specs = [row_spec(w) for w in widths]
    if emit_kv:
        out_shape += [jax.ShapeDtypeStruct((DEPTH, n, ATT_KV_W), F32)] * 2
        out_specs += [pl.BlockSpec((None, tm, ATT_KV_W), lambda i: (layer, i, 0))] * 2
    args = [x2, mod, g_pre, w_in, dft]
    carry_specs, carry, aliases = _carry_args(kv_carry, len(args), len(widths))
    return pl.pallas_call(
        functools.partial(_inproj_kernel, layer=layer, tm=tm, row0=row0, rows_per_mod=rows_per_mod,
                          n_carry=len(carry)),
        grid=(n // tm,),
        in_specs=[row_spec(D_MODEL),
                  _layer_spec((MOD_ROWS, 3 * D_MODEL), layer),
                  _const_spec((DEPTH, D_MODEL)),
                  _layer_spec((D_MODEL, C_MG), layer),
                  _const_spec((FOURIER_W, 2 * FOURIER_W))] + carry_specs,
        out_specs=out_specs,
        out_shape=out_shape,
        input_output_aliases=aliases,
        compiler_params=_params(("parallel",)),
        name="inproj",
    )(*args, *carry)


def _fourier_kernel(ct_ref, xcs_ref, fz_ref, w_ref, ya_ref, xcat_ref, *, bg, t):
    @pl.when(pl.program_id(1) == 0)
    def _gather():
        for b in range(bg):
            cols = slice(b * FOURIER_W, (b + 1) * FOURIER_W)
            xcat_ref[0:t, cols] = xcs_ref[b, :, 0:FOURIER_W]
            xcat_ref[t:2 * t, cols] = xcs_ref[b, :, FOURIER_W:]

    yr = jnp.dot(ct_ref[...], xcat_ref[...], preferred_element_type=F32).astype(BF16)
    w = w_ref[...].astype(BF16)
    for b in range(bg):
        ya = jnp.dot(yr[:, b * FOURIER_W:(b + 1) * FOURIER_W], w, preferred_element_type=F32)
        ya_ref[b] = (ya * _silu(fz_ref[b].astype(F32))).astype(BF16)


def _fourier(xcs, fz, ct, w_four, layer, *, bg, tq):
    b, t, _ = xcs.shape
    return pl.pallas_call(
        functools.partial(_fourier_kernel, bg=bg, t=t),
        grid=(b // bg, t // tq),
        in_specs=[pl.BlockSpec((tq, 2 * t), lambda i, j: (j, 0)),
                  pl.BlockSpec((bg, t, 2 * FOURIER_W), lambda i, j: (i, 0, 0)),
                  pl.BlockSpec((bg, tq, FOURIER_W), lambda i, j: (i, j, 0)),
                  _layer_spec((FOURIER_W, FOURIER_W), layer)],
        out_specs=pl.BlockSpec((bg, tq, FOURIER_W), lambda i, j: (i, j, 0)),
        out_shape=jax.ShapeDtypeStruct((b, t, FOURIER_W), BF16),
        scratch_shapes=[pltpu.VMEM((2 * t, bg * FOURIER_W), BF16)],
        compiler_params=_params(("parallel", "arbitrary")),
        name="fourier",
    )(ct, xcs, fz, w_four)


def _log_sigmoid(x):
    return jnp.minimum(x, 0.0) - jnp.log(1.0 + jnp.exp(-jnp.abs(x)))


def _head_blocks(a, width):
    lane = lax.broadcasted_iota(jnp.int32, a.shape, 1)
    zero = jnp.zeros_like(a)
    return jnp.concatenate(
        [jnp.where((lane >= h * width) & (lane < (h + 1) * width), a, zero) for h in range(RET_HEADS)],
        axis=0)


def _per_head(dec_ref, layer, direction, head_of):
    out = jnp.full(head_of.shape, dec_ref[layer, direction, RET_HEADS - 1], F32)
    for h in range(RET_HEADS - 2, -1, -1):
        out = jnp.where(head_of == h, dec_ref[layer, direction, h], out)
    return out


def _ret_kernel(*refs, nb, t, layer, rope, has_s0, n_carry):
    refs = list(refs)
    dec_ref = refs.pop(0)
    ur_ref = refs.pop(0)
    cos_ref = refs.pop(0) if rope else None
    sin_ref = refs.pop(0) if rope else None
    s0_ref = refs.pop(0) if has_s0 else None
    gn_ref = refs.pop(0)
    del refs[:n_carry]
    (yb_ref, sfin_ref, kr_ref, dsf_ref, dsb_ref, sfs_ref, sbs_ref, st_ref) = refs
    c = RET_CHUNK
    nc = t // c
    nch = nb * nc
    group = RET_UNROLL

    lane_head = lax.broadcasted_iota(jnp.int32, (1, RET_W), 1) // RET_DK
    lgf = _log_sigmoid(_per_head(dec_ref, layer, 0, lane_head))
    lgb = _log_sigmoid(_per_head(dec_ref, layer, 1, lane_head))
    ri = lax.broadcasted_iota(jnp.int32, (c, RET_W), 0).astype(F32)
    read_f = jnp.exp((ri + 1.0) * lgf)
    read_b = jnp.exp((c - ri) * lgb)
    write_f = jnp.exp((c - 1.0 - ri) * lgf)
    write_b = jnp.exp(ri * lgb)
    carry_f = jnp.exp(c * lgf)
    carry_b = jnp.exp(c * lgb)
    ii = lax.broadcasted_iota(jnp.int32, (RET_HEADS * c, c), 0)
    jj = lax.broadcasted_iota(jnp.int32, (RET_HEADS * c, c), 1)
    row_head = ii // c
    diff = ((ii & (c - 1)) - jj).astype(F32)
    decay = (jnp.where(diff >= 0, jnp.exp(jnp.maximum(diff, 0.0)
                                          * _log_sigmoid(_per_head(dec_ref, layer, 0, row_head))), 0.0)
             + jnp.where(diff <= 0, jnp.exp(jnp.maximum(-diff, 0.0)
                                            * _log_sigmoid(_per_head(dec_ref, layer, 1, row_head))), 0.0))
    r2 = lax.broadcasted_iota(jnp.int32, (RET_W, RET_W), 0)
    c2 = lax.broadcasted_iota(jnp.int32, (RET_W, RET_W), 1)
    same_head = (r2 // RET_DK) == (c2 // RET_DK)
    group_mean = jnp.where(same_head, 1.0 / RET_DK, 0.0).astype(BF16)
    out_head = lax.broadcasted_iota(jnp.int32, (c, RET_W), 1) // RET_DK

    def chunk_pos(ci):
        return ci // nc, pl.multiple_of((ci % nc) * c, c)

    def load_qk(col, b, r0):
        a = ur_ref[b, pl.ds(r0, c), col:col + RET_W].astype(F32)
        if rope:
            cs = cos_ref[pl.ds(r0, c), :]
            sn = sin_ref[pl.ds(r0, c), :]
            a = jnp.concatenate([_rope(a[:, :LANES], cs, sn), _rope(a[:, LANES:], cs, sn)], axis=1)
        return a

    def increments(ci, carry):
        b, r0 = chunk_pos(ci)
        k = load_qk(RET_W, b, r0) * (RET_DK ** -0.5)
        v = ur_ref[b, pl.ds(r0, c), 2 * RET_W:3 * RET_W]
        kr_ref[pl.ds(pl.multiple_of(ci * c, c), c), :] = k.astype(BF16)
        kw = jnp.concatenate([(k * write_f).astype(BF16), (k * write_b).astype(BF16)], axis=1)
        d = lax.dot_general(kw, v, (((0,), (0,)), ((), ())), preferred_element_type=F32)
        dsf_ref[ci] = jnp.where(same_head, d[:RET_W], 0.0)
        dsb_ref[ci] = jnp.where(same_head, d[RET_W:], 0.0)
        return carry

    lax.fori_loop(0, nch, increments, 0, unroll=group)

    def scan(b, direction, ds_ref, out_ref, carry_decay):
        st_ref[...] = jnp.zeros((RET_W, RET_W), F32)
        if has_s0:
            for h in range(RET_HEADS):
                sl = slice(h * RET_DK, (h + 1) * RET_DK)
                st_ref[sl, sl] = s0_ref[b, direction, h]

        def step(n, carry):
            ci = b * nc + (n if direction == 0 else nc - 1 - n)
            out_ref[ci] = st_ref[...].astype(BF16)
            st_ref[...] = carry_decay * st_ref[...] + ds_ref[ci]
            return carry

        lax.fori_loop(0, nc, step, 0, unroll=min(nc, 4))
        for h in range(RET_HEADS):
            sl = slice(h * RET_DK, (h + 1) * RET_DK)
            sfin_ref[b, direction, h] = st_ref[sl, sl]

    for b in range(nb):
        scan(b, 0, dsf_ref, sfs_ref, carry_f)
        scan(b, 1, dsb_ref, sbs_ref, carry_b)

    def chunk_output(ci, b, r0):
        q = load_qk(0, b, r0)
        kb = kr_ref[pl.ds(pl.multiple_of(ci * c, c), c), :]
        v = ur_ref[b, pl.ds(r0, c), 2 * RET_W:3 * RET_W]
        att = lax.dot_general(_head_blocks(q.astype(BF16), RET_DK), kb,
                              (((1,), (1,)), ((), ())), preferred_element_type=F32)
        o4 = jnp.dot((att * decay).astype(BF16), v, preferred_element_type=F32)
        o = o4[(RET_HEADS - 1) * c:]
        for h in range(RET_HEADS - 2, -1, -1):
            o = jnp.where(out_head == h, o4[h * c:(h + 1) * c], o)
        o = o + jnp.dot((q * read_f).astype(BF16), sfs_ref[ci], preferred_element_type=F32)
        return o + jnp.dot((q * read_b).astype(BF16), sbs_ref[ci], preferred_element_type=F32)

    def outputs(it, carry):
        pos = [chunk_pos(it * group + u) for u in range(group)]
        o = jnp.concatenate([chunk_output(it * group + u, *pos[u]) for u in range(group)], axis=0)
        hi, lo = _split_bf16(o * o)
        ms = (jnp.dot(hi, group_mean, preferred_element_type=F32)
              + jnp.dot(lo, group_mean, preferred_element_type=F32))
        y = o * lax.rsqrt(ms + EPS) * gn_ref[layer:layer + 1, :]
        for u, (b, r0) in enumerate(pos):
            z = ur_ref[b, pl.ds(r0, c), 3 * RET_W:4 * RET_W].astype(F32)
            yb_ref[b, pl.ds(r0, c), :] = (y[u * c:(u + 1) * c] * _silu(z)).astype(BF16)
        return carry

    lax.fori_loop(0, nch // group, outputs, 0)


def _retention(ur, layer, dec, gn, *, nb, rope_tabs=None, state=None, state_carry=()):
    b, t, _ = ur.shape
    nch = nb * (t // RET_CHUNK)
    assert nch % RET_UNROLL == 0
    rope = rope_tabs is not None
    has_s0 = state is not None
    in_specs = [pl.BlockSpec(memory_space=pltpu.SMEM),
                pl.BlockSpec((nb, t, 4 * RET_W), lambda i: (i, 0, 0))]
    args = [dec, ur]
    if rope:
        in_specs += [_const_spec((t, LANES)), _const_spec((t, LANES))]
        args += list(rope_tabs)
    if has_s0:
        in_specs.append(pl.BlockSpec((nb, None, 2, RET_HEADS, RET_DK, RET_DK),
                                     lambda i: (i, layer, 0, 0, 0, 0)))
        args.append(state)
    in_specs.append(_const_spec((DEPTH, RET_W)))
    args.append(gn)
    carry_specs, carry, aliases = _carry_args(state_carry, len(args), 1)
    state_scratch = lambda dt: pltpu.VMEM((nch, RET_W, RET_W), dt)
    return pl.pallas_call(
        functools.partial(_ret_kernel, nb=nb, t=t, layer=layer, rope=rope, has_s0=has_s0,
                          n_carry=len(carry)),
        grid=(b // nb,),
        in_specs=in_specs + carry_specs,
        out_specs=[pl.BlockSpec((nb, t, RET_W), lambda i: (i, 0, 0)),
                   pl.BlockSpec((None, nb, 2, RET_HEADS, RET_DK, RET_DK), lambda i: (layer, i, 0, 0, 0, 0))],
        out_shape=[jax.ShapeDtypeStruct((b, t, RET_W), BF16),
                   jax.ShapeDtypeStruct((DEPTH, b, 2, RET_HEADS, RET_DK, RET_DK), F32)],
        input_output_aliases=aliases,
        scratch_shapes=[pltpu.VMEM((nb * t, RET_W), BF16),
                        state_scratch(F32), state_scratch(F32), state_scratch(BF16), state_scratch(BF16),
                        pltpu.VMEM((RET_W, RET_W), F32)],
        compiler_params=_params(("parallel",)),
        name="retention",
    )(*args, *carry)


def _kv_variants(a, ones_block):
    lane_half = lax.broadcasted_iota(jnp.int32, a.shape, 1) // HEAD_DIM
    swapped = pltpu.roll(a, HEAD_DIM, axis=1)
    out = []
    for g in range(ATT_KV_HEADS):
        row = []
        for half in range(2):
            var = jnp.where(lane_half == half, a if half == g else swapped, 0.0).astype(BF16)
            if ones_block:
                ones = jnp.where(lane_half == half, 1.0, 0.0).astype(BF16)
                var = jnp.concatenate([var, ones], axis=1)
            row.append(var)
        out.append(row)
    return out


def _attend_group(q2, keys, values, masks, sink_ref, layer, g, z2):
    rows = q2.shape[0]
    upper = lax.broadcasted_iota(jnp.int32, (rows, 1), 0) < rows // 2
    acc = None
    sink_terms = []
    for half in range(2):
        sink = jnp.where(upper, sink_ref[layer, g, half], sink_ref[layer, g, 2 + half]) * LOG2E
        logits = []
        for kpart, mask in zip(keys[half], masks):
            l = lax.dot_general(q2, kpart, (((1,), (1,)), ((), ())), preferred_element_type=F32)
            logits.append(l if mask is None else jnp.where(mask, l, NEG))
        m = sink
        for l in logits:
            m = jnp.maximum(m, jnp.max(l, axis=-1, keepdims=True))
        for l, vpart in zip(logits, values[half]):
            pv = jnp.dot(jnp.exp2(l - m).astype(BF16), vpart, preferred_element_type=F32)
            acc = pv if acc is None else acc + pv
        sink_terms.append(jnp.exp2(sink - m))
    lane = lax.broadcasted_iota(jnp.int32, (rows, LANES), 1)
    den = acc[:, LANES:] + jnp.where(lane < HEAD_DIM, sink_terms[0], sink_terms[1])
    return (acc[:, :LANES] * (1.0 / den) * _silu(z2)).astype(BF16)


def _pair_rows(ref_slice, g):
    return jnp.concatenate([ref_slice(2 * g), ref_slice(2 * g + 1)], axis=0)


def _ctx_attn_kernel(sink_ref, ua_ref, yc_ref, *, nb, t, layer):
    for b in range(nb):
        kv = ua_ref[b, :, 2 * ATT_W:].astype(F32)
        kvar = _kv_variants(kv[:, :LANES] * (HEAD_DIM ** -0.5 * LOG2E), False)
        vvar = _kv_variants(kv[:, LANES:], True)
        for g in range(ATT_KV_HEADS):
            q2 = _pair_rows(lambda p: ua_ref[b, :, p * LANES:(p + 1) * LANES], g)
            z2 = _pair_rows(lambda p: ua_ref[b, :, ATT_W + p * LANES:ATT_W + (p + 1) * LANES], g)
            o = _attend_group(q2, [[kvar[g][0]], [kvar[g][1]]], [[vvar[g][0]], [vvar[g][1]]], [None],
                              sink_ref, layer, g, z2.astype(F32))
            yc_ref[b, :, 2 * g * LANES:(2 * g + 1) * LANES] = o[:t]
            yc_ref[b, :, (2 * g + 1) * LANES:(2 * g + 2) * LANES] = o[t:]


def _ctx_attention(ua, layer, sink):
    b, t, w = ua.shape
    nb = NB_CTX_ATTN
    return pl.pallas_call(
        functools.partial(_ctx_attn_kernel, nb=nb, t=t, layer=layer),
        grid=(b // nb,),
        in_specs=[pl.BlockSpec(memory_space=pltpu.SMEM),
                  pl.BlockSpec((nb, t, w), lambda i: (i, 0, 0))],
        out_specs=pl.BlockSpec((nb, t, ATT_W), lambda i: (i, 0, 0)),
        out_shape=jax.ShapeDtypeStruct((b, t, ATT_W), BF16),
        compiler_params=_params(("parallel",)),
        name="ctx_attention",
    )(sink, ua)


def _lat_attn_kernel(sink_ref, qz_ref, kv_ref, ck_ref, cv_ref, cos_ref, sin_ref, yc_ref,
                     kl_ref, vl_ref, kc_ref, vc_ref, *, t, tq, layer):
    j = pl.program_id(1)
    scale = HEAD_DIM ** -0.5 * LOG2E

    @pl.when(j == 0)
    def _prepare():
        kv = kv_ref[...].astype(F32)
        kvar = _kv_variants(_rope(kv[:, :LANES], cos_ref[...], sin_ref[...]) * scale, False)
        vvar = _kv_variants(kv[:, LANES:], True)
        cvar = _kv_variants(ck_ref[...] * scale, False)
        dvar = _kv_variants(cv_ref[...], True)
        for g in range(ATT_KV_HEADS):
            for half in range(2):
                i = 2 * g + half
                for ref, var in ((kl_ref, kvar), (vl_ref, vvar)):
                    pad = jnp.zeros((WINDOW, ref.shape[-1]), BF16)
                    ref[i, 0:WINDOW, :] = pad
                    ref[i, WINDOW:WINDOW + t, :] = var[g][half]
                    ref[i, WINDOW + t:, :] = pad
                kc_ref[i] = cvar[g][half]
                vc_ref[i] = dvar[g][half]

    r0 = pl.multiple_of(j * tq, tq)
    nloc = tq + 2 * WINDOW
    rr = lax.broadcasted_iota(jnp.int32, (2 * tq, nloc), 0) & (tq - 1)
    ss = lax.broadcasted_iota(jnp.int32, (2 * tq, nloc), 1)
    band = ((ss - rr >= 0) & (ss - rr <= 2 * WINDOW)
            & (ss >= WINDOW - j * tq) & (ss < t + WINDOW - j * tq))
    cs = cos_ref[pl.ds(r0, tq), :]
    sn = sin_ref[pl.ds(r0, tq), :]
    for g in range(ATT_KV_HEADS):
        q2 = _pair_rows(
            lambda p: _rope(qz_ref[:, p * LANES:(p + 1) * LANES].astype(F32), cs, sn).astype(BF16), g)
        z2 = _pair_rows(lambda p: qz_ref[:, ATT_W + p * LANES:ATT_W + (p + 1) * LANES], g)
        keys = [[kl_ref[2 * g + half, pl.ds(r0, nloc), :], kc_ref[2 * g + half]] for half in range(2)]
        vals = [[vl_ref[2 * g + half, pl.ds(r0, nloc), :], vc_ref[2 * g + half]] for half in range(2)]
        o = _attend_group(q2, keys, vals, [band, None], sink_ref, layer, g, z2.astype(F32))
        yc_ref[:, 2 * g * LANES:(2 * g + 1) * LANES] = o[:tq]
        yc_ref[:, (2 * g + 1) * LANES:(2 * g + 2) * LANES] = o[tq:]


def _lat_attention(ua, cache_k, cache_v, layer, sink, rope_tabs):
    b, t, _ = ua.shape
    past = cache_k.shape[2]
    tq = TQ_LATENT
    nvar = 2 * ATT_KV_HEADS
    cache_spec = pl.BlockSpec((None, None, past, ATT_KV_W), lambda i, j: (i, layer, 0, 0))
    return pl.pallas_call(
        functools.partial(_lat_attn_kernel, t=t, tq=tq, layer=layer),
        grid=(b, t // tq),
        in_specs=[pl.BlockSpec(memory_space=pltpu.SMEM),
                  pl.BlockSpec((None, tq, 2 * ATT_W), lambda i, j: (i, j, 0)),
                  pl.BlockSpec((None, t, 2 * ATT_KV_W), lambda i, j: (i, 0, 2 * ATT_W // (2 * ATT_KV_W))),
                  cache_spec, cache_spec,
                  _const_spec((t, LANES)), _const_spec((t, LANES))],
        out_specs=pl.BlockSpec((None, tq, ATT_W), lambda i, j: (i, j, 0)),
        out_shape=jax.ShapeDtypeStruct((b, t, ATT_W), BF16),
        scratch_shapes=[pltpu.VMEM((nvar, t + 2 * WINDOW, LANES), BF16),
                        pltpu.VMEM((nvar, t + 2 * WINDOW, 2 * LANES), BF16),
                        pltpu.VMEM((nvar, past, LANES), BF16),
                        pltpu.VMEM((nvar, past, 2 * LANES), BF16)],
        compiler_params=_params(("parallel", "arbitrary")),
        name="lat_attention",
    )(sink, ua, ua, cache_k, cache_v, *rope_tabs)


def _outproj_kernel(x_ref, ya_ref, yb_ref, yc_ref, mod_ref, gpre_ref, gpost_ref,
                    wg_ref, wgt_ref, wa_ref, wb_ref, wc_ref, wo_ref, o_ref, *, layer, tm, row0, rows_per_mod):
    for r in range(tm // SUB_ROWS):
        rows = slice(r * SUB_ROWS, (r + 1) * SUB_ROWS)
        x = x_ref[rows, :]
        h, row = _modulated_norm(x, gpre_ref[layer:layer + 1, :], mod_ref, tm, row0, rows_per_mod)

        def gate_logits(c):
            lo, hi = c * D_MODEL, min((c + 1) * D_MODEL, C_MG)
            mg = jnp.dot(h, wg_ref[:, lo:hi].astype(BF16), preferred_element_type=F32)
            if hi - lo < D_MODEL:
                tail = jnp.dot(h, wgt_ref[...].astype(BF16), preferred_element_type=F32)
                mg = jnp.concatenate([mg, tail], axis=1)
            return mg

        merged = None
        for c, (y_ref, w_ref) in enumerate(((ya_ref, wa_ref), (yb_ref, wb_ref), (yc_ref, wc_ref))):
            term = _sigmoid(gate_logits(c)) * jnp.dot(y_ref[rows, :], w_ref[...].astype(BF16),
                                                      preferred_element_type=F32)
            merged = term if merged is None else merged + term
        out = jnp.dot(merged.astype(BF16), wo_ref[...].astype(BF16), preferred_element_type=F32)
        ms = jnp.mean(out * out, axis=-1, keepdims=True)
        normed = out * lax.rsqrt(ms + EPS) * gpost_ref[layer:layer + 1, :]
        gate = mod_ref[pl.ds(row, 1), 2 * D_MODEL:3 * D_MODEL]
        o_ref[rows, :] = x + gate * normed


def _outproj(x2, ya, yb, yc, layer, mod, g_pre, g_post, w_in, wa, wb, wc, wo, *, row0, rows_per_mod):
    n = x2.shape[0]
    tm = TM_OUTPROJ
    row_spec = lambda w: pl.BlockSpec((tm, w), lambda i: (i, 0))
    return pl.pallas_call(
        functools.partial(_outproj_kernel, layer=layer, tm=tm, row0=row0, rows_per_mod=rows_per_mod),
        grid=(n // tm,),
        in_specs=[row_spec(D_MODEL), row_spec(FOURIER_W), row_spec(RET_W), row_spec(ATT_W),
                  _layer_spec((MOD_ROWS, 3 * D_MODEL), layer),
                  _const_spec((DEPTH, D_MODEL)), _const_spec((DEPTH, D_MODEL)),
                  _layer_spec((D_MODEL, C_MG), layer, col_block=1),
                  _layer_spec((D_MODEL, C_END - 2 * C_MG), layer, col_block=2 * C_MG // (C_END - 2 * C_MG)),
                  _layer_spec((FOURIER_W, D_MODEL), layer), _layer_spec((RET_W, D_MODEL), layer),
                  _layer_spec((ATT_W, D_MODEL), layer), _layer_spec((D_MODEL, D_MODEL), layer)],
        out_specs=row_spec(D_MODEL),
        out_shape=jax.ShapeDtypeStruct((n, D_MODEL), F32),
        compiler_params=_params(("parallel",)),
        name="outproj",
    )(x2, ya, yb, yc, mod, g_pre, g_post, w_in, w_in, wa, wb, wc, wo)


def _layer(x, layer, mod, p, *, latent, dft_chan, dft_pos, rope_tabs=None, cache_k=None, cache_v=None,
           state=None, kv_carry=(), state_carry=()):
    b, t, _ = x.shape
    n = b * t
    row0, rows_per_mod = (1, t) if latent else (0, n)
    x2 = x.reshape(n, D_MODEL)
    xcs, fz, ur, ua, *kv32 = _inproj(x2, layer, mod, p["g_pre"], p["w_in"], dft_chan,
                                     row0=row0, rows_per_mod=rows_per_mod, emit_kv=not latent,
                                     kv_carry=kv_carry)
    ya = _fourier(xcs.reshape(b, t, -1), fz.reshape(b, t, -1), dft_pos, p["w_four"], layer,
                  bg=b if latent else BG_CTX_FOURIER, tq=min(t, TQ_FOURIER))
    ur3 = ur.reshape(b, t, -1)
    ua3 = ua.reshape(b, t, -1)
    if latent:
        yb, s_fin = _retention(ur3, layer, p["dec"], p["gn"], nb=1, rope_tabs=rope_tabs, state=state)
        yc = _lat_attention(ua3, cache_k, cache_v, layer, p["sink"], rope_tabs)
    else:
        yb, s_fin = _retention(ur3, layer, p["dec"], p["gn"], nb=NB_CTX_RET, state_carry=state_carry)
        yc = _ctx_attention(ua3, layer, p["sink"])
    out = _outproj(x2, ya.reshape(n, -1), yb.reshape(n, -1), yc.reshape(n, -1), layer, mod,
                   p["g_pre"], p["g_post"], p["w_in"], p["w_pa"], p["w_pb"], p["w_pc"], p["w_out"],
                   row0=row0, rows_per_mod=rows_per_mod)
    return out.reshape(b, t, D_MODEL), kv32, s_fin


def kernel(x_prompt, x_sample, cache_k, cache_v, state_ret, c, c_ctx, w_mod, b_mod, g_pre, g_post, w_in,
           w_four, ret_decay, ret_gn, attn_sink, w_branch_a, w_branch_b, w_branch_c, w_out):
    batch, seq, _ = x_prompt.shape
    dec_batch, dec_seq, _ = x_sample.shape
    past = cache_k.shape[2]
    assert 1 + dec_batch <= MOD_ROWS

    cv = jnp.zeros((MOD_ROWS, D_MODEL), F32).at[0].set(c_ctx).at[1:1 + dec_batch].set(c)
    mod = _modulation(cv, w_mod, b_mod)

    dft_chan, dft_ctx = _dft_tables(seq)
    _, dft_lat = _dft_tables(dec_seq)
    rope_tabs = _rope_tables(dec_seq)
    ck = cache_k.reshape(dec_batch, DEPTH, past, ATT_KV_W)
    cvv = cache_v.reshape(dec_batch, DEPTH, past, ATT_KV_W)

    p = dict(
        g_pre=g_pre, g_post=g_post,
        w_in=w_in, w_four=w_four, w_pa=w_branch_a, w_pb=w_branch_b, w_pc=w_branch_c, w_out=w_out,
        dec=ret_decay, gn=ret_gn, sink=attn_sink)

    xp = x_prompt
    kv_all, s_all = (), ()
    for l in range(DEPTH):
        xp, kv_all, s_all = _layer(xp, l, mod, p, latent=False, dft_chan=dft_chan, dft_pos=dft_ctx,
                                   kv_carry=tuple(kv_all), state_carry=s_all)
        s_all = (s_all,)

    xs = x_sample
    for l in range(DEPTH):
        xs, _, _ = _layer(xs, l, mod, p, latent=True, dft_chan=dft_chan, dft_pos=dft_lat,
                             rope_tabs=rope_tabs, cache_k=ck, cache_v=cvv, state=state_ret)

    new_k, new_v = (jnp.swapaxes(a.reshape(DEPTH, batch, seq, ATT_KV_HEADS, HEAD_DIM), 0, 1) for a in kv_all)
    return (xp, xs, new_k, new_v, jnp.swapaxes(s_all[0], 0, 1))
```

```python
import functools
import math

import numpy as np
import jax
import jax.numpy as jnp
from jax import lax
from jax.experimental import pallas as pl
from jax.experimental.pallas import tpu as pltpu

F32 = jnp.float32
BF16 = jnp.bfloat16

D_MODEL = 1024
DEPTH = 2
GRID_W = 64
HEAD_DIM = 64
FOURIER_GROUPS = 4
FOURIER_GROUP_W = 64
FOURIER_W = FOURIER_GROUPS * FOURIER_GROUP_W
RET_HEADS = 4
RET_DK = 64
RET_W = RET_HEADS * RET_DK
RET_CHUNK = 128
ATT_Q_HEADS = 8
ATT_KV_HEADS = 2
ATT_W = ATT_Q_HEADS * HEAD_DIM
ATT_KV_W = ATT_KV_HEADS * HEAD_DIM
WINDOW = 128
ROPE_BASE = 10000.0
EPS = 1e-6
MOD_ROWS = 8
LANES = 128
NEG = -1e30
LOG2E = math.log2(math.e)
VMEM_LIMIT = 56 * 1024 * 1024

C_FX, C_RQ, C_RZ_END = 0, 512, 1536
C_AQ, C_AK, C_AV, C_AZ, C_MG, C_END = 1536, 2048, 2176, 2304, 2816, 5888

TM_INPROJ = 1024
TM_OUTPROJ = 1024
SUB_ROWS = 512
TQ_LATENT = 256
TQ_FOURIER = 512
NB_CTX_RET = 8
NB_CTX_ATTN = 4
BG_CTX_FOURIER = 8
RET_UNROLL = 4


def _sigmoid(x):
    return 0.5 * jnp.tanh(0.5 * x) + 0.5


def _silu(x):
    return x * _sigmoid(x)


def _params(sem):
    return pltpu.CompilerParams(dimension_semantics=sem, vmem_limit_bytes=VMEM_LIMIT)


def _const_spec(shape):
    nd = len(shape)
    return pl.BlockSpec(shape, lambda *_: (0,) * nd, pipeline_mode=pl.Buffered(1))


def _layer_spec(shape, layer, col_block=0):
    idx = (layer,) + (0,) * (len(shape) - 1) + (col_block,)
    return pl.BlockSpec((None,) + tuple(shape), lambda *_: idx, pipeline_mode=pl.Buffered(1))


def _carry_args(carry, first_input, first_output):
    specs = [pl.BlockSpec(memory_space=pl.ANY)] * len(carry)
    aliases = {first_input + k: first_output + k for k in range(len(carry))}
    return specs, list(carry), aliases


def _dft_tables(t):
    c = np.arange(FOURIER_GROUP_W)
    ang = 2.0 * np.pi * ((c[:, None] * c[None, :]) % FOURIER_GROUP_W) / FOURIER_GROUP_W
    eye = np.eye(FOURIER_GROUPS)
    s64 = FOURIER_GROUP_W ** -0.5
    chan = np.concatenate([np.kron(eye, np.cos(ang) * s64), np.kron(eye, np.sin(ang) * s64)], axis=1)
    p = np.arange(t)
    angt = 2.0 * np.pi * ((p[:, None] * p[None, :]) % t) / t
    pos = np.concatenate([np.cos(angt), -np.sin(angt)], axis=1) * (t ** -0.5)
    return jnp.asarray(chan, F32).astype(BF16), jnp.asarray(pos, F32).astype(BF16)


def _rope_tables(t):
    quarter = HEAD_DIM // 4
    lane = np.arange(LANES) % HEAD_DIM
    inv = ROPE_BASE ** (-(lane % quarter).astype(np.float64) / quarter)
    n = np.arange(t)
    pos = np.where(lane[None, :] < HEAD_DIM // 2, (n // GRID_W)[:, None], (n % GRID_W)[:, None])
    ang = pos.astype(np.float64) * inv[None, :]
    sign = np.where((lane % (2 * quarter)) < quarter, -1.0, 1.0)
    return jnp.asarray(np.cos(ang), F32), jnp.asarray(np.sin(ang) * sign[None, :], F32)


def _rope(x, cos, sin):
    lane = lax.broadcasted_iota(jnp.int32, x.shape, 1)
    first = (lane & 31) < 16
    partner = jnp.where(first, pltpu.roll(x, LANES - 16, axis=1), pltpu.roll(x, 16, axis=1))
    return x * cos + partner * sin


def _split_bf16(x):
    hi = x.astype(BF16)
    return hi, (x - hi.astype(F32)).astype(BF16)


def _mod_kernel(cv_ref, w_ref, b_ref, o_ref):
    a_hi, a_lo = _split_bf16(_silu(cv_ref[...]))
    w_hi, w_lo = _split_bf16(w_ref[...])
    dot = functools.partial(jnp.dot, preferred_element_type=F32)
    bias = b_ref[pl.ds(pl.program_id(0), 1), :]
    o_ref[...] = dot(a_hi, w_hi) + (dot(a_lo, w_hi) + dot(a_hi, w_lo)) + bias


def _modulation(cv, w_mod, b_mod):
    tn = 1024
    return pl.pallas_call(
        _mod_kernel,
        grid=(DEPTH, 3 * D_MODEL // tn),
        in_specs=[pl.BlockSpec((MOD_ROWS, D_MODEL), lambda l, j: (0, 0)),
                  pl.BlockSpec((None, D_MODEL, tn), lambda l, j: (l, 0, j)),
                  pl.BlockSpec((DEPTH, tn), lambda l, j: (0, j))],
        out_specs=pl.BlockSpec((None, MOD_ROWS, tn), lambda l, j: (l, 0, j)),
        out_shape=jax.ShapeDtypeStruct((DEPTH, MOD_ROWS, 3 * D_MODEL), F32),
        compiler_params=_params(("parallel", "parallel")),
        name="modulation",
    )(cv, w_mod, b_mod)


def _modulated_norm(x, g, mod_ref, tm, row0, rows_per_mod):
    row = row0 + (pl.program_id(0) * tm) // rows_per_mod
    ms = jnp.mean(x * x, axis=-1, keepdims=True)
    y = x * lax.rsqrt(ms + EPS) * g
    shift = mod_ref[pl.ds(row, 1), 0:D_MODEL]
    scale = mod_ref[pl.ds(row, 1), D_MODEL:2 * D_MODEL]
    return (y * (1.0 + scale) + shift).astype(BF16), row


def _inproj_kernel(x_ref, mod_ref, g_ref, w_ref, dft_ref, *refs, layer, tm, seq, row0, rows_per_mod,
                   n_carry):
    xcs_ref, fz_ref, ur_ref, ua_ref, *kv_refs = refs[n_carry:]
    for r in range(tm // SUB_ROWS):
        rows = slice(r * SUB_ROWS, (r + 1) * SUB_ROWS)
        h, _ = _modulated_norm(x_ref[rows, :], g_ref[layer:layer + 1, :], mod_ref, tm, row0, rows_per_mod)

        def mm(c0, c1):
            return jnp.dot(h, w_ref[:, c0:c1].astype(BF16), preferred_element_type=F32)

        f = mm(C_FX, C_RQ)
        xcs_ref[rows, :] = jnp.dot(f[:, :FOURIER_W].astype(BF16), dft_ref[...],
                                   preferred_element_type=F32).astype(BF16)
        fz_ref[rows, :] = f[:, FOURIER_W:].astype(BF16)
        ur_ref[rows, :] = mm(C_RQ, C_RZ_END).astype(BF16)
        ua_ref[rows, 0:ATT_W] = mm(C_AQ, C_AK).astype(BF16)
        ua_ref[rows, ATT_W:2 * ATT_W] = mm(C_AZ, C_MG).astype(BF16)
        kv = mm(C_AK, C_AZ)
        if kv_refs:
            for s in range(SUB_ROWS // seq):
                blk = kv[s * seq:(s + 1) * seq]
                bi = (r * SUB_ROWS) // seq + s
                kv_refs[0][bi] = blk[:, :ATT_KV_W].T.reshape(ATT_KV_HEADS, HEAD_DIM, seq)
                kv_refs[1][bi] = blk[:, ATT_KV_W:].T.reshape(ATT_KV_HEADS, HEAD_DIM, seq)
        ua_ref[rows, 2 * ATT_W:] = kv.astype(BF16)


def _inproj(x2, layer, mod, g_pre, w_in, dft, *, seq, row0, rows_per_mod, emit_kv, kv_carry=()):
    n = x2.shape[0]
    tm = TM_INPROJ
    row_spec = lambda w: pl.BlockSpec((tm, w), lambda i: (i, 0))
    widths = (2 * FOURIER_W, FOURIER_W, 4 * RET_W, 2 * ATT_W + 2 * ATT_KV_W)
    out_shape = [jax.ShapeDtypeStruct((n, w), BF16) for w in widths]
    out_specs = [row_spec(w) for w in widths]
    if emit_kv:
        assert SUB_ROWS % seq == 0
        out_shape += [jax.ShapeDtypeStruct((n // seq, DEPTH, ATT_KV_HEADS, HEAD_DIM, seq), F32)] * 2
        out_specs += [pl.BlockSpec((tm // seq, None, ATT_KV_HEADS, HEAD_DIM, seq),
                                   lambda i: (i, layer, 0, 0, 0))] * 2
    args = [x2, mod, g_pre, w_in, dft]
    carry_specs, carry, aliases = _carry_args(kv_carry, len(args), len(widths))
    return pl.pallas_call(
        functools.partial(_inproj_kernel, layer=layer, tm=tm, seq=seq, row0=row0, rows_per_mod=rows_per_mod,
                          n_carry=len(carry)),
        grid=(n // tm,),
        in_specs=[row_spec(D_MODEL),
                  _layer_spec((MOD_ROWS, 3 * D_MODEL), layer),
                  _const_spec((DEPTH, D_MODEL)),
                  _layer_spec((D_MODEL, C_MG), layer),
                  _const_spec((FOURIER_W, 2 * FOURIER_W))] + carry_specs,
        out_specs=out_specs,
        out_shape=out_shape,
        input_output_aliases=aliases,
        compiler_params=_params(("parallel",)),
        name="inproj",
    )(*args, *carry)


def _fourier_kernel(ct_ref, xcs_ref, fz_ref, w_ref, ya_ref, xcat_ref, *, bg, t):
    @pl.when(pl.program_id(1) == 0)
    def _gather():
        for b in range(bg):
            cols = slice(b * FOURIER_W, (b + 1) * FOURIER_W)
            xcat_ref[0:t, cols] = xcs_ref[b, :, 0:FOURIER_W]
            xcat_ref[t:2 * t, cols] = xcs_ref[b, :, FOURIER_W:]

    yr = jnp.dot(ct_ref[...], xcat_ref[...], preferred_element_type=F32).astype(BF16)
    w = w_ref[...].astype(BF16)
    for b in range(bg):
        ya = jnp.dot(yr[:, b * FOURIER_W:(b + 1) * FOURIER_W], w, preferred_element_type=F32)
        ya_ref[b] = (ya * _silu(fz_ref[b].astype(F32))).astype(BF16)


def _fourier(xcs, fz, ct, w_four, layer, *, bg, tq):
    b, t, _ = xcs.shape
    return pl.pallas_call(
        functools.partial(_fourier_kernel, bg=bg, t=t),
        grid=(b // bg, t // tq),
        in_specs=[pl.BlockSpec((tq, 2 * t), lambda i, j: (j, 0)),
                  pl.BlockSpec((bg, t, 2 * FOURIER_W), lambda i, j: (i, 0, 0)),
                  pl.BlockSpec((bg, tq, FOURIER_W), lambda i, j: (i, j, 0)),
                  _layer_spec((FOURIER_W, FOURIER_W), layer)],
        out_specs=pl.BlockSpec((bg, tq, FOURIER_W), lambda i, j: (i, j, 0)),
        out_shape=jax.ShapeDtypeStruct((b, t, FOURIER_W), BF16),
        scratch_shapes=[pltpu.VMEM((2 * t, bg * FOURIER_W), BF16)],
        compiler_params=_params(("parallel", "arbitrary")),
        name="fourier",
    )(ct, xcs, fz, w_four)


def _log_sigmoid(x):
    return jnp.minimum(x, 0.0) - jnp.log(1.0 + jnp.exp(-jnp.abs(x)))


def _head_blocks(a, width):
    lane = lax.broadcasted_iota(jnp.int32, a.shape, 1)
    zero = jnp.zeros_like(a)
    return jnp.concatenate(
        [jnp.where((lane >= h * width) & (lane < (h + 1) * width), a, zero) for h in range(RET_HEADS)],
        axis=0)


def _per_head(dec_ref, layer, direction, head_of):
    out = jnp.full(head_of.shape, dec_ref[layer, direction, RET_HEADS - 1], F32)
    for h in range(RET_HEADS - 2, -1, -1):
        out = jnp.where(head_of == h, dec_ref[layer, direction, h], out)
    return out


def _ret_kernel(*refs, nb, t, layer, rope, has_s0, n_carry, emit_state):
    refs = list(refs)
    dec_ref = refs.pop(0)
    ur_ref = refs.pop(0)
    cos_ref = refs.pop(0) if rope else None
    sin_ref = refs.pop(0) if rope else None
    s0_ref = refs.pop(0) if has_s0 else None
    gn_ref = refs.pop(0)
    del refs[:n_carry]
    yb_ref = refs.pop(0)
    sfin_ref = refs.pop(0) if emit_state else None
    (kr_ref, dsf_ref, dsb_ref, sfs_ref, sbs_ref, st_ref) = refs
    c = RET_CHUNK
    nc = t // c
    nch = nb * nc
    group = RET_UNROLL

    lane_head = lax.broadcasted_iota(jnp.int32, (1, RET_W), 1) // RET_DK
    lgf = _log_sigmoid(_per_head(dec_ref, layer, 0, lane_head))
    lgb = _log_sigmoid(_per_head(dec_ref, layer, 1, lane_head))
    ri = lax.broadcasted_iota(jnp.int32, (c, RET_W), 0).astype(F32)
    read_f = jnp.exp((ri + 1.0) * lgf)
    read_b = jnp.exp((c - ri) * lgb)
    write_f = jnp.exp((c - 1.0 - ri) * lgf)
    write_b = jnp.exp(ri * lgb)
    carry_f = jnp.exp(c * lgf)
    carry_b = jnp.exp(c * lgb)
    ii = lax.broadcasted_iota(jnp.int32, (RET_HEADS * c, c), 0)
    jj = lax.broadcasted_iota(jnp.int32, (RET_HEADS * c, c), 1)
    row_head = ii // c
    diff = ((ii & (c - 1)) - jj).astype(F32)
    decay = (jnp.where(diff >= 0, jnp.exp(jnp.maximum(diff, 0.0)
                                          * _log_sigmoid(_per_head(dec_ref, layer, 0, row_head))), 0.0)
             + jnp.where(diff <= 0, jnp.exp(jnp.maximum(-diff, 0.0)
                                            * _log_sigmoid(_per_head(dec_ref, layer, 1, row_head))), 0.0))
    r2 = lax.broadcasted_iota(jnp.int32, (RET_W, RET_W), 0)
    c2 = lax.broadcasted_iota(jnp.int32, (RET_W, RET_W), 1)
    same_head = (r2 // RET_DK) == (c2 // RET_DK)
    group_mean = jnp.where(same_head, 1.0 / RET_DK, 0.0).astype(BF16)
    out_head = lax.broadcasted_iota(jnp.int32, (c, RET_W), 1) // RET_DK

    def chunk_pos(ci):
        return ci // nc, pl.multiple_of((ci % nc) * c, c)

    def load_qk(col, b, r0):
        a = ur_ref[b, pl.ds(r0, c), col:col + RET_W].astype(F32)
        if rope:
            cs = cos_ref[pl.ds(r0, c), :]
            sn = sin_ref[pl.ds(r0, c), :]
            a = jnp.concatenate([_rope(a[:, :LANES], cs, sn), _rope(a[:, LANES:], cs, sn)], axis=1)
        return a

    def increments(ci, carry):
        b, r0 = chunk_pos(ci)
        k = load_qk(RET_W, b, r0) * (RET_DK ** -0.5)
        v = ur_ref[b, pl.ds(r0, c), 2 * RET_W:3 * RET_W]
        kr_ref[pl.ds(pl.multiple_of(ci * c, c), c), :] = k.astype(BF16)
        kw = jnp.concatenate([(k * write_f).astype(BF16), (k * write_b).astype(BF16)], axis=1)
        d = lax.dot_general(kw, v, (((0,), (0,)), ((), ())), preferred_element_type=F32)
        dsf_ref[ci] = jnp.where(same_head, d[:RET_W], 0.0)
        dsb_ref[ci] = jnp.where(same_head, d[RET_W:], 0.0)
        return carry

    lax.fori_loop(0, nch, increments, 0, unroll=group)

    def scan(b, direction, ds_ref, out_ref, carry_decay):
        st_ref[...] = jnp.zeros((RET_W, RET_W), F32)
        if has_s0:
            for h in range(RET_HEADS):
                sl = slice(h * RET_DK, (h + 1) * RET_DK)
                st_ref[sl, sl] = s0_ref[b, direction, h]

        def step(n, carry):
            ci = b * nc + (n if direction == 0 else nc - 1 - n)
            out_ref[ci] = st_ref[...].astype(BF16)
            st_ref[...] = carry_decay * st_ref[...] + ds_ref[ci]
            return carry

        lax.fori_loop(0, nc, step, 0, unroll=min(nc, 4))
        if emit_state:
            for h in range(RET_HEADS):
                sl = slice(h * RET_DK, (h + 1) * RET_DK)
                sfin_ref[b, direction, h] = st_ref[sl, sl]

    for b in range(nb):
        scan(b, 0, dsf_ref, sfs_ref, carry_f)
        scan(b, 1, dsb_ref, sbs_ref, carry_b)

    def chunk_output(ci, b, r0):
        q = load_qk(0, b, r0)
        kb = kr_ref[pl.ds(pl.multiple_of(ci * c, c), c), :]
        v = ur_ref[b, pl.ds(r0, c), 2 * RET_W:3 * RET_W]
        att = lax.dot_general(_head_blocks(q.astype(BF16), RET_DK), kb,
                              (((1,), (1,)), ((), ())), preferred_element_type=F32)
        o4 = jnp.dot((att * decay).astype(BF16), v, preferred_element_type=F32)
        o = o4[(RET_HEADS - 1) * c:]
        for h in range(RET_HEADS - 2, -1, -1):
            o = jnp.where(out_head == h, o4[h * c:(h + 1) * c], o)
        o = o + jnp.dot((q * read_f).astype(BF16), sfs_ref[ci], preferred_element_type=F32)
        return o + jnp.dot((q * read_b).astype(BF16), sbs_ref[ci], preferred_element_type=F32)

    def outputs(it, carry):
        pos = [chunk_pos(it * group + u) for u in range(group)]
        o = jnp.concatenate([chunk_output(it * group + u, *pos[u]) for u in range(group)], axis=0)
        ms = jnp.dot((o * o).astype(BF16), group_mean, preferred_element_type=F32)
        y = o * lax.rsqrt(ms + EPS) * gn_ref[layer:layer + 1, :]
        for u, (b, r0) in enumerate(pos):
            z = ur_ref[b, pl.ds(r0, c), 3 * RET_W:4 * RET_W].astype(F32)
            yb_ref[b, pl.ds(r0, c), :] = (y[u * c:(u + 1) * c] * _silu(z)).astype(BF16)
        return carry

    lax.fori_loop(0, nch // group, outputs, 0)


def _retention(ur, layer, dec, gn, *, nb, emit_state, rope_tabs=None, state=None, state_carry=()):
    b, t, _ = ur.shape
    nch = nb * (t // RET_CHUNK)
    assert nch % RET_UNROLL == 0
    rope = rope_tabs is not None
    has_s0 = state is not None
    in_specs = [pl.BlockSpec(memory_space=pltpu.SMEM),
                pl.BlockSpec((nb, t, 4 * RET_W), lambda i: (i, 0, 0))]
    args = [dec, ur]
    if rope:
        in_specs += [_const_spec((t, LANES)), _const_spec((t, LANES))]
        args += list(rope_tabs)
    if has_s0:
        in_specs.append(pl.BlockSpec((nb, None, 2, RET_HEADS, RET_DK, RET_DK),
                                     lambda i: (i, layer, 0, 0, 0, 0)))
        args.append(state)
    in_specs.append(_const_spec((DEPTH, RET_W)))
    args.append(gn)
    carry_specs, carry, aliases = _carry_args(state_carry, len(args), 1)
    state_scratch = lambda dt: pltpu.VMEM((nch, RET_W, RET_W), dt)
    out_specs = [pl.BlockSpec((nb, t, RET_W), lambda i: (i, 0, 0))]
    out_shape = [jax.ShapeDtypeStruct((b, t, RET_W), BF16)]
    if emit_state:
        out_specs.append(pl.BlockSpec((nb, None, 2, RET_HEADS, RET_DK, RET_DK),
                                      lambda i: (i, layer, 0, 0, 0, 0)))
        out_shape.append(jax.ShapeDtypeStruct((b, DEPTH, 2, RET_HEADS, RET_DK, RET_DK), F32))
    return pl.pallas_call(
        functools.partial(_ret_kernel, nb=nb, t=t, layer=layer, rope=rope, has_s0=has_s0,
                          n_carry=len(carry), emit_state=emit_state),
        grid=(b // nb,),
        in_specs=in_specs + carry_specs,
        out_specs=out_specs,
        out_shape=out_shape,
        input_output_aliases=aliases,
        scratch_shapes=[pltpu.VMEM((nb * t, RET_W), BF16),
                        state_scratch(F32), state_scratch(F32), state_scratch(BF16), state_scratch(BF16),
                        pltpu.VMEM((RET_W, RET_W), F32)],
        compiler_params=_params(("parallel",)),
        name="retention",
    )(*args, *carry)


def _kv_variants(a, ones_block):
    lane_half = lax.broadcasted_iota(jnp.int32, a.shape, 1) // HEAD_DIM
    swapped = pltpu.roll(a, HEAD_DIM, axis=1)
    out = []
    for g in range(ATT_KV_HEADS):
        row = []
        for half in range(2):
            var = jnp.where(lane_half == half, a if half == g else swapped, 0.0).astype(BF16)
            if ones_block:
                ones = jnp.where(lane_half == half, 1.0, 0.0).astype(BF16)
                var = jnp.concatenate([var, ones], axis=1)
            row.append(var)
        out.append(row)
    return out


def _attend_group(q2, keys, values, masks, sink_ref, layer, g, z2):
    rows = q2.shape[0]
    upper = lax.broadcasted_iota(jnp.int32, (rows, 1), 0) < rows // 2
    acc = None
    sink_terms = []
    for half in range(2):
        sink = jnp.where(upper, sink_ref[layer, g, half], sink_ref[layer, g, 2 + half]) * LOG2E
        logits = []
        for kpart, mask in zip(keys[half], masks):
            l = lax.dot_general(q2, kpart, (((1,), (1,)), ((), ())), preferred_element_type=F32)
            logits.append(l if mask is None else jnp.where(mask, l, NEG))
        m = sink
        for l in logits:
            m = jnp.maximum(m, jnp.max(l, axis=-1, keepdims=True))
        for l, vpart in zip(logits, values[half]):
            pv = jnp.dot(jnp.exp2(l - m).astype(BF16), vpart, preferred_element_type=F32)
            acc = pv if acc is None else acc + pv
        sink_terms.append(jnp.exp2(sink - m))
    lane = lax.broadcasted_iota(jnp.int32, (rows, LANES), 1)
    den = acc[:, LANES:] + jnp.where(lane < HEAD_DIM, sink_terms[0], sink_terms[1])
    return (acc[:, :LANES] * (1.0 / den) * _silu(z2)).astype(BF16)


def _pair_rows(ref_slice, g):
    return jnp.concatenate([ref_slice(2 * g), ref_slice(2 * g + 1)], axis=0)


def _ctx_attn_kernel(sink_ref, ua_ref, yc_ref, *, nb, t, layer):
    for b in range(nb):
        kv = ua_ref[b, :, 2 * ATT_W:].astype(F32)
        kvar = _kv_variants(kv[:, :LANES] * (HEAD_DIM ** -0.5 * LOG2E), False)
        vvar = _kv_variants(kv[:, LANES:], True)
        for g in range(ATT_KV_HEADS):
            q2 = _pair_rows(lambda p: ua_ref[b, :, p * LANES:(p + 1) * LANES], g)
            z2 = _pair_rows(lambda p: ua_ref[b, :, ATT_W + p * LANES:ATT_W + (p + 1) * LANES], g)
            o = _attend_group(q2, [[kvar[g][0]], [kvar[g][1]]], [[vvar[g][0]], [vvar[g][1]]], [None],
                              sink_ref, layer, g, z2.astype(F32))
            yc_ref[b, :, 2 * g * LANES:(2 * g + 1) * LANES] = o[:t]
            yc_ref[b, :, (2 * g + 1) * LANES:(2 * g + 2) * LANES] = o[t:]


def _ctx_attention(ua, layer, sink):
    b, t, w = ua.shape
    nb = NB_CTX_ATTN
    return pl.pallas_call(
        functools.partial(_ctx_attn_kernel, nb=nb, t=t, layer=layer),
        grid=(b // nb,),
        in_specs=[pl.BlockSpec(memory_space=pltpu.SMEM),
                  pl.BlockSpec((nb, t, w), lambda i: (i, 0, 0))],
        out_specs=pl.BlockSpec((nb, t, ATT_W), lambda i: (i, 0, 0)),
        out_shape=jax.ShapeDtypeStruct((b, t, ATT_W), BF16),
        compiler_params=_params(("parallel",)),
        name="ctx_attention",
    )(sink, ua)


def _lat_attn_kernel(sink_ref, qz_ref, kv_ref, ck_ref, cv_ref, cos_ref, sin_ref, yc_ref,
                     kl_ref, vl_ref, kc_ref, vc_ref, *, t, tq, layer):
    j = pl.program_id(1)
    scale = HEAD_DIM ** -0.5 * LOG2E

    @pl.when(j == 0)
    def _prepare():
        kv = kv_ref[...].astype(F32)
        kvar = _kv_variants(_rope(kv[:, :LANES], cos_ref[...], sin_ref[...]) * scale, False)
        vvar = _kv_variants(kv[:, LANES:], True)
        cvar = _kv_variants(ck_ref[...] * scale, False)
        dvar = _kv_variants(cv_ref[...], True)
        for g in range(ATT_KV_HEADS):
            for half in range(2):
                i = 2 * g + half
                for ref, var in ((kl_ref, kvar), (vl_ref, vvar)):
                    pad = jnp.zeros((WINDOW, ref.shape[-1]), BF16)
                    ref[i, 0:WINDOW, :] = pad
                    ref[i, WINDOW:WINDOW + t, :] = var[g][half]
                    ref[i, WINDOW + t:, :] = pad
                kc_ref[i] = cvar[g][half]
                vc_ref[i] = dvar[g][half]

    r0 = pl.multiple_of(j * tq, tq)
    nloc = tq + 2 * WINDOW
    rr = lax.broadcasted_iota(jnp.int32, (2 * tq, nloc), 0) & (tq - 1)
    ss = lax.broadcasted_iota(jnp.int32, (2 * tq, nloc), 1)
    band = ((ss - rr >= 0) & (ss - rr <= 2 * WINDOW)
            & (ss >= WINDOW - j * tq) & (ss < t + WINDOW - j * tq))
    cs = cos_ref[pl.ds(r0, tq), :]
    sn = sin_ref[pl.ds(r0, tq), :]
    for g in range(ATT_KV_HEADS):
        q2 = _pair_rows(
            lambda p: _rope(qz_ref[:, p * LANES:(p + 1) * LANES].astype(F32), cs, sn).astype(BF16), g)
        z2 = _pair_rows(lambda p: qz_ref[:, ATT_W + p * LANES:ATT_W + (p + 1) * LANES], g)
        keys = [[kl_ref[2 * g + half, pl.ds(r0, nloc), :], kc_ref[2 * g + half]] for half in range(2)]
        vals = [[vl_ref[2 * g + half, pl.ds(r0, nloc), :], vc_ref[2 * g + half]] for half in range(2)]
        o = _attend_group(q2, keys, vals, [band, None], sink_ref, layer, g, z2.astype(F32))
        yc_ref[:, 2 * g * LANES:(2 * g + 1) * LANES] = o[:tq]
        yc_ref[:, (2 * g + 1) * LANES:(2 * g + 2) * LANES] = o[tq:]


def _lat_attention(ua, cache_k, cache_v, layer, sink, rope_tabs):
    b, t, _ = ua.shape
    past = cache_k.shape[2]
    tq = TQ_LATENT
    nvar = 2 * ATT_KV_HEADS
    cache_spec = pl.BlockSpec((None, None, past, ATT_KV_W), lambda i, j: (i, layer, 0, 0))
    return pl.pallas_call(
        functools.partial(_lat_attn_kernel, t=t, tq=tq, layer=layer),
        grid=(b, t // tq),
        in_specs=[pl.BlockSpec(memory_space=pltpu.SMEM),
                  pl.BlockSpec((None, tq, 2 * ATT_W), lambda i, j: (i, j, 0)),
                  pl.BlockSpec((None, t, 2 * ATT_KV_W), lambda i, j: (i, 0, 2 * ATT_W // (2 * ATT_KV_W))),
                  cache_spec, cache_spec,
                  _const_spec((t, LANES)), _const_spec((t, LANES))],
        out_specs=pl.BlockSpec((None, tq, ATT_W), lambda i, j: (i, j, 0)),
        out_shape=jax.ShapeDtypeStruct((b, t, ATT_W), BF16),
        scratch_shapes=[pltpu.VMEM((nvar, t + 2 * WINDOW, LANES), BF16),
                        pltpu.VMEM((nvar, t + 2 * WINDOW, 2 * LANES), BF16),
                        pltpu.VMEM((nvar, past, LANES), BF16),
                        pltpu.VMEM((nvar, past, 2 * LANES), BF16)],
        compiler_params=_params(("parallel", "arbitrary")),
        name="lat_attention",
    )(sink, ua, ua, cache_k, cache_v, *rope_tabs)


def _outproj_kernel(x_ref, ya_ref, yb_ref, yc_ref, mod_ref, gpre_ref, gpost_ref,
                    wg_ref, wgt_ref, wa_ref, wb_ref, wc_ref, wo_ref, o_ref, *, layer, tm, row0, rows_per_mod):
    for r in range(tm // SUB_ROWS):
        rows = slice(r * SUB_ROWS, (r + 1) * SUB_ROWS)
        x = x_ref[rows, :]
        h, row = _modulated_norm(x, gpre_ref[layer:layer + 1, :], mod_ref, tm, row0, rows_per_mod)

        def gate_logits(c):
            lo, hi = c * D_MODEL, min((c + 1) * D_MODEL, C_MG)
            mg = jnp.dot(h, wg_ref[:, lo:hi].astype(BF16), preferred_element_type=F32)
            if hi - lo < D_MODEL:
                tail = jnp.dot(h, wgt_ref[...].astype(BF16), preferred_element_type=F32)
                mg = jnp.concatenate([mg, tail], axis=1)
            return mg

        merged = None
        for c, (y_ref, w_ref) in enumerate(((ya_ref, wa_ref), (yb_ref, wb_ref), (yc_ref, wc_ref))):
            term = _sigmoid(gate_logits(c)) * jnp.dot(y_ref[rows, :], w_ref[...].astype(BF16),
                                                      preferred_element_type=F32)
            merged = term if merged is None else merged + term
        out = jnp.dot(merged.astype(BF16), wo_ref[...].astype(BF16), preferred_element_type=F32)
        ms = jnp.mean(out * out, axis=-1, keepdims=True)
        normed = out * lax.rsqrt(ms + EPS) * gpost_ref[layer:layer + 1, :]
        gate = mod_ref[pl.ds(row, 1), 2 * D_MODEL:3 * D_MODEL]
        o_ref[rows, :] = x + gate * normed


def _outproj(x2, ya, yb, yc, layer, mod, g_pre, g_post, w_in, wa, wb, wc, wo, *, row0, rows_per_mod):
    n = x2.shape[0]
    tm = TM_OUTPROJ
    row_spec = lambda w: pl.BlockSpec((tm, w), lambda i: (i, 0))
    return pl.pallas_call(
        functools.partial(_outproj_kernel, layer=layer, tm=tm, row0=row0, rows_per_mod=rows_per_mod),
        grid=(n // tm,),
        in_specs=[row_spec(D_MODEL), row_spec(FOURIER_W), row_spec(RET_W), row_spec(ATT_W),
                  _layer_spec((MOD_ROWS, 3 * D_MODEL), layer),
                  _const_spec((DEPTH, D_MODEL)), _const_spec((DEPTH, D_MODEL)),
                  _layer_spec((D_MODEL, C_MG), layer, col_block=1),
                  _layer_spec((D_MODEL, C_END - 2 * C_MG), layer, col_block=2 * C_MG // (C_END - 2 * C_MG)),
                  _layer_spec((FOURIER_W, D_MODEL), layer), _layer_spec((RET_W, D_MODEL), layer),
                  _layer_spec((ATT_W, D_MODEL), layer), _layer_spec((D_MODEL, D_MODEL), layer)],
        out_specs=row_spec(D_MODEL),
        out_shape=jax.ShapeDtypeStruct((n, D_MODEL), F32),
        compiler_params=_params(("parallel",)),
        name="outproj",
    )(x2, ya, yb, yc, mod, g_pre, g_post, w_in, w_in, wa, wb, wc, wo)


def _layer(x, layer, mod, p, *, latent, dft_chan, dft_pos, rope_tabs=None, cache_k=None, cache_v=None,
           state=None, kv_carry=(), state_carry=()):
    b, t, _ = x.shape
    n = b * t
    row0, rows_per_mod = (1, t) if latent else (0, n)
    x2 = x.reshape(n, D_MODEL)
    xcs, fz, ur, ua, *kv32 = _inproj(x2, layer, mod, p["g_pre"], p["w_in"], dft_chan,
                                     seq=t, row0=row0, rows_per_mod=rows_per_mod, emit_kv=not latent,
                                     kv_carry=kv_carry)
    ya = _fourier(xcs.reshape(b, t, -1), fz.reshape(b, t, -1), dft_pos, p["w_four"], layer,
                  bg=b if latent else BG_CTX_FOURIER, tq=min(t, TQ_FOURIER))
    ur3 = ur.reshape(b, t, -1)
    ua3 = ua.reshape(b, t, -1)
    if latent:
        (yb,), s_fin = _retention(ur3, layer, p["dec"], p["gn"], nb=1, emit_state=False,
                                  rope_tabs=rope_tabs, state=state), None
        yc = _lat_attention(ua3, cache_k, cache_v, layer, p["sink"], rope_tabs)
    else:
        yb, s_fin = _retention(ur3, layer, p["dec"], p["gn"], nb=NB_CTX_RET, emit_state=True,
                               state_carry=state_carry)
        yc = _ctx_attention(ua3, layer, p["sink"])
    out = _outproj(x2, ya.reshape(n, -1), yb.reshape(n, -1), yc.reshape(n, -1), layer, mod,
                   p["g_pre"], p["g_post"], p["w_in"], p["w_pa"], p["w_pb"], p["w_pc"], p["w_out"],
                   row0=row0, rows_per_mod=rows_per_mod)
    return out.reshape(b, t, D_MODEL), kv32, s_fin


def kernel(x_prompt, x_sample, cache_k, cache_v, state_ret, c, c_ctx, w_mod, b_mod, g_pre, g_post, w_in,
           w_four, ret_decay, ret_gn, attn_sink, w_branch_a, w_branch_b, w_branch_c, w_out):
    batch, seq, _ = x_prompt.shape
    dec_batch, dec_seq, _ = x_sample.shape
    past = cache_k.shape[2]
    assert 1 + dec_batch <= MOD_ROWS

    cv = jnp.zeros((MOD_ROWS, D_MODEL), F32).at[0].set(c_ctx).at[1:1 + dec_batch].set(c)
    mod = _modulation(cv, w_mod, b_mod)

    dft_chan, dft_ctx = _dft_tables(seq)
    _, dft_lat = _dft_tables(dec_seq)
    rope_tabs = _rope_tables(dec_seq)
    ck = cache_k.reshape(dec_batch, DEPTH, past, ATT_KV_W)
    cvv = cache_v.reshape(dec_batch, DEPTH, past, ATT_KV_W)

    p = dict(
        g_pre=g_pre, g_post=g_post,
        w_in=w_in, w_four=w_four, w_pa=w_branch_a, w_pb=w_branch_b, w_pc=w_branch_c, w_out=w_out,
        dec=ret_decay, gn=ret_gn, sink=attn_sink)

    xp = x_prompt
    kv_all, s_all = (), ()
    for l in range(DEPTH):
        xp, kv_all, s_all = _layer(xp, l, mod, p, latent=False, dft_chan=dft_chan, dft_pos=dft_ctx,
                                   kv_carry=tuple(kv_all), state_carry=s_all)
        s_all = (s_all,)

    xs = x_sample
    for l in range(DEPTH):
        xs, _, _ = _layer(xs, l, mod, p, latent=True, dft_chan=dft_chan, dft_pos=dft_lat,
                             rope_tabs=rope_tabs, cache_k=ck, cache_v=cvv, state=state_ret)

    new_k, new_v = (jnp.transpose(a, (0, 1, 4, 2, 3)) for a in kv_all)
    return (xp, xs, new_k, new_v, s_all[0])
```

```python
import functools
import math

import numpy as np
import jax
import jax.numpy as jnp
from jax import lax
from jax.experimental import pallas as pl
from jax.experimental.pallas import tpu as pltpu

F32 = jnp.float32
BF16 = jnp.bfloat16

D_MODEL = 1024
DEPTH = 2
GRID_W = 64
HEAD_DIM = 64
FOURIER_GROUPS = 4
FOURIER_GROUP_W = 64
FOURIER_W = FOURIER_GROUPS * FOURIER_GROUP_W
RET_HEADS = 4
RET_DK = 64
RET_W = RET_HEADS * RET_DK
RET_CHUNK = 128
ATT_Q_HEADS = 8
ATT_KV_HEADS = 2
ATT_W = ATT_Q_HEADS * HEAD_DIM
ATT_KV_W = ATT_KV_HEADS * HEAD_DIM
WINDOW = 128
ROPE_BASE = 10000.0
EPS = 1e-6
MOD_ROWS = 8
LANES = 128
NEG = -1e30
LOG2E = math.log2(math.e)
VMEM_LIMIT = 56 * 1024 * 1024

C_FX, C_RQ, C_RZ_END = 0, 512, 1536
C_AQ, C_AK, C_AV, C_AZ, C_MG, C_END = 1536, 2048, 2176, 2304, 2816, 5888

TM_INPROJ = 1024
TM_OUTPROJ = 1024
SUB_ROWS = 512
TQ_LATENT = 256
LAT_TILES_PER_STEP = 4
TQ_FOURIER = 512
NB_CTX_RET = 8
NB_CTX_ATTN = 4
BG_CTX_FOURIER = 8
RET_UNROLL = 4


def _sigmoid(x):
    return 0.5 * jnp.tanh(0.5 * x) + 0.5


def _silu(x):
    return x * _sigmoid(x)


def _params(sem):
    return pltpu.CompilerParams(dimension_semantics=sem, vmem_limit_bytes=VMEM_LIMIT)


def _const_spec(shape):
    nd = len(shape)
    return pl.BlockSpec(shape, lambda *_: (0,) * nd, pipeline_mode=pl.Buffered(1))


def _layer_spec(shape, layer, col_block=0):
    idx = (layer,) + (0,) * (len(shape) - 1) + (col_block,)
    return pl.BlockSpec((None,) + tuple(shape), lambda *_: idx, pipeline_mode=pl.Buffered(1))


def _carry_args(carry, first_input, first_output):
    specs = [pl.BlockSpec(memory_space=pl.ANY)] * len(carry)
    aliases = {first_input + k: first_output + k for k in range(len(carry))}
    return specs, list(carry), aliases


def _dft_tables(t):
    c = np.arange(FOURIER_GROUP_W)
    ang = 2.0 * np.pi * ((c[:, None] * c[None, :]) % FOURIER_GROUP_W) / FOURIER_GROUP_W
    eye = np.eye(FOURIER_GROUPS)
    s64 = FOURIER_GROUP_W ** -0.5
    chan = np.concatenate([np.kron(eye, np.cos(ang) * s64), np.kron(eye, np.sin(ang) * s64)], axis=1)
    p = np.arange(t)
    angt = 2.0 * np.pi * ((p[:, None] * p[None, :]) % t) / t
    pos = np.concatenate([np.cos(angt), -np.sin(angt)], axis=1) * (t ** -0.5)
    return jnp.asarray(chan, F32).astype(BF16), jnp.asarray(pos, F32).astype(BF16)


def _rope_tables(t):
    quarter = HEAD_DIM // 4
    lane = np.arange(LANES) % HEAD_DIM
    inv = ROPE_BASE ** (-(lane % quarter).astype(np.float64) / quarter)
    n = np.arange(t)
    pos = np.where(lane[None, :] < HEAD_DIM // 2, (n // GRID_W)[:, None], (n % GRID_W)[:, None])
    ang = pos.astype(np.float64) * inv[None, :]
    sign = np.where((lane % (2 * quarter)) < quarter, -1.0, 1.0)
    return jnp.asarray(np.cos(ang), F32), jnp.asarray(np.sin(ang) * sign[None, :], F32)


def _rope(x, cos, sin):
    lane = lax.broadcasted_iota(jnp.int32, x.shape, 1)
    first = (lane & 31) < 16
    partner = jnp.where(first, pltpu.roll(x, LANES - 16, axis=1), pltpu.roll(x, 16, axis=1))
    return x * cos + partner * sin


def _split_bf16(x):
    hi = x.astype(BF16)
    return hi, (x - hi.astype(F32)).astype(BF16)


def _mod_kernel(cv_ref, w_ref, b_ref, o_ref):
    a_hi, a_lo = _split_bf16(_silu(cv_ref[...]))
    w_hi, w_lo = _split_bf16(w_ref[...])
    dot = functools.partial(jnp.dot, preferred_element_type=F32)
    bias = b_ref[pl.ds(pl.program_id(0), 1), :]
    o_ref[...] = dot(a_hi, w_hi) + (dot(a_lo, w_hi) + dot(a_hi, w_lo)) + bias


def _modulation(cv, w_mod, b_mod):
    tn = 1024
    return pl.pallas_call(
        _mod_kernel,
        grid=(DEPTH, 3 * D_MODEL // tn),
        in_specs=[pl.BlockSpec((MOD_ROWS, D_MODEL), lambda l, j: (0, 0)),
                  pl.BlockSpec((None, D_MODEL, tn), lambda l, j: (l, 0, j)),
                  pl.BlockSpec((DEPTH, tn), lambda l, j: (0, j))],
        out_specs=pl.BlockSpec((None, MOD_ROWS, tn), lambda l, j: (l, 0, j)),
        out_shape=jax.ShapeDtypeStruct((DEPTH, MOD_ROWS, 3 * D_MODEL), F32),
        compiler_params=_params(("parallel", "parallel")),
        name="modulation",
    )(cv, w_mod, b_mod)


def _modulated_norm(x, g, mod_ref, tm, row0, rows_per_mod):
    row = row0 + (pl.program_id(0) * tm) // rows_per_mod
    ms = jnp.mean(x * x, axis=-1, keepdims=True)
    y = x * lax.rsqrt(ms + EPS) * g
    shift = mod_ref[pl.ds(row, 1), 0:D_MODEL]
    scale = mod_ref[pl.ds(row, 1), D_MODEL:2 * D_MODEL]
    return (y * (1.0 + scale) + shift).astype(BF16), row


def _inproj_kernel(x_ref, mod_ref, g_ref, w_ref, dft_ref, *refs, layer, tm, seq, row0, rows_per_mod,
                   n_carry):
    xcs_ref, fz_ref, ur_ref, ua_ref, *kv_refs = refs[n_carry:]
    for r in range(tm // SUB_ROWS):
        rows = slice(r * SUB_ROWS, (r + 1) * SUB_ROWS)
        h, _ = _modulated_norm(x_ref[rows, :], g_ref[layer:layer + 1, :], mod_ref, tm, row0, rows_per_mod)

        def mm(c0, c1):
            return jnp.dot(h, w_ref[:, c0:c1].astype(BF16), preferred_element_type=F32)

        f = mm(C_FX, C_RQ)
        xcs_ref[rows, :] = jnp.dot(f[:, :FOURIER_W].astype(BF16), dft_ref[...],
                                   preferred_element_type=F32).astype(BF16)
        fz_ref[rows, :] = f[:, FOURIER_W:].astype(BF16)
        ur_ref[rows, :] = mm(C_RQ, C_RZ_END).astype(BF16)
        ua_ref[rows, 0:ATT_W] = mm(C_AQ, C_AK).astype(BF16)
        ua_ref[rows, ATT_W:2 * ATT_W] = mm(C_AZ, C_MG).astype(BF16)
        kv = mm(C_AK, C_AZ)
        if kv_refs:
            for s in range(SUB_ROWS // seq):
                blk = kv[s * seq:(s + 1) * seq]
                bi = (r * SUB_ROWS) // seq + s
                kv_refs[0][bi] = blk[:, :ATT_KV_W].T.reshape(ATT_KV_HEADS, HEAD_DIM, seq)
                kv_refs[1][bi] = blk[:, ATT_KV_W:].T.reshape(ATT_KV_HEADS, HEAD_DIM, seq)
        ua_ref[rows, 2 * ATT_W:] = kv.astype(BF16)


def _inproj(x2, layer, mod, g_pre, w_in, dft, *, seq, row0, rows_per_mod, emit_kv, kv_carry=()):
    n = x2.shape[0]
    tm = TM_INPROJ
    row_spec = lambda w: pl.BlockSpec((tm, w), lambda i: (i, 0))
    widths = (2 * FOURIER_W, FOURIER_W, 4 * RET_W, 2 * ATT_W + 2 * ATT_KV_W)
    out_shape = [jax.ShapeDtypeStruct((n, w), BF16) for w in widths]
    out_specs = [row_spec(w) for w in widths]
    if emit_kv:
        assert SUB_ROWS % seq == 0
        out_shape += [jax.ShapeDtypeStruct((n // seq, DEPTH, ATT_KV_HEADS, HEAD_DIM, seq), F32)] * 2
        out_specs += [pl.BlockSpec((tm // seq, None, ATT_KV_HEADS, HEAD_DIM, seq),
                                   lambda i: (i, layer, 0, 0, 0))] * 2
    args = [x2, mod, g_pre, w_in, dft]
    carry_specs, carry, aliases = _carry_args(kv_carry, len(args), len(widths))
    return pl.pallas_call(
        functools.partial(_inproj_kernel, layer=layer, tm=tm, seq=seq, row0=row0, rows_per_mod=rows_per_mod,
                          n_carry=len(carry)),
        grid=(n // tm,),
        in_specs=[row_spec(D_MODEL),
                  _layer_spec((MOD_ROWS, 3 * D_MODEL), layer),
                  _const_spec((DEPTH, D_MODEL)),
                  _layer_spec((D_MODEL, C_MG), layer),
                  _const_spec((FOURIER_W, 2 * FOURIER_W))] + carry_specs,
        out_specs=out_specs,
        out_shape=out_shape,
        input_output_aliases=aliases,
        compiler_params=_params(("parallel",)),
        name="inproj",
    )(*args, *carry)


def _fourier_kernel(ct_ref, xcs_ref, fz_ref, w_ref, ya_ref, xcat_ref, *, bg, t):
    @pl.when(pl.program_id(1) == 0)
    def _gather():
        for b in range(bg):
            cols = slice(b * FOURIER_W, (b + 1) * FOURIER_W)
            xcat_ref[0:t, cols] = xcs_ref[b, :, 0:FOURIER_W]
            xcat_ref[t:2 * t, cols] = xcs_ref[b, :, FOURIER_W:]

    yr = jnp.dot(ct_ref[...], xcat_ref[...], preferred_element_type=F32).astype(BF16)
    w = w_ref[...].astype(BF16)
    for b in range(bg):
        ya = jnp.dot(yr[:, b * FOURIER_W:(b + 1) * FOURIER_W], w, preferred_element_type=F32)
        ya_ref[b] = (ya * _silu(fz_ref[b].astype(F32))).astype(BF16)


def _fourier(xcs, fz, ct, w_four, layer, *, bg, tq):
    b, t, _ = xcs.shape
    return pl.pallas_call(
        functools.partial(_fourier_kernel, bg=bg, t=t),
        grid=(b // bg, t // tq),
        in_specs=[pl.BlockSpec((tq, 2 * t), lambda i, j: (j, 0)),
                  pl.BlockSpec((bg, t, 2 * FOURIER_W), lambda i, j: (i, 0, 0)),
                  pl.BlockSpec((bg, tq, FOURIER_W), lambda i, j: (i, j, 0)),
                  _layer_spec((FOURIER_W, FOURIER_W), layer)],
        out_specs=pl.BlockSpec((bg, tq, FOURIER_W), lambda i, j: (i, j, 0)),
        out_shape=jax.ShapeDtypeStruct((b, t, FOURIER_W), BF16),
        scratch_shapes=[pltpu.VMEM((2 * t, bg * FOURIER_W), BF16)],
        compiler_params=_params(("parallel", "arbitrary")),
        name="fourier",
    )(ct, xcs, fz, w_four)


def _log_sigmoid(x):
    return jnp.minimum(x, 0.0) - jnp.log(1.0 + jnp.exp(-jnp.abs(x)))


def _head_blocks(a, width):
    lane = lax.broadcasted_iota(jnp.int32, a.shape, 1)
    zero = jnp.zeros_like(a)
    return jnp.concatenate(
        [jnp.where((lane >= h * width) & (lane < (h + 1) * width), a, zero) for h in range(RET_HEADS)],
        axis=0)


def _per_head(dec_ref, layer, direction, head_of):
    out = jnp.full(head_of.shape, dec_ref[layer, direction, RET_HEADS - 1], F32)
    for h in range(RET_HEADS - 2, -1, -1):
        out = jnp.where(head_of == h, dec_ref[layer, direction, h], out)
    return out


def _ret_kernel(*refs, nb, t, layer, rope, has_s0, n_carry, emit_state):
    refs = list(refs)
    dec_ref = refs.pop(0)
    ur_ref = refs.pop(0)
    cos_ref = refs.pop(0) if rope else None
    sin_ref = refs.pop(0) if rope else None
    s0_ref = refs.pop(0) if has_s0 else None
    gn_ref = refs.pop(0)
    del refs[:n_carry]
    yb_ref = refs.pop(0)
    sfin_ref = refs.pop(0) if emit_state else None
    (kr_ref, dsf_ref, dsb_ref, sfs_ref, sbs_ref, st_ref) = refs
    c = RET_CHUNK
    nc = t // c
    nch = nb * nc
    group = RET_UNROLL

    lane_head = lax.broadcasted_iota(jnp.int32, (1, RET_W), 1) // RET_DK
    lgf = _log_sigmoid(_per_head(dec_ref, layer, 0, lane_head))
    lgb = _log_sigmoid(_per_head(dec_ref, layer, 1, lane_head))
    ri = lax.broadcasted_iota(jnp.int32, (c, RET_W), 0).astype(F32)
    read_f = jnp.exp((ri + 1.0) * lgf)
    read_b = jnp.exp((c - ri) * lgb)
    write_f = jnp.exp((c - 1.0 - ri) * lgf)
    write_b = jnp.exp(ri * lgb)
    carry_f = jnp.exp(c * lgf)
    carry_b = jnp.exp(c * lgb)
    ii = lax.broadcasted_iota(jnp.int32, (RET_HEADS * c, c), 0)
    jj = lax.broadcasted_iota(jnp.int32, (RET_HEADS * c, c), 1)
    row_head = ii // c
    diff = ((ii & (c - 1)) - jj).astype(F32)
    decay = (jnp.where(diff >= 0, jnp.exp(jnp.maximum(diff, 0.0)
                                          * _log_sigmoid(_per_head(dec_ref, layer, 0, row_head))), 0.0)
             + jnp.where(diff <= 0, jnp.exp(jnp.maximum(-diff, 0.0)
                                            * _log_sigmoid(_per_head(dec_ref, layer, 1, row_head))), 0.0))
    r2 = lax.broadcasted_iota(jnp.int32, (RET_W, RET_W), 0)
    c2 = lax.broadcasted_iota(jnp.int32, (RET_W, RET_W), 1)
    same_head = (r2 // RET_DK) == (c2 // RET_DK)
    group_mean = jnp.where(same_head, 1.0 / RET_DK, 0.0).astype(BF16)
    out_head = lax.broadcasted_iota(jnp.int32, (c, RET_W), 1) // RET_DK

    def chunk_pos(ci):
        return ci // nc, pl.multiple_of((ci % nc) * c, c)

    def load_qk(col, b, r0):
        a = ur_ref[b, pl.ds(r0, c), col:col + RET_W].astype(F32)
        if rope:
            cs = cos_ref[pl.ds(r0, c), :]
            sn = sin_ref[pl.ds(r0, c), :]
            a = jnp.concatenate([_rope(a[:, :LANES], cs, sn), _rope(a[:, LANES:], cs, sn)], axis=1)
        return a

    def increments(ci, carry):
        b, r0 = chunk_pos(ci)
        k = load_qk(RET_W, b, r0) * (RET_DK ** -0.5)
        v = ur_ref[b, pl.ds(r0, c), 2 * RET_W:3 * RET_W]
        kr_ref[pl.ds(pl.multiple_of(ci * c, c), c), :] = k.astype(BF16)
        kw = jnp.concatenate([(k * write_f).astype(BF16), (k * write_b).astype(BF16)], axis=1)
        d = lax.dot_general(kw, v, (((0,), (0,)), ((), ())), preferred_element_type=F32)
        dsf_ref[ci] = jnp.where(same_head, d[:RET_W], 0.0)
        dsb_ref[ci] = jnp.where(same_head, d[RET_W:], 0.0)
        return carry

    lax.fori_loop(0, nch, increments, 0, unroll=group)

    def scan(b, direction, ds_ref, out_ref, carry_decay):
        st_ref[...] = jnp.zeros((RET_W, RET_W), F32)
        if has_s0:
            for h in range(RET_HEADS):
                sl = slice(h * RET_DK, (h + 1) * RET_DK)
                st_ref[sl, sl] = s0_ref[b, direction, h]

        def step(n, carry):
            ci = b * nc + (n if direction == 0 else nc - 1 - n)
            out_ref[ci] = st_ref[...].astype(BF16)
            st_ref[...] = carry_decay * st_ref[...] + ds_ref[ci]
            return carry

        lax.fori_loop(0, nc, step, 0, unroll=min(nc, 4))
        if emit_state:
            for h in range(RET_HEADS):
                sl = slice(h * RET_DK, (h + 1) * RET_DK)
                sfin_ref[b, direction, h] = st_ref[sl, sl]

    for b in range(nb):
        scan(b, 0, dsf_ref, sfs_ref, carry_f)
        scan(b, 1, dsb_ref, sbs_ref, carry_b)

    def chunk_output(ci, b, r0):
        q = load_qk(0, b, r0)
        kb = kr_ref[pl.ds(pl.multiple_of(ci * c, c), c), :]
        v = ur_ref[b, pl.ds(r0, c), 2 * RET_W:3 * RET_W]
        att = lax.dot_general(_head_blocks(q.astype(BF16), RET_DK), kb,
                              (((1,), (1,)), ((), ())), preferred_element_type=F32)
        o4 = jnp.dot((att * decay).astype(BF16), v, preferred_element_type=F32)
        o = o4[(RET_HEADS - 1) * c:]
        for h in range(RET_HEADS - 2, -1, -1):
            o = jnp.where(out_head == h, o4[h * c:(h + 1) * c], o)
        o = o + jnp.dot((q * read_f).astype(BF16), sfs_ref[ci], preferred_element_type=F32)
        return o + jnp.dot((q * read_b).astype(BF16), sbs_ref[ci], preferred_element_type=F32)

    def outputs(it, carry):
        pos = [chunk_pos(it * group + u) for u in range(group)]
        o = jnp.concatenate([chunk_output(it * group + u, *pos[u]) for u in range(group)], axis=0)
        ms = jnp.dot((o * o).astype(BF16), group_mean, preferred_element_type=F32)
        y = o * lax.rsqrt(ms + EPS) * gn_ref[layer:layer + 1, :]
        for u, (b, r0) in enumerate(pos):
            z = ur_ref[b, pl.ds(r0, c), 3 * RET_W:4 * RET_W].astype(F32)
            yb_ref[b, pl.ds(r0, c), :] = (y[u * c:(u + 1) * c] * _silu(z)).astype(BF16)
        return carry

    lax.fori_loop(0, nch // group, outputs, 0)


def _retention(ur, layer, dec, gn, *, nb, emit_state, rope_tabs=None, state=None, state_carry=()):
    b, t, _ = ur.shape
    nch = nb * (t // RET_CHUNK)
    assert nch % RET_UNROLL == 0
    rope = rope_tabs is not None
    has_s0 = state is not None
    in_specs = [pl.BlockSpec(memory_space=pltpu.SMEM),
                pl.BlockSpec((nb, t, 4 * RET_W), lambda i: (i, 0, 0))]
    args = [dec, ur]
    if rope:
        in_specs += [_const_spec((t, LANES)), _const_spec((t, LANES))]
        args += list(rope_tabs)
    if has_s0:
        in_specs.append(pl.BlockSpec((nb, None, 2, RET_HEADS, RET_DK, RET_DK),
                                     lambda i: (i, layer, 0, 0, 0, 0)))
        args.append(state)
    in_specs.append(_const_spec((DEPTH, RET_W)))
    args.append(gn)
    carry_specs, carry, aliases = _carry_args(state_carry, len(args), 1)
    state_scratch = lambda dt: pltpu.VMEM((nch, RET_W, RET_W), dt)
    out_specs = [pl.BlockSpec((nb, t, RET_W), lambda i: (i, 0, 0))]
    out_shape = [jax.ShapeDtypeStruct((b, t, RET_W), BF16)]
    if emit_state:
        out_specs.append(pl.BlockSpec((nb, None, 2, RET_HEADS, RET_DK, RET_DK),
                                      lambda i: (i, layer, 0, 0, 0, 0)))
        out_shape.append(jax.ShapeDtypeStruct((b, DEPTH, 2, RET_HEADS, RET_DK, RET_DK), F32))
    return pl.pallas_call(
        functools.partial(_ret_kernel, nb=nb, t=t, layer=layer, rope=rope, has_s0=has_s0,
                          n_carry=len(carry), emit_state=emit_state),
        grid=(b // nb,),
        in_specs=in_specs + carry_specs,
        out_specs=out_specs,
        out_shape=out_shape,
        input_output_aliases=aliases,
        scratch_shapes=[pltpu.VMEM((nb * t, RET_W), BF16),
                        state_scratch(F32), state_scratch(F32), state_scratch(BF16), state_scratch(BF16),
                        pltpu.VMEM((RET_W, RET_W), F32)],
        compiler_params=_params(("parallel",)),
        name="retention",
    )(*args, *carry)


def _kv_variants(a, ones_block):
    lane_half = lax.broadcasted_iota(jnp.int32, a.shape, 1) // HEAD_DIM
    swapped = pltpu.roll(a, HEAD_DIM, axis=1)
    out = []
    for g in range(ATT_KV_HEADS):
        row = []
        for half in range(2):
            var = jnp.where(lane_half == half, a if half == g else swapped, 0.0).astype(BF16)
            if ones_block:
                ones = jnp.where(lane_half == half, 1.0, 0.0).astype(BF16)
                var = jnp.concatenate([var, ones], axis=1)
            row.append(var)
        out.append(row)
    return out


def _attend_group(q2, keys, values, masks, sink_ref, layer, g, z2):
    rows = q2.shape[0]
    upper = lax.broadcasted_iota(jnp.int32, (rows, 1), 0) < rows // 2
    acc = None
    sink_terms = []
    for half in range(2):
        sink = jnp.where(upper, sink_ref[layer, g, half], sink_ref[layer, g, 2 + half]) * LOG2E
        logits = []
        for kpart, mask in zip(keys[half], masks):
            l = lax.dot_general(q2, kpart, (((1,), (1,)), ((), ())), preferred_element_type=F32)
            logits.append(l if mask is None else jnp.where(mask, l, NEG))
        m = sink
        for l in logits:
            m = jnp.maximum(m, jnp.max(l, axis=-1, keepdims=True))
        for l, vpart in zip(logits, values[half]):
            pv = jnp.dot(jnp.exp2(l - m).astype(BF16), vpart, preferred_element_type=F32)
            acc = pv if acc is None else acc + pv
        sink_terms.append(jnp.exp2(sink - m))
    lane = lax.broadcasted_iota(jnp.int32, (rows, LANES), 1)
    den = acc[:, LANES:] + jnp.where(lane < HEAD_DIM, sink_terms[0], sink_terms[1])
    return (acc[:, :LANES] * (1.0 / den) * _silu(z2)).astype(BF16)


def _pair_rows(ref_slice, g):
    return jnp.concatenate([ref_slice(2 * g), ref_slice(2 * g + 1)], axis=0)


def _ctx_attn_kernel(sink_ref, ua_ref, yc_ref, *, nb, t, layer):
    for b in range(nb):
        kv = ua_ref[b, :, 2 * ATT_W:].astype(F32)
        kvar = _kv_variants(kv[:, :LANES] * (HEAD_DIM ** -0.5 * LOG2E), False)
        vvar = _kv_variants(kv[:, LANES:], True)
        for g in range(ATT_KV_HEADS):
            q2 = _pair_rows(lambda p: ua_ref[b, :, p * LANES:(p + 1) * LANES], g)
            z2 = _pair_rows(lambda p: ua_ref[b, :, ATT_W + p * LANES:ATT_W + (p + 1) * LANES], g)
            o = _attend_group(q2, [[kvar[g][0]], [kvar[g][1]]], [[vvar[g][0]], [vvar[g][1]]], [None],
                              sink_ref, layer, g, z2.astype(F32))
            yc_ref[b, :, 2 * g * LANES:(2 * g + 1) * LANES] = o[:t]
            yc_ref[b, :, (2 * g + 1) * LANES:(2 * g + 2) * LANES] = o[t:]


def _ctx_attention(ua, layer, sink):
    b, t, w = ua.shape
    nb = NB_CTX_ATTN
    return pl.pallas_call(
        functools.partial(_ctx_attn_kernel, nb=nb, t=t, layer=layer),
        grid=(b // nb,),
        in_specs=[pl.BlockSpec(memory_space=pltpu.SMEM),
                  pl.BlockSpec((nb, t, w), lambda i: (i, 0, 0))],
        out_specs=pl.BlockSpec((nb, t, ATT_W), lambda i: (i, 0, 0)),
        out_shape=jax.ShapeDtypeStruct((b, t, ATT_W), BF16),
        compiler_params=_params(("parallel",)),
        name="ctx_attention",
    )(sink, ua)


def _lat_attn_kernel(sink_ref, qz_ref, kv_ref, ck_ref, cv_ref, cos_ref, sin_ref, yc_ref,
                     kl_ref, vl_ref, kc_ref, vc_ref, *, t, tq, tiles, layer):
    j = pl.program_id(1)
    scale = HEAD_DIM ** -0.5 * LOG2E

    @pl.when(j == 0)
    def _prepare():
        kv = kv_ref[...].astype(F32)
        kvar = _kv_variants(_rope(kv[:, :LANES], cos_ref[...], sin_ref[...]) * scale, False)
        vvar = _kv_variants(kv[:, LANES:], True)
        cvar = _kv_variants(ck_ref[...] * scale, False)
        dvar = _kv_variants(cv_ref[...], True)
        for g in range(ATT_KV_HEADS):
            for half in range(2):
                i = 2 * g + half
                for ref, var in ((kl_ref, kvar), (vl_ref, vvar)):
                    pad = jnp.zeros((WINDOW, ref.shape[-1]), BF16)
                    ref[i, 0:WINDOW, :] = pad
                    ref[i, WINDOW:WINDOW + t, :] = var[g][half]
                    ref[i, WINDOW + t:, :] = pad
                kc_ref[i] = cvar[g][half]
                vc_ref[i] = dvar[g][half]

    nloc = tq + 2 * WINDOW
    rr = lax.broadcasted_iota(jnp.int32, (2 * tq, nloc), 0) & (tq - 1)
    ss = lax.broadcasted_iota(jnp.int32, (2 * tq, nloc), 1)
    for u in range(tiles):
        rows = slice(u * tq, (u + 1) * tq)
        first = (j * tiles + u) * tq
        r0 = pl.multiple_of(first, tq)
        band = ((ss - rr >= 0) & (ss - rr <= 2 * WINDOW)
                & (ss >= WINDOW - first) & (ss < t + WINDOW - first))
        cs = cos_ref[pl.ds(r0, tq), :]
        sn = sin_ref[pl.ds(r0, tq), :]
        for g in range(ATT_KV_HEADS):
            q2 = _pair_rows(
                lambda p: _rope(qz_ref[rows, p * LANES:(p + 1) * LANES].astype(F32), cs, sn).astype(BF16), g)
            z2 = _pair_rows(lambda p: qz_ref[rows, ATT_W + p * LANES:ATT_W + (p + 1) * LANES], g)
            keys = [[kl_ref[2 * g + half, pl.ds(r0, nloc), :], kc_ref[2 * g + half]] for half in range(2)]
            vals = [[vl_ref[2 * g + half, pl.ds(r0, nloc), :], vc_ref[2 * g + half]] for half in range(2)]
            o = _attend_group(q2, keys, vals, [band, None], sink_ref, layer, g, z2.astype(F32))
            yc_ref[rows, 2 * g * LANES:(2 * g + 1) * LANES] = o[:tq]
            yc_ref[rows, (2 * g + 1) * LANES:(2 * g + 2) * LANES] = o[tq:]


def _lat_attention(ua, cache_k, cache_v, layer, sink, rope_tabs):
    b, t, _ = ua.shape
    past = cache_k.shape[2]
    tq = TQ_LATENT
    tiles = LAT_TILES_PER_STEP
    nvar = 2 * ATT_KV_HEADS
    cache_spec = pl.BlockSpec((None, None, past, ATT_KV_W), lambda i, j: (i, layer, 0, 0))
    return pl.pallas_call(
        functools.partial(_lat_attn_kernel, t=t, tq=tq, tiles=tiles, layer=layer),
        grid=(b, t // (tiles * tq)),
        in_specs=[pl.BlockSpec(memory_space=pltpu.SMEM),
                  pl.BlockSpec((None, tiles * tq, 2 * ATT_W), lambda i, j: (i, j, 0)),
                  pl.BlockSpec((None, t, 2 * ATT_KV_W), lambda i, j: (i, 0, 2 * ATT_W // (2 * ATT_KV_W))),
                  cache_spec, cache_spec,
                  _const_spec((t, LANES)), _const_spec((t, LANES))],
        out_specs=pl.BlockSpec((None, tiles * tq, ATT_W), lambda i, j: (i, j, 0)),
        out_shape=jax.ShapeDtypeStruct((b, t, ATT_W), BF16),
        scratch_shapes=[pltpu.VMEM((nvar, t + 2 * WINDOW, LANES), BF16),
                        pltpu.VMEM((nvar, t + 2 * WINDOW, 2 * LANES), BF16),
                        pltpu.VMEM((nvar, past, LANES), BF16),
                        pltpu.VMEM((nvar, past, 2 * LANES), BF16)],
        compiler_params=_params(("parallel", "arbitrary")),
        name="lat_attention",
    )(sink, ua, ua, cache_k, cache_v, *rope_tabs)


def _outproj_kernel(x_ref, ya_ref, yb_ref, yc_ref, mod_ref, gpre_ref, gpost_ref,
                    wg_ref, wgt_ref, wa_ref, wb_ref, wc_ref, wo_ref, o_ref, *, layer, tm, row0, rows_per_mod):
    for r in range(tm // SUB_ROWS):
        rows = slice(r * SUB_ROWS, (r + 1) * SUB_ROWS)
        x = x_ref[rows, :]
        h, row = _modulated_norm(x, gpre_ref[layer:layer + 1, :], mod_ref, tm, row0, rows_per_mod)

        def gate_logits(c):
            lo, hi = c * D_MODEL, min((c + 1) * D_MODEL, C_MG)
            mg = jnp.dot(h, wg_ref[:, lo:hi].astype(BF16), preferred_element_type=F32)
            if hi - lo < D_MODEL:
                tail = jnp.dot(h, wgt_ref[...].astype(BF16), preferred_element_type=F32)
                mg = jnp.concatenate([mg, tail], axis=1)
            return mg

        merged = None
        for c, (y_ref, w_ref) in enumerate(((ya_ref, wa_ref), (yb_ref, wb_ref), (yc_ref, wc_ref))):
            term = _sigmoid(gate_logits(c)) * jnp.dot(y_ref[rows, :], w_ref[...].astype(BF16),
                                                      preferred_element_type=F32)
            merged = term if merged is None else merged + term
        out = jnp.dot(merged.astype(BF16), wo_ref[...].astype(BF16), preferred_element_type=F32)
        ms = jnp.mean(out * out, axis=-1, keepdims=True)
        normed = out * lax.rsqrt(ms + EPS) * gpost_ref[layer:layer + 1, :]
        gate = mod_ref[pl.ds(row, 1), 2 * D_MODEL:3 * D_MODEL]
        o_ref[rows, :] = x + gate * normed


def _outproj(x2, ya, yb, yc, layer, mod, g_pre, g_post, w_in, wa, wb, wc, wo, *, row0, rows_per_mod):
    n = x2.shape[0]
    tm = TM_OUTPROJ
    row_spec = lambda w: pl.BlockSpec((tm, w), lambda i: (i, 0))
    return pl.pallas_call(
        functools.partial(_outproj_kernel, layer=layer, tm=tm, row0=row0, rows_per_mod=rows_per_mod),
        grid=(n // tm,),
        in_specs=[row_spec(D_MODEL), row_spec(FOURIER_W), row_spec(RET_W), row_spec(ATT_W),
                  _layer_spec((MOD_ROWS, 3 * D_MODEL), layer),
                  _const_spec((DEPTH, D_MODEL)), _const_spec((DEPTH, D_MODEL)),
                  _layer_spec((D_MODEL, C_MG), layer, col_block=1),
                  _layer_spec((D_MODEL, C_END - 2 * C_MG), layer, col_block=2 * C_MG // (C_END - 2 * C_MG)),
                  _layer_spec((FOURIER_W, D_MODEL), layer), _layer_spec((RET_W, D_MODEL), layer),
                  _layer_spec((ATT_W, D_MODEL), layer), _layer_spec((D_MODEL, D_MODEL), layer)],
        out_specs=row_spec(D_MODEL),
        out_shape=jax.ShapeDtypeStruct((n, D_MODEL), F32),
        compiler_params=_params(("parallel",)),
        name="outproj",
    )(x2, ya, yb, yc, mod, g_pre, g_post, w_in, w_in, wa, wb, wc, wo)


def _layer(x, layer, mod, p, *, latent, dft_chan, dft_pos, rope_tabs=None, cache_k=None, cache_v=None,
           state=None, kv_carry=(), state_carry=()):
    b, t, _ = x.shape
    n = b * t
    row0, rows_per_mod = (1, t) if latent else (0, n)
    x2 = x.reshape(n, D_MODEL)
    xcs, fz, ur, ua, *kv32 = _inproj(x2, layer, mod, p["g_pre"], p["w_in"], dft_chan,
                                     seq=t, row0=row0, rows_per_mod=rows_per_mod, emit_kv=not latent,
                                     kv_carry=kv_carry)
    ya = _fourier(xcs.reshape(b, t, -1), fz.reshape(b, t, -1), dft_pos, p["w_four"], layer,
                  bg=b if latent else BG_CTX_FOURIER, tq=min(t, TQ_FOURIER))
    ur3 = ur.reshape(b, t, -1)
    ua3 = ua.reshape(b, t, -1)
    if latent:
        (yb,), s_fin = _retention(ur3, layer, p["dec"], p["gn"], nb=1, emit_state=False,
                                  rope_tabs=rope_tabs, state=state), None
        yc = _lat_attention(ua3, cache_k, cache_v, layer, p["sink"], rope_tabs)
    else:
        yb, s_fin = _retention(ur3, layer, p["dec"], p["gn"], nb=NB_CTX_RET, emit_state=True,
                               state_carry=state_carry)
        yc = _ctx_attention(ua3, layer, p["sink"])
    out = _outproj(x2, ya.reshape(n, -1), yb.reshape(n, -1), yc.reshape(n, -1), layer, mod,
                   p["g_pre"], p["g_post"], p["w_in"], p["w_pa"], p["w_pb"], p["w_pc"], p["w_out"],
                   row0=row0, rows_per_mod=rows_per_mod)
    return out.reshape(b, t, D_MODEL), kv32, s_fin


def kernel(x_prompt, x_sample, cache_k, cache_v, state_ret, c, c_ctx, w_mod, b_mod, g_pre, g_post, w_in,
           w_four, ret_decay, ret_gn, attn_sink, w_branch_a, w_branch_b, w_branch_c, w_out):
    batch, seq, _ = x_prompt.shape
    dec_batch, dec_seq, _ = x_sample.shape
    past = cache_k.shape[2]
    assert 1 + dec_batch <= MOD_ROWS

    cv = jnp.zeros((MOD_ROWS, D_MODEL), F32).at[0].set(c_ctx).at[1:1 + dec_batch].set(c)
    mod = _modulation(cv, w_mod, b_mod)

    dft_chan, dft_ctx = _dft_tables(seq)
    _, dft_lat = _dft_tables(dec_seq)
    rope_tabs = _rope_tables(dec_seq)
    ck = cache_k.reshape(dec_batch, DEPTH, past, ATT_KV_W)
    cvv = cache_v.reshape(dec_batch, DEPTH, past, ATT_KV_W)

    p = dict(
        g_pre=g_pre, g_post=g_post,
        w_in=w_in, w_four=w_four, w_pa=w_branch_a, w_pb=w_branch_b, w_pc=w_branch_c, w_out=w_out,
        dec=ret_decay, gn=ret_gn, sink=attn_sink)

    xp = x_prompt
    kv_all, s_all = (), ()
    for l in range(DEPTH):
        xp, kv_all, s_all = _layer(xp, l, mod, p, latent=False, dft_chan=dft_chan, dft_pos=dft_ctx,
                                   kv_carry=tuple(kv_all), state_carry=s_all)
        s_all = (s_all,)

    xs = x_sample
    for l in range(DEPTH):
        xs, _, _ = _layer(xs, l, mod, p, latent=True, dft_chan=dft_chan, dft_pos=dft_lat,
                             rope_tabs=rope_tabs, cache_k=ck, cache_v=cvv, state=state_ret)

    new_k, new_v = (jnp.transpose(a, (0, 1, 4, 2, 3)) for a in kv_all)
    return (xp, xs, new_k, new_v, s_all[0])
```

```python
import functools
import math

import numpy as np
import jax
import jax.numpy as jnp
from jax import lax
from jax.experimental import pallas as pl
from jax.experimental.pallas import tpu as pltpu

F32 = jnp.float32
BF16 = jnp.bfloat16

D_MODEL = 1024
DEPTH = 2
GRID_W = 64
HEAD_DIM = 64
FOURIER_GROUPS = 4
FOURIER_GROUP_W = 64
FOURIER_W = FOURIER_GROUPS * FOURIER_GROUP_W
RET_HEADS = 4
RET_DK = 64
RET_W = RET_HEADS * RET_DK
RET_CHUNK = 128
ATT_Q_HEADS = 8
ATT_KV_HEADS = 2
ATT_W = ATT_Q_HEADS * HEAD_DIM
ATT_KV_W = ATT_KV_HEADS * HEAD_DIM
WINDOW = 128
ROPE_BASE = 10000.0
EPS = 1e-6
MOD_ROWS = 8
LANES = 128
NEG = -1e30
LOG2E = math.log2(math.e)
VMEM_LIMIT = 56 * 1024 * 1024

C_FX, C_RQ, C_RZ_END = 0, 512, 1536
C_AQ, C_AK, C_AV, C_AZ, C_MG, C_END = 1536, 2048, 2176, 2304, 2816, 5888

TM_INPROJ = 1024
TM_OUTPROJ = 1024
SUB_ROWS = 512
TQ_LATENT = 256
LAT_TILES_PER_STEP = 4
TQ_FOURIER = 512
NB_CTX_RET = 8
NB_CTX_ATTN = 4
BG_CTX_FOURIER = 8
RET_UNROLL = 4


def _sigmoid(x):
    return 0.5 * jnp.tanh(0.5 * x) + 0.5


def _silu(x):
    return x * _sigmoid(x)


def _params(sem):
    return pltpu.CompilerParams(dimension_semantics=sem, vmem_limit_bytes=VMEM_LIMIT)


def _const_spec(shape):
    nd = len(shape)
    return pl.BlockSpec(shape, lambda *_: (0,) * nd, pipeline_mode=pl.Buffered(1))


def _layer_spec(shape, layer, col_block=0):
    idx = (layer,) + (0,) * (len(shape) - 1) + (col_block,)
    return pl.BlockSpec((None,) + tuple(shape), lambda *_: idx, pipeline_mode=pl.Buffered(1))


def _carry_args(carry, first_input, first_output):
    specs = [pl.BlockSpec(memory_space=pl.ANY)] * len(carry)
    aliases = {first_input + k: first_output + k for k in range(len(carry))}
    return specs, list(carry), aliases


def _slab_spec(lead, rest, layer, whole):
    zeros = (0,) * len(rest)
    if whole:
        return pl.BlockSpec((lead, DEPTH) + tuple(rest), lambda i: (i, 0) + zeros)
    return pl.BlockSpec((lead, None) + tuple(rest), lambda i: (i, layer) + zeros)


def _store_slab(ref, lead_idx, layer, whole, value, rest_idx=()):
    if not whole:
        ref[(lead_idx,) + tuple(rest_idx)] = value
        return
    for l in range(DEPTH):
        ref[(lead_idx, l) + tuple(rest_idx)] = value if l == layer else jnp.zeros_like(value)


def _dft_tables(t):
    c = np.arange(FOURIER_GROUP_W)
    ang = 2.0 * np.pi * ((c[:, None] * c[None, :]) % FOURIER_GROUP_W) / FOURIER_GROUP_W
    eye = np.eye(FOURIER_GROUPS)
    s64 = FOURIER_GROUP_W ** -0.5
    chan = np.concatenate([np.kron(eye, np.cos(ang) * s64), np.kron(eye, np.sin(ang) * s64)], axis=1)
    p = np.arange(t)
    angt = 2.0 * np.pi * ((p[:, None] * p[None, :]) % t) / t
    pos = np.concatenate([np.cos(angt), -np.sin(angt)], axis=1) * (t ** -0.5)
    return jnp.asarray(chan, F32).astype(BF16), jnp.asarray(pos, F32).astype(BF16)


def _rope_tables(t):
    quarter = HEAD_DIM // 4
    lane = np.arange(LANES) % HEAD_DIM
    inv = ROPE_BASE ** (-(lane % quarter).astype(np.float64) / quarter)
    n = np.arange(t)
    pos = np.where(lane[None, :] < HEAD_DIM // 2, (n // GRID_W)[:, None], (n % GRID_W)[:, None])
    ang = pos.astype(np.float64) * inv[None, :]
    sign = np.where((lane % (2 * quarter)) < quarter, -1.0, 1.0)
    return jnp.asarray(np.cos(ang), F32), jnp.asarray(np.sin(ang) * sign[None, :], F32)


def _rope(x, cos, sin):
    lane = lax.broadcasted_iota(jnp.int32, x.shape, 1)
    first = (lane & 31) < 16
    partner = jnp.where(first, pltpu.roll(x, LANES - 16, axis=1), pltpu.roll(x, 16, axis=1))
    return x * cos + partner * sin


def _split_bf16(x):
    hi = x.astype(BF16)
    return hi, (x - hi.astype(F32)).astype(BF16)


def _mod_kernel(cv_ref, w_ref, b_ref, o_ref):
    a_hi, a_lo = _split_bf16(_silu(cv_ref[...]))
    w_hi, w_lo = _split_bf16(w_ref[...])
    dot = functools.partial(jnp.dot, preferred_element_type=F32)
    bias = b_ref[pl.ds(pl.program_id(0), 1), :]
    o_ref[...] = dot(a_hi, w_hi) + (dot(a_lo, w_hi) + dot(a_hi, w_lo)) + bias


def _modulation(cv, w_mod, b_mod):
    tn = 1024
    return pl.pallas_call(
        _mod_kernel,
        grid=(DEPTH, 3 * D_MODEL // tn),
        in_specs=[pl.BlockSpec((MOD_ROWS, D_MODEL), lambda l, j: (0, 0)),
                  pl.BlockSpec((None, D_MODEL, tn), lambda l, j: (l, 0, j)),
                  pl.BlockSpec((DEPTH, tn), lambda l, j: (0, j))],
        out_specs=pl.BlockSpec((None, MOD_ROWS, tn), lambda l, j: (l, 0, j)),
        out_shape=jax.ShapeDtypeStruct((DEPTH, MOD_ROWS, 3 * D_MODEL), F32),
        compiler_params=_params(("parallel", "parallel")),
        name="modulation",
    )(cv, w_mod, b_mod)


def _modulated_norm(x, g, mod_ref, tm, row0, rows_per_mod):
    row = row0 + (pl.program_id(0) * tm) // rows_per_mod
    ms = jnp.mean(x * x, axis=-1, keepdims=True)
    y = x * lax.rsqrt(ms + EPS) * g
    shift = mod_ref[pl.ds(row, 1), 0:D_MODEL]
    scale = mod_ref[pl.ds(row, 1), D_MODEL:2 * D_MODEL]
    return (y * (1.0 + scale) + shift).astype(BF16), row


def _inproj_kernel(x_ref, mod_ref, g_ref, w_ref, dft_ref, *refs, layer, tm, seq, row0, rows_per_mod,
                   n_carry):
    xcs_ref, fz_ref, ur_ref, ua_ref, *kv_refs = refs[n_carry:]
    for r in range(tm // SUB_ROWS):
        rows = slice(r * SUB_ROWS, (r + 1) * SUB_ROWS)
        h, _ = _modulated_norm(x_ref[rows, :], g_ref[layer:layer + 1, :], mod_ref, tm, row0, rows_per_mod)

        def mm(c0, c1):
            return jnp.dot(h, w_ref[:, c0:c1].astype(BF16), preferred_element_type=F32)

        f = mm(C_FX, C_RQ)
        xcs_ref[rows, :] = jnp.dot(f[:, :FOURIER_W].astype(BF16), dft_ref[...],
                                   preferred_element_type=F32).astype(BF16)
        fz_ref[rows, :] = f[:, FOURIER_W:].astype(BF16)
        ur_ref[rows, :] = mm(C_RQ, C_RZ_END).astype(BF16)
        ua_ref[rows, 0:ATT_W] = mm(C_AQ, C_AK).astype(BF16)
        ua_ref[rows, ATT_W:2 * ATT_W] = mm(C_AZ, C_MG).astype(BF16)
        kv = mm(C_AK, C_AZ)
        if kv_refs:
            for s in range(SUB_ROWS // seq):
                blk = kv[s * seq:(s + 1) * seq]
                bi = (r * SUB_ROWS) // seq + s
                for ref, cols in zip(kv_refs, (blk[:, :ATT_KV_W], blk[:, ATT_KV_W:])):
                    _store_slab(ref, bi, layer, n_carry == 0,
                                cols.T.reshape(ATT_KV_HEADS, HEAD_DIM, seq))
        ua_ref[rows, 2 * ATT_W:] = kv.astype(BF16)


def _inproj(x2, layer, mod, g_pre, w_in, dft, *, seq, row0, rows_per_mod, emit_kv, kv_carry=()):
    n = x2.shape[0]
    tm = TM_INPROJ
    row_spec = lambda w: pl.BlockSpec((tm, w), lambda i: (i, 0))
    widths = (2 * FOURIER_W, FOURIER_W, 4 * RET_W, 2 * ATT_W + 2 * ATT_KV_W)
    out_shape = [jax.ShapeDtypeStruct((n, w), BF16) for w in widths]
    out_specs = [row_spec(w) for w in widths]
    if emit_kv:
        assert SUB_ROWS % seq == 0
        out_shape += [jax.ShapeDtypeStruct((n // seq, DEPTH, ATT_KV_HEADS, HEAD_DIM, seq), F32)] * 2
        out_specs += [_slab_spec(tm // seq, (ATT_KV_HEADS, HEAD_DIM, seq), layer, not kv_carry)] * 2
    args = [x2, mod, g_pre, w_in, dft]
    carry_specs, carry, aliases = _carry_args(kv_carry, len(args), len(widths))
    return pl.pallas_call(
        functools.partial(_inproj_kernel, layer=layer, tm=tm, seq=seq, row0=row0, rows_per_mod=rows_per_mod,
                          n_carry=len(carry)),
        grid=(n // tm,),
        in_specs=[row_spec(D_MODEL),
                  _layer_spec((MOD_ROWS, 3 * D_MODEL), layer),
                  _const_spec((DEPTH, D_MODEL)),
                  _layer_spec((D_MODEL, C_MG), layer),
                  _const_spec((FOURIER_W, 2 * FOURIER_W))] + carry_specs,
        out_specs=out_specs,
        out_shape=out_shape,
        input_output_aliases=aliases,
        compiler_params=_params(("parallel",)),
        name="inproj",
    )(*args, *carry)


def _fourier_kernel(ct_ref, xcs_ref, fz_ref, w_ref, ya_ref, xcat_ref, *, bg, t):
    @pl.when(pl.program_id(1) == 0)
    def _gather():
        for b in range(bg):
            cols = slice(b * FOURIER_W, (b + 1) * FOURIER_W)
            xcat_ref[0:t, cols] = xcs_ref[b, :, 0:FOURIER_W]
            xcat_ref[t:2 * t, cols] = xcs_ref[b, :, FOURIER_W:]

    yr = jnp.dot(ct_ref[...], xcat_ref[...], preferred_element_type=F32).astype(BF16)
    w = w_ref[...].astype(BF16)
    for b in range(bg):
        ya = jnp.dot(yr[:, b * FOURIER_W:(b + 1) * FOURIER_W], w, preferred_element_type=F32)
        ya_ref[b] = (ya * _silu(fz_ref[b].astype(F32))).astype(BF16)


def _fourier(xcs, fz, ct, w_four, layer, *, bg, tq):
    b, t, _ = xcs.shape
    return pl.pallas_call(
        functools.partial(_fourier_kernel, bg=bg, t=t),
        grid=(b // bg, t // tq),
        in_specs=[pl.BlockSpec((tq, 2 * t), lambda i, j: (j, 0)),
                  pl.BlockSpec((bg, t, 2 * FOURIER_W), lambda i, j: (i, 0, 0)),
                  pl.BlockSpec((bg, tq, FOURIER_W), lambda i, j: (i, j, 0)),
                  _layer_spec((FOURIER_W, FOURIER_W), layer)],
        out_specs=pl.BlockSpec((bg, tq, FOURIER_W), lambda i, j: (i, j, 0)),
        out_shape=jax.ShapeDtypeStruct((b, t, FOURIER_W), BF16),
        scratch_shapes=[pltpu.VMEM((2 * t, bg * FOURIER_W), BF16)],
        compiler_params=_params(("parallel", "arbitrary")),
        name="fourier",
    )(ct, xcs, fz, w_four)


def _log_sigmoid(x):
    return jnp.minimum(x, 0.0) - jnp.log(1.0 + jnp.exp(-jnp.abs(x)))


def _head_blocks(a, width):
    lane = lax.broadcasted_iota(jnp.int32, a.shape, 1)
    zero = jnp.zeros_like(a)
    return jnp.concatenate(
        [jnp.where((lane >= h * width) & (lane < (h + 1) * width), a, zero) for h in range(RET_HEADS)],
        axis=0)


def _per_head(dec_ref, layer, direction, head_of):
    out = jnp.full(head_of.shape, dec_ref[layer, direction, RET_HEADS - 1], F32)
    for h in range(RET_HEADS - 2, -1, -1):
        out = jnp.where(head_of == h, dec_ref[layer, direction, h], out)
    return out


def _ret_kernel(*refs, nb, t, layer, rope, has_s0, n_carry, emit_state):
    refs = list(refs)
    dec_ref = refs.pop(0)
    ur_ref = refs.pop(0)
    cos_ref = refs.pop(0) if rope else None
    sin_ref = refs.pop(0) if rope else None
    s0_ref = refs.pop(0) if has_s0 else None
    gn_ref = refs.pop(0)
    del refs[:n_carry]
    yb_ref = refs.pop(0)
    sfin_ref = refs.pop(0) if emit_state else None
    (kr_ref, dsf_ref, dsb_ref, sfs_ref, sbs_ref, st_ref) = refs
    c = RET_CHUNK
    nc = t // c
    nch = nb * nc
    group = RET_UNROLL

    lane_head = lax.broadcasted_iota(jnp.int32, (1, RET_W), 1) // RET_DK
    lgf = _log_sigmoid(_per_head(dec_ref, layer, 0, lane_head))
    lgb = _log_sigmoid(_per_head(dec_ref, layer, 1, lane_head))
    ri = lax.broadcasted_iota(jnp.int32, (c, RET_W), 0).astype(F32)
    read_f = jnp.exp((ri + 1.0) * lgf)
    read_b = jnp.exp((c - ri) * lgb)
    write_f = jnp.exp((c - 1.0 - ri) * lgf)
    write_b = jnp.exp(ri * lgb)
    carry_f = jnp.exp(c * lgf)
    carry_b = jnp.exp(c * lgb)
    ii = lax.broadcasted_iota(jnp.int32, (RET_HEADS * c, c), 0)
    jj = lax.broadcasted_iota(jnp.int32, (RET_HEADS * c, c), 1)
    row_head = ii // c
    diff = ((ii & (c - 1)) - jj).astype(F32)
    decay = (jnp.where(diff >= 0, jnp.exp(jnp.maximum(diff, 0.0)
                                          * _log_sigmoid(_per_head(dec_ref, layer, 0, row_head))), 0.0)
             + jnp.where(diff <= 0, jnp.exp(jnp.maximum(-diff, 0.0)
                                            * _log_sigmoid(_per_head(dec_ref, layer, 1, row_head))), 0.0))
    r2 = lax.broadcasted_iota(jnp.int32, (RET_W, RET_W), 0)
    c2 = lax.broadcasted_iota(jnp.int32, (RET_W, RET_W), 1)
    same_head = (r2 // RET_DK) == (c2 // RET_DK)
    group_mean = jnp.where(same_head, 1.0 / RET_DK, 0.0).astype(BF16)
    out_head = lax.broadcasted_iota(jnp.int32, (c, RET_W), 1) // RET_DK

    def chunk_pos(ci):
        return ci // nc, pl.multiple_of((ci % nc) * c, c)

    def load_qk(col, b, r0):
        a = ur_ref[b, pl.ds(r0, c), col:col + RET_W].astype(F32)
        if rope:
            cs = cos_ref[pl.ds(r0, c), :]
            sn = sin_ref[pl.ds(r0, c), :]
            a = jnp.concatenate([_rope(a[:, :LANES], cs, sn), _rope(a[:, LANES:], cs, sn)], axis=1)
        return a

    def increments(ci, carry):
        b, r0 = chunk_pos(ci)
        k = load_qk(RET_W, b, r0) * (RET_DK ** -0.5)
        v = ur_ref[b, pl.ds(r0, c), 2 * RET_W:3 * RET_W]
        kr_ref[pl.ds(pl.multiple_of(ci * c, c), c), :] = k.astype(BF16)
        kw = jnp.concatenate([(k * write_f).astype(BF16), (k * write_b).astype(BF16)], axis=1)
        d = lax.dot_general(kw, v, (((0,), (0,)), ((), ())), preferred_element_type=F32)
        dsf_ref[ci] = jnp.where(same_head, d[:RET_W], 0.0)
        dsb_ref[ci] = jnp.where(same_head, d[RET_W:], 0.0)
        return carry

    lax.fori_loop(0, nch, increments, 0, unroll=group)

    def scan(b, direction, ds_ref, out_ref, carry_decay):
        st_ref[...] = jnp.zeros((RET_W, RET_W), F32)
        if has_s0:
            for h in range(RET_HEADS):
                sl = slice(h * RET_DK, (h + 1) * RET_DK)
                st_ref[sl, sl] = s0_ref[b, direction, h]

        def step(n, carry):
            ci = b * nc + (n if direction == 0 else nc - 1 - n)
            out_ref[ci] = st_ref[...].astype(BF16)
            st_ref[...] = carry_decay * st_ref[...] + ds_ref[ci]
            return carry

        lax.fori_loop(0, nc, step, 0, unroll=min(nc, 4))
        if emit_state:
            for h in range(RET_HEADS):
                sl = slice(h * RET_DK, (h + 1) * RET_DK)
                _store_slab(sfin_ref, b, layer, n_carry == 0, st_ref[sl, sl], (direction, h))

    for b in range(nb):
        scan(b, 0, dsf_ref, sfs_ref, carry_f)
        scan(b, 1, dsb_ref, sbs_ref, carry_b)

    def chunk_output(ci, b, r0):
        q = load_qk(0, b, r0)
        kb = kr_ref[pl.ds(pl.multiple_of(ci * c, c), c), :]
        v = ur_ref[b, pl.ds(r0, c), 2 * RET_W:3 * RET_W]
        att = lax.dot_general(_head_blocks(q.astype(BF16), RET_DK), kb,
                              (((1,), (1,)), ((), ())), preferred_element_type=F32)
        o4 = jnp.dot((att * decay).astype(BF16), v, preferred_element_type=F32)
        o = o4[(RET_HEADS - 1) * c:]
        for h in range(RET_HEADS - 2, -1, -1):
            o = jnp.where(out_head == h, o4[h * c:(h + 1) * c], o)
        o = o + jnp.dot((q * read_f).astype(BF16), sfs_ref[ci], preferred_element_type=F32)
        return o + jnp.dot((q * read_b).astype(BF16), sbs_ref[ci], preferred_element_type=F32)

    def outputs(it, carry):
        pos = [chunk_pos(it * group + u) for u in range(group)]
        o = jnp.concatenate([chunk_output(it * group + u, *pos[u]) for u in range(group)], axis=0)
        ms = jnp.dot((o * o).astype(BF16), group_mean, preferred_element_type=F32)
        y = o * lax.rsqrt(ms + EPS) * gn_ref[layer:layer + 1, :]
        for u, (b, r0) in enumerate(pos):
            z = ur_ref[b, pl.ds(r0, c), 3 * RET_W:4 * RET_W].astype(F32)
            yb_ref[b, pl.ds(r0, c), :] = (y[u * c:(u + 1) * c] * _silu(z)).astype(BF16)
        return carry

    lax.fori_loop(0, nch // group, outputs, 0)


def _retention(ur, layer, dec, gn, *, nb, emit_state, rope_tabs=None, state=None, state_carry=()):
    b, t, _ = ur.shape
    nch = nb * (t // RET_CHUNK)
    assert nch % RET_UNROLL == 0
    rope = rope_tabs is not None
    has_s0 = state is not None
    in_specs = [pl.BlockSpec(memory_space=pltpu.SMEM),
                pl.BlockSpec((nb, t, 4 * RET_W), lambda i: (i, 0, 0))]
    args = [dec, ur]
    if rope:
        in_specs += [_const_spec((t, LANES)), _const_spec((t, LANES))]
        args += list(rope_tabs)
    if has_s0:
        in_specs.append(pl.BlockSpec((nb, None, 2, RET_HEADS, RET_DK, RET_DK),
                                     lambda i: (i, layer, 0, 0, 0, 0)))
        args.append(state)
    in_specs.append(_const_spec((DEPTH, RET_W)))
    args.append(gn)
    carry_specs, carry, aliases = _carry_args(state_carry, len(args), 1)
    state_scratch = lambda dt: pltpu.VMEM((nch, RET_W, RET_W), dt)
    out_specs = [pl.BlockSpec((nb, t, RET_W), lambda i: (i, 0, 0))]
    out_shape = [jax.ShapeDtypeStruct((b, t, RET_W), BF16)]
    if emit_state:
        out_specs.append(_slab_spec(nb, (2, RET_HEADS, RET_DK, RET_DK), layer, not state_carry))
        out_shape.append(jax.ShapeDtypeStruct((b, DEPTH, 2, RET_HEADS, RET_DK, RET_DK), F32))
    return pl.pallas_call(
        functools.partial(_ret_kernel, nb=nb, t=t, layer=layer, rope=rope, has_s0=has_s0,
                          n_carry=len(carry), emit_state=emit_state),
        grid=(b // nb,),
        in_specs=in_specs + carry_specs,
        out_specs=out_specs,
        out_shape=out_shape,
        input_output_aliases=aliases,
        scratch_shapes=[pltpu.VMEM((nb * t, RET_W), BF16),
                        state_scratch(F32), state_scratch(F32), state_scratch(BF16), state_scratch(BF16),
                        pltpu.VMEM((RET_W, RET_W), F32)],
        compiler_params=_params(("parallel",)),
        name="retention",
    )(*args, *carry)


def _kv_variants(a, ones_block):
    lane_half = lax.broadcasted_iota(jnp.int32, a.shape, 1) // HEAD_DIM
    swapped = pltpu.roll(a, HEAD_DIM, axis=1)
    out = []
    for g in range(ATT_KV_HEADS):
        row = []
        for half in range(2):
            var = jnp.where(lane_half == half, a if half == g else swapped, 0.0).astype(BF16)
            if ones_block:
                ones = jnp.where(lane_half == half, 1.0, 0.0).astype(BF16)
                var = jnp.concatenate([var, ones], axis=1)
            row.append(var)
        out.append(row)
    return out


def _attend_group(q2, keys, values, masks, sink_ref, layer, g, z2):
    rows = q2.shape[0]
    upper = lax.broadcasted_iota(jnp.int32, (rows, 1), 0) < rows // 2
    acc = None
    sink_terms = []
    for half in range(2):
        sink = jnp.where(upper, sink_ref[layer, g, half], sink_ref[layer, g, 2 + half]) * LOG2E
        logits = []
        for kpart, mask in zip(keys[half], masks):
            l = lax.dot_general(q2, kpart, (((1,), (1,)), ((), ())), preferred_element_type=F32)
            logits.append(l if mask is None else jnp.where(mask, l, NEG))
        m = sink
        for l in logits:
            m = jnp.maximum(m, jnp.max(l, axis=-1, keepdims=True))
        for l, vpart in zip(logits, values[half]):
            pv = jnp.dot(jnp.exp2(l - m).astype(BF16), vpart, preferred_element_type=F32)
            acc = pv if acc is None else acc + pv
        sink_terms.append(jnp.exp2(sink - m))
    lane = lax.broadcasted_iota(jnp.int32, (rows, LANES), 1)
    den = acc[:, LANES:] + jnp.where(lane < HEAD_DIM, sink_terms[0], sink_terms[1])
    return (acc[:, :LANES] * (1.0 / den) * _silu(z2)).astype(BF16)


def _pair_rows(ref_slice, g):
    return jnp.concatenate([ref_slice(2 * g), ref_slice(2 * g + 1)], axis=0)


def _ctx_attn_kernel(sink_ref, ua_ref, yc_ref, *, nb, t, layer):
    for b in range(nb):
        kv = ua_ref[b, :, 2 * ATT_W:].astype(F32)
        kvar = _kv_variants(kv[:, :LANES] * (HEAD_DIM ** -0.5 * LOG2E), False)
        vvar = _kv_variants(kv[:, LANES:], True)
        for g in range(ATT_KV_HEADS):
            q2 = _pair_rows(lambda p: ua_ref[b, :, p * LANES:(p + 1) * LANES], g)
            z2 = _pair_rows(lambda p: ua_ref[b, :, ATT_W + p * LANES:ATT_W + (p + 1) * LANES], g)
            o = _attend_group(q2, [[kvar[g][0]], [kvar[g][1]]], [[vvar[g][0]], [vvar[g][1]]], [None],
                              sink_ref, layer, g, z2.astype(F32))
            yc_ref[b, :, 2 * g * LANES:(2 * g + 1) * LANES] = o[:t]
            yc_ref[b, :, (2 * g + 1) * LANES:(2 * g + 2) * LANES] = o[t:]


def _ctx_attention(ua, layer, sink):
    b, t, w = ua.shape
    nb = NB_CTX_ATTN
    return pl.pallas_call(
        functools.partial(_ctx_attn_kernel, nb=nb, t=t, layer=layer),
        grid=(b // nb,),
        in_specs=[pl.BlockSpec(memory_space=pltpu.SMEM),
                  pl.BlockSpec((nb, t, w), lambda i: (i, 0, 0))],
        out_specs=pl.BlockSpec((nb, t, ATT_W), lambda i: (i, 0, 0)),
        out_shape=jax.ShapeDtypeStruct((b, t, ATT_W), BF16),
        compiler_params=_params(("parallel",)),
        name="ctx_attention",
    )(sink, ua)


def _lat_attn_kernel(sink_ref, qz_ref, kv_ref, ck_ref, cv_ref, cos_ref, sin_ref, yc_ref,
                     kl_ref, vl_ref, kc_ref, vc_ref, *, t, tq, tiles, layer):
    j = pl.program_id(1)
    scale = HEAD_DIM ** -0.5 * LOG2E

    @pl.when(j == 0)
    def _prepare():
        kv = kv_ref[...].astype(F32)
        kvar = _kv_variants(_rope(kv[:, :LANES], cos_ref[...], sin_ref[...]) * scale, False)
        vvar = _kv_variants(kv[:, LANES:], True)
        cvar = _kv_variants(ck_ref[...] * scale, False)
        dvar = _kv_variants(cv_ref[...], True)
        for g in range(ATT_KV_HEADS):
            for half in range(2):
                i = 2 * g + half
                for ref, var in ((kl_ref, kvar), (vl_ref, vvar)):
                    pad = jnp.zeros((WINDOW, ref.shape[-1]), BF16)
                    ref[i, 0:WINDOW, :] = pad
                    ref[i, WINDOW:WINDOW + t, :] = var[g][half]
                    ref[i, WINDOW + t:, :] = pad
                kc_ref[i] = cvar[g][half]
                vc_ref[i] = dvar[g][half]

    nloc = tq + 2 * WINDOW
    rr = lax.broadcasted_iota(jnp.int32, (2 * tq, nloc), 0) & (tq - 1)
    ss = lax.broadcasted_iota(jnp.int32, (2 * tq, nloc), 1)
    for u in range(tiles):
        rows = slice(u * tq, (u + 1) * tq)
        first = (j * tiles + u) * tq
        r0 = pl.multiple_of(first, tq)
        band = ((ss - rr >= 0) & (ss - rr <= 2 * WINDOW)
                & (ss >= WINDOW - first) & (ss < t + WINDOW - first))
        cs = cos_ref[pl.ds(r0, tq), :]
        sn = sin_ref[pl.ds(r0, tq), :]
        for g in range(ATT_KV_HEADS):
            q2 = _pair_rows(
                lambda p: _rope(qz_ref[rows, p * LANES:(p + 1) * LANES].astype(F32), cs, sn).astype(BF16), g)
            z2 = _pair_rows(lambda p: qz_ref[rows, ATT_W + p * LANES:ATT_W + (p + 1) * LANES], g)
            keys = [[kl_ref[2 * g + half, pl.ds(r0, nloc), :], kc_ref[2 * g + half]] for half in range(2)]
            vals = [[vl_ref[2 * g + half, pl.ds(r0, nloc), :], vc_ref[2 * g + half]] for half in range(2)]
            o = _attend_group(q2, keys, vals, [band, None], sink_ref, layer, g, z2.astype(F32))
            yc_ref[rows, 2 * g * LANES:(2 * g + 1) * LANES] = o[:tq]
            yc_ref[rows, (2 * g + 1) * LANES:(2 * g + 2) * LANES] = o[tq:]


def _lat_attention(ua, cache_k, cache_v, layer, sink, rope_tabs):
    b, t, _ = ua.shape
    past = cache_k.shape[2]
    tq = TQ_LATENT
    tiles = LAT_TILES_PER_STEP
    nvar = 2 * ATT_KV_HEADS
    cache_spec = pl.BlockSpec((None, None, past, ATT_KV_W), lambda i, j: (i, layer, 0, 0))
    return pl.pallas_call(
        functools.partial(_lat_attn_kernel, t=t, tq=tq, tiles=tiles, layer=layer),
        grid=(b, t // (tiles * tq)),
        in_specs=[pl.BlockSpec(memory_space=pltpu.SMEM),
                  pl.BlockSpec((None, tiles * tq, 2 * ATT_W), lambda i, j: (i, j, 0)),
                  pl.BlockSpec((None, t, 2 * ATT_KV_W), lambda i, j: (i, 0, 2 * ATT_W // (2 * ATT_KV_W))),
                  cache_spec, cache_spec,
                  _const_spec((t, LANES)), _const_spec((t, LANES))],
        out_specs=pl.BlockSpec((None, tiles * tq, ATT_W), lambda i, j: (i, j, 0)),
        out_shape=jax.ShapeDtypeStruct((b, t, ATT_W), BF16),
        scratch_shapes=[pltpu.VMEM((nvar, t + 2 * WINDOW, LANES), BF16),
                        pltpu.VMEM((nvar, t + 2 * WINDOW, 2 * LANES), BF16),
                        pltpu.VMEM((nvar, past, LANES), BF16),
                        pltpu.VMEM((nvar, past, 2 * LANES), BF16)],
        compiler_params=_params(("parallel", "arbitrary")),
        name="lat_attention",
    )(sink, ua, ua, cache_k, cache_v, *rope_tabs)


def _outproj_kernel(x_ref, ya_ref, yb_ref, yc_ref, mod_ref, gpre_ref, gpost_ref,
                    wg_ref, wgt_ref, wa_ref, wb_ref, wc_ref, wo_ref, o_ref, *, layer, tm, row0, rows_per_mod):
    for r in range(tm // SUB_ROWS):
        rows = slice(r * SUB_ROWS, (r + 1) * SUB_ROWS)
        x = x_ref[rows, :]
        h, row = _modulated_norm(x, gpre_ref[layer:layer + 1, :], mod_ref, tm, row0, rows_per_mod)

        def gate_logits(c):
            lo, hi = c * D_MODEL, min((c + 1) * D_MODEL, C_MG)
            mg = jnp.dot(h, wg_ref[:, lo:hi].astype(BF16), preferred_element_type=F32)
            if hi - lo < D_MODEL:
                tail = jnp.dot(h, wgt_ref[...].astype(BF16), preferred_element_type=F32)
                mg = jnp.concatenate([mg, tail], axis=1)
            return mg

        merged = None
        for c, (y_ref, w_ref) in enumerate(((ya_ref, wa_ref), (yb_ref, wb_ref), (yc_ref, wc_ref))):
            term = _sigmoid(gate_logits(c)) * jnp.dot(y_ref[rows, :], w_ref[...].astype(BF16),
                                                      preferred_element_type=F32)
            merged = term if merged is None else merged + term
        out = jnp.dot(merged.astype(BF16), wo_ref[...].astype(BF16), preferred_element_type=F32)
        ms = jnp.mean(out * out, axis=-1, keepdims=True)
        normed = out * lax.rsqrt(ms + EPS) * gpost_ref[layer:layer + 1, :]
        gate = mod_ref[pl.ds(row, 1), 2 * D_MODEL:3 * D_MODEL]
        o_ref[rows, :] = x + gate * normed


def _outproj(x2, ya, yb, yc, layer, mod, g_pre, g_post, w_in, wa, wb, wc, wo, *, row0, rows_per_mod):
    n = x2.shape[0]
    tm = TM_OUTPROJ
    row_spec = lambda w: pl.BlockSpec((tm, w), lambda i: (i, 0))
    return pl.pallas_call(
        functools.partial(_outproj_kernel, layer=layer, tm=tm, row0=row0, rows_per_mod=rows_per_mod),
        grid=(n // tm,),
        in_specs=[row_spec(D_MODEL), row_spec(FOURIER_W), row_spec(RET_W), row_spec(ATT_W),
                  _layer_spec((MOD_ROWS, 3 * D_MODEL), layer),
                  _const_spec((DEPTH, D_MODEL)), _const_spec((DEPTH, D_MODEL)),
                  _layer_spec((D_MODEL, C_MG), layer, col_block=1),
                  _layer_spec((D_MODEL, C_END - 2 * C_MG), layer, col_block=2 * C_MG // (C_END - 2 * C_MG)),
                  _layer_spec((FOURIER_W, D_MODEL), layer), _layer_spec((RET_W, D_MODEL), layer),
                  _layer_spec((ATT_W, D_MODEL), layer), _layer_spec((D_MODEL, D_MODEL), layer)],
        out_specs=row_spec(D_MODEL),
        out_shape=jax.ShapeDtypeStruct((n, D_MODEL), F32),
        compiler_params=_params(("parallel",)),
        name="outproj",
    )(x2, ya, yb, yc, mod, g_pre, g_post, w_in, w_in, wa, wb, wc, wo)


def _layer(x, layer, mod, p, *, latent, dft_chan, dft_pos, rope_tabs=None, cache_k=None, cache_v=None,
           state=None, kv_carry=(), state_carry=()):
    b, t, _ = x.shape
    n = b * t
    row0, rows_per_mod = (1, t) if latent else (0, n)
    x2 = x.reshape(n, D_MODEL)
    xcs, fz, ur, ua, *kv32 = _inproj(x2, layer, mod, p["g_pre"], p["w_in"], dft_chan,
                                     seq=t, row0=row0, rows_per_mod=rows_per_mod, emit_kv=not latent,
                                     kv_carry=kv_carry)
    ya = _fourier(xcs.reshape(b, t, -1), fz.reshape(b, t, -1), dft_pos, p["w_four"], layer,
                  bg=b if latent else BG_CTX_FOURIER, tq=min(t, TQ_FOURIER))
    ur3 = ur.reshape(b, t, -1)
    ua3 = ua.reshape(b, t, -1)
    if latent:
        (yb,), s_fin = _retention(ur3, layer, p["dec"], p["gn"], nb=1, emit_state=False,
                                  rope_tabs=rope_tabs, state=state), None
        yc = _lat_attention(ua3, cache_k, cache_v, layer, p["sink"], rope_tabs)
    else:
        yb, s_fin = _retention(ur3, layer, p["dec"], p["gn"], nb=NB_CTX_RET, emit_state=True,
                               state_carry=state_carry)
        yc = _ctx_attention(ua3, layer, p["sink"])
    out = _outproj(x2, ya.reshape(n, -1), yb.reshape(n, -1), yc.reshape(n, -1), layer, mod,
                   p["g_pre"], p["g_post"], p["w_in"], p["w_pa"], p["w_pb"], p["w_pc"], p["w_out"],
                   row0=row0, rows_per_mod=rows_per_mod)
    return out.reshape(b, t, D_MODEL), kv32, s_fin


def kernel(x_prompt, x_sample, cache_k, cache_v, state_ret, c, c_ctx, w_mod, b_mod, g_pre, g_post, w_in,
           w_four, ret_decay, ret_gn, attn_sink, w_branch_a, w_branch_b, w_branch_c, w_out):
    batch, seq, _ = x_prompt.shape
    dec_batch, dec_seq, _ = x_sample.shape
    past = cache_k.shape[2]
    assert 1 + dec_batch <= MOD_ROWS

    cv = jnp.zeros((MOD_ROWS, D_MODEL), F32).at[0].set(c_ctx).at[1:1 + dec_batch].set(c)
    mod = _modulation(cv, w_mod, b_mod)

    dft_chan, dft_ctx = _dft_tables(seq)
    _, dft_lat = _dft_tables(dec_seq)
    rope_tabs = _rope_tables(dec_seq)
    ck = cache_k.reshape(dec_batch, DEPTH, past, ATT_KV_W)
    cvv = cache_v.reshape(dec_batch, DEPTH, past, ATT_KV_W)

    p = dict(
        g_pre=g_pre, g_post=g_post,
        w_in=w_in, w_four=w_four, w_pa=w_branch_a, w_pb=w_branch_b, w_pc=w_branch_c, w_out=w_out,
        dec=ret_decay, gn=ret_gn, sink=attn_sink)

    xp = x_prompt
    kv_all, s_all = (), ()
    for l in range(DEPTH):
        xp, kv_all, s_all = _layer(xp, l, mod, p, latent=False, dft_chan=dft_chan, dft_pos=dft_ctx,
                                   kv_carry=tuple(kv_all), state_carry=s_all)
        s_all = (s_all,)

    xs = x_sample
    for l in range(DEPTH):
        xs, _, _ = _layer(xs, l, mod, p, latent=True, dft_chan=dft_chan, dft_pos=dft_lat,
                             rope_tabs=rope_tabs, cache_k=ck, cache_v=cvv, state=state_ret)

    new_k, new_v = (jnp.transpose(a, (0, 1, 4, 2, 3)) for a in kv_all)
    return (xp, xs, new_k, new_v, s_all[0])
```

```python
import functools
import math

import numpy as np
import jax
import jax.numpy as jnp
from jax import lax
from jax.experimental import pallas as pl
from jax.experimental.pallas import tpu as pltpu

F32 = jnp.float32
BF16 = jnp.bfloat16

D_MODEL = 1024
DEPTH = 2
GRID_W = 64
HEAD_DIM = 64
FOURIER_GROUPS = 4
FOURIER_GROUP_W = 64
FOURIER_W = FOURIER_GROUPS * FOURIER_GROUP_W
RET_HEADS = 4
RET_DK = 64
RET_W = RET_HEADS * RET_DK
RET_CHUNK = 128
ATT_Q_HEADS = 8
ATT_KV_HEADS = 2
ATT_W = ATT_Q_HEADS * HEAD_DIM
ATT_KV_W = ATT_KV_HEADS * HEAD_DIM
WINDOW = 128
ROPE_BASE = 10000.0
EPS = 1e-6
MOD_ROWS = 8
LANES = 128
NEG = -1e30
LOG2E = math.log2(math.e)
VMEM_LIMIT = 56 * 1024 * 1024

C_FX, C_RQ, C_RZ_END = 0, 512, 1536
C_AQ, C_AK, C_AV, C_AZ, C_MG, C_END = 1536, 2048, 2176, 2304, 2816, 5888

TM_INPROJ = 1024
TM_OUTPROJ = 1024
SUB_ROWS = 512
TQ_LATENT = 256
LAT_TILES_PER_STEP = 4
TQ_FOURIER = 1024
NB_CTX_RET = 8
NB_CTX_ATTN = 4
BG_CTX_FOURIER = 8
RET_UNROLL = 8


def _sigmoid(x):
    return 0.5 * jnp.tanh(0.5 * x) + 0.5


def _silu(x):
    return x * _sigmoid(x)


def _params(sem):
    return pltpu.CompilerParams(dimension_semantics=sem, vmem_limit_bytes=VMEM_LIMIT)


def _const_spec(shape):
    nd = len(shape)
    return pl.BlockSpec(shape, lambda *_: (0,) * nd, pipeline_mode=pl.Buffered(1))


def _layer_spec(shape, layer, col_block=0):
    idx = (layer,) + (0,) * (len(shape) - 1) + (col_block,)
    return pl.BlockSpec((None,) + tuple(shape), lambda *_: idx, pipeline_mode=pl.Buffered(1))


def _carry_args(carry, first_input, first_output):
    specs = [pl.BlockSpec(memory_space=pl.ANY)] * len(carry)
    aliases = {first_input + k: first_output + k for k in range(len(carry))}
    return specs, list(carry), aliases


def _slab_spec(lead, rest, layer, whole):
    zeros = (0,) * len(rest)
    if whole:
        return pl.BlockSpec((lead, DEPTH) + tuple(rest), lambda i: (i, 0) + zeros)
    return pl.BlockSpec((lead, None) + tuple(rest), lambda i: (i, layer) + zeros)


def _store_slab(ref, lead_idx, layer, whole, value, rest_idx=()):
    if not whole:
        ref[(lead_idx,) + tuple(rest_idx)] = value
        return
    for l in range(DEPTH):
        ref[(lead_idx, l) + tuple(rest_idx)] = value if l == layer else jnp.zeros_like(value)


def _dft_tables(t):
    c = np.arange(FOURIER_GROUP_W)
    ang = 2.0 * np.pi * ((c[:, None] * c[None, :]) % FOURIER_GROUP_W) / FOURIER_GROUP_W
    eye = np.eye(FOURIER_GROUPS)
    s64 = FOURIER_GROUP_W ** -0.5
    chan = np.concatenate([np.kron(eye, np.cos(ang) * s64), np.kron(eye, np.sin(ang) * s64)], axis=1)
    p = np.arange(t)
    angt = 2.0 * np.pi * ((p[:, None] * p[None, :]) % t) / t
    pos = np.concatenate([np.cos(angt), -np.sin(angt)], axis=1) * (t ** -0.5)
    return jnp.asarray(chan, F32).astype(BF16), jnp.asarray(pos, F32).astype(BF16)


def _rope_tables(t):
    quarter = HEAD_DIM // 4
    lane = np.arange(LANES) % HEAD_DIM
    inv = ROPE_BASE ** (-(lane % quarter).astype(np.float64) / quarter)
    n = np.arange(t)
    pos = np.where(lane[None, :] < HEAD_DIM // 2, (n // GRID_W)[:, None], (n % GRID_W)[:, None])
    ang = pos.astype(np.float64) * inv[None, :]
    sign = np.where((lane % (2 * quarter)) < quarter, -1.0, 1.0)
    return jnp.asarray(np.cos(ang), F32), jnp.asarray(np.sin(ang) * sign[None, :], F32)


def _rope(x, cos, sin):
    lane = lax.broadcasted_iota(jnp.int32, x.shape, 1)
    first = (lane & 31) < 16
    partner = jnp.where(first, pltpu.roll(x, LANES - 16, axis=1), pltpu.roll(x, 16, axis=1))
    return x * cos + partner * sin


def _split_bf16(x):
    hi = x.astype(BF16)
    return hi, (x - hi.astype(F32)).astype(BF16)


def _mod_kernel(cv_ref, w_ref, b_ref, o_ref):
    a_hi, a_lo = _split_bf16(_silu(cv_ref[...]))
    w_hi, w_lo = _split_bf16(w_ref[...])
    dot = functools.partial(jnp.dot, preferred_element_type=F32)
    bias = b_ref[pl.ds(pl.program_id(0), 1), :]
    o_ref[...] = dot(a_hi, w_hi) + (dot(a_lo, w_hi) + dot(a_hi, w_lo)) + bias


def _modulation(cv, w_mod, b_mod):
    tn = 1024
    return pl.pallas_call(
        _mod_kernel,
        grid=(DEPTH, 3 * D_MODEL // tn),
        in_specs=[pl.BlockSpec((MOD_ROWS, D_MODEL), lambda l, j: (0, 0)),
                  pl.BlockSpec((None, D_MODEL, tn), lambda l, j: (l, 0, j)),
                  pl.BlockSpec((DEPTH, tn), lambda l, j: (0, j))],
        out_specs=pl.BlockSpec((None, MOD_ROWS, tn), lambda l, j: (l, 0, j)),
        out_shape=jax.ShapeDtypeStruct((DEPTH, MOD_ROWS, 3 * D_MODEL), F32),
        compiler_params=_params(("parallel", "parallel")),
        name="modulation",
    )(cv, w_mod, b_mod)


def _modulated_norm(x, g, mod_ref, tm, row0, rows_per_mod):
    row = row0 + (pl.program_id(0) * tm) // rows_per_mod
    ms = jnp.mean(x * x, axis=-1, keepdims=True)
    y = x * lax.rsqrt(ms + EPS) * g
    shift = mod_ref[pl.ds(row, 1), 0:D_MODEL]
    scale = mod_ref[pl.ds(row, 1), D_MODEL:2 * D_MODEL]
    return (y * (1.0 + scale) + shift).astype(BF16), row


def _inproj_kernel(x_ref, mod_ref, g_ref, w_ref, dft_ref, *refs, layer, tm, seq, row0, rows_per_mod,
                   n_carry):
    xcs_ref, fz_ref, ur_ref, ua_ref, *kv_refs = refs[n_carry:]
    for r in range(tm // SUB_ROWS):
        rows = slice(r * SUB_ROWS, (r + 1) * SUB_ROWS)
        h, _ = _modulated_norm(x_ref[rows, :], g_ref[layer:layer + 1, :], mod_ref, tm, row0, rows_per_mod)

        def mm(c0, c1):
            return jnp.dot(h, w_ref[:, c0:c1].astype(BF16), preferred_element_type=F32)

        f = mm(C_FX, C_RQ)
        xcs_ref[rows, :] = jnp.dot(f[:, :FOURIER_W].astype(BF16), dft_ref[...],
                                   preferred_element_type=F32).astype(BF16)
        fz_ref[rows, :] = f[:, FOURIER_W:].astype(BF16)
        ur_ref[rows, :] = mm(C_RQ, C_RZ_END).astype(BF16)
        ua_ref[rows, 0:ATT_W] = mm(C_AQ, C_AK).astype(BF16)
        ua_ref[rows, ATT_W:2 * ATT_W] = mm(C_AZ, C_MG).astype(BF16)
        kv = mm(C_AK, C_AZ)
        if kv_refs:
            for s in range(SUB_ROWS // seq):
                blk = kv[s * seq:(s + 1) * seq]
                bi = (r * SUB_ROWS) // seq + s
                for ref, cols in zip(kv_refs, (blk[:, :ATT_KV_W], blk[:, ATT_KV_W:])):
                    _store_slab(ref, bi, layer, n_carry == 0,
                                cols.T.reshape(ATT_KV_HEADS, HEAD_DIM, seq))
        ua_ref[rows, 2 * ATT_W:] = kv.astype(BF16)


def _inproj(x2, layer, mod, g_pre, w_in, dft, *, seq, row0, rows_per_mod, emit_kv, kv_carry=()):
    n = x2.shape[0]
    tm = TM_INPROJ
    row_spec = lambda w: pl.BlockSpec((tm, w), lambda i: (i, 0))
    widths = (2 * FOURIER_W, FOURIER_W, 4 * RET_W, 2 * ATT_W + 2 * ATT_KV_W)
    out_shape = [jax.ShapeDtypeStruct((n, w), BF16) for w in widths]
    out_specs = [row_spec(w) for w in widths]
    if emit_kv:
        assert SUB_ROWS % seq == 0
        out_shape += [jax.ShapeDtypeStruct((n // seq, DEPTH, ATT_KV_HEADS, HEAD_DIM, seq), F32)] * 2
        out_specs += [_slab_spec(tm // seq, (ATT_KV_HEADS, HEAD_DIM, seq), layer, not kv_carry)] * 2
    args = [x2, mod, g_pre, w_in, dft]
    carry_specs, carry, aliases = _carry_args(kv_carry, len(args), len(widths))
    return pl.pallas_call(
        functools.partial(_inproj_kernel, layer=layer, tm=tm, seq=seq, row0=row0, rows_per_mod=rows_per_mod,
                          n_carry=len(carry)),
        grid=(n // tm,),
        in_specs=[row_spec(D_MODEL),
                  _layer_spec((MOD_ROWS, 3 * D_MODEL), layer),
                  _const_spec((DEPTH, D_MODEL)),
                  _layer_spec((D_MODEL, C_MG), layer),
                  _const_spec((FOURIER_W, 2 * FOURIER_W))] + carry_specs,
        out_specs=out_specs,
        out_shape=out_shape,
        input_output_aliases=aliases,
        compiler_params=_params(("parallel",)),
        name="inproj",
    )(*args, *carry)


def _fourier_kernel(ct_ref, xcs_ref, fz_ref, w_ref, ya_ref, xcat_ref, *, bg, t):
    @pl.when(pl.program_id(1) == 0)
    def _gather():
        for b in range(bg):
            cols = slice(b * FOURIER_W, (b + 1) * FOURIER_W)
            xcat_ref[0:t, cols] = xcs_ref[b, :, 0:FOURIER_W]
            xcat_ref[t:2 * t, cols] = xcs_ref[b, :, FOURIER_W:]

    yr = jnp.dot(ct_ref[...], xcat_ref[...], preferred_element_type=F32).astype(BF16)
    w = w_ref[...].astype(BF16)
    for b in range(bg):
        ya = jnp.dot(yr[:, b * FOURIER_W:(b + 1) * FOURIER_W], w, preferred_element_type=F32)
        ya_ref[b] = (ya * _silu(fz_ref[b].astype(F32))).astype(BF16)


def _fourier(xcs, fz, ct, w_four, layer, *, bg, tq):
    b, t, _ = xcs.shape
    return pl.pallas_call(
        functools.partial(_fourier_kernel, bg=bg, t=t),
        grid=(b // bg, t // tq),
        in_specs=[pl.BlockSpec((tq, 2 * t), lambda i, j: (j, 0)),
                  pl.BlockSpec((bg, t, 2 * FOURIER_W), lambda i, j: (i, 0, 0)),
                  pl.BlockSpec((bg, tq, FOURIER_W), lambda i, j: (i, j, 0)),
                  _layer_spec((FOURIER_W, FOURIER_W), layer)],
        out_specs=pl.BlockSpec((bg, tq, FOURIER_W), lambda i, j: (i, j, 0)),
        out_shape=jax.ShapeDtypeStruct((b, t, FOURIER_W), BF16),
        scratch_shapes=[pltpu.VMEM((2 * t, bg * FOURIER_W), BF16)],
        compiler_params=_params(("parallel", "arbitrary")),
        name="fourier",
    )(ct, xcs, fz, w_four)


def _log_sigmoid(x):
    return jnp.minimum(x, 0.0) - jnp.log(1.0 + jnp.exp(-jnp.abs(x)))


def _head_blocks(a, width):
    lane = lax.broadcasted_iota(jnp.int32, a.shape, 1)
    zero = jnp.zeros_like(a)
    return jnp.concatenate(
        [jnp.where((lane >= h * width) & (lane < (h + 1) * width), a, zero) for h in range(RET_HEADS)],
        axis=0)


def _per_head(dec_ref, layer, direction, head_of):
    out = jnp.full(head_of.shape, dec_ref[layer, direction, RET_HEADS - 1], F32)
    for h in range(RET_HEADS - 2, -1, -1):
        out = jnp.where(head_of == h, dec_ref[layer, direction, h], out)
    return out


def _ret_kernel(*refs, nb, t, layer, rope, has_s0, n_carry, emit_state):
    refs = list(refs)
    dec_ref = refs.pop(0)
    ur_ref = refs.pop(0)
    cos_ref = refs.pop(0) if rope else None
    sin_ref = refs.pop(0) if rope else None
    s0_ref = refs.pop(0) if has_s0 else None
    gn_ref = refs.pop(0)
    del refs[:n_carry]
    yb_ref = refs.pop(0)
    sfin_ref = refs.pop(0) if emit_state else None
    (kr_ref, dsf_ref, dsb_ref, sfs_ref, sbs_ref, st_ref) = refs
    c = RET_CHUNK
    nc = t // c
    nch = nb * nc
    group = RET_UNROLL

    lane_head = lax.broadcasted_iota(jnp.int32, (1, RET_W), 1) // RET_DK
    lgf = _log_sigmoid(_per_head(dec_ref, layer, 0, lane_head))
    lgb = _log_sigmoid(_per_head(dec_ref, layer, 1, lane_head))
    ri = lax.broadcasted_iota(jnp.int32, (c, RET_W), 0).astype(F32)
    read_f = jnp.exp((ri + 1.0) * lgf)
    read_b = jnp.exp((c - ri) * lgb)
    write_f = jnp.exp((c - 1.0 - ri) * lgf)
    write_b = jnp.exp(ri * lgb)
    carry_f = jnp.exp(c * lgf)
    carry_b = jnp.exp(c * lgb)
    ii = lax.broadcasted_iota(jnp.int32, (RET_HEADS * c, c), 0)
    jj = lax.broadcasted_iota(jnp.int32, (RET_HEADS * c, c), 1)
    row_head = ii // c
    diff = ((ii & (c - 1)) - jj).astype(F32)
    decay = (jnp.where(diff >= 0, jnp.exp(jnp.maximum(diff, 0.0)
                                          * _log_sigmoid(_per_head(dec_ref, layer, 0, row_head))), 0.0)
             + jnp.where(diff <= 0, jnp.exp(jnp.maximum(-diff, 0.0)
                                            * _log_sigmoid(_per_head(dec_ref, layer, 1, row_head))), 0.0))
    r2 = lax.broadcasted_iota(jnp.int32, (RET_W, RET_W), 0)
    c2 = lax.broadcasted_iota(jnp.int32, (RET_W, RET_W), 1)
    same_head = (r2 // RET_DK) == (c2 // RET_DK)
    group_mean = jnp.where(same_head, 1.0 / RET_DK, 0.0).astype(BF16)
    out_head = lax.broadcasted_iota(jnp.int32, (c, RET_W), 1) // RET_DK

    def chunk_pos(ci):
        return ci // nc, pl.multiple_of((ci % nc) * c, c)

    def load_qk(col, b, r0):
        a = ur_ref[b, pl.ds(r0, c), col:col + RET_W].astype(F32)
        if rope:
            cs = cos_ref[pl.ds(r0, c), :]
            sn = sin_ref[pl.ds(r0, c), :]
            a = jnp.concatenate([_rope(a[:, :LANES], cs, sn), _rope(a[:, LANES:], cs, sn)], axis=1)
        return a

    def increments(ci, carry):
        b, r0 = chunk_pos(ci)
        k = load_qk(RET_W, b, r0) * (RET_DK ** -0.5)
        v = ur_ref[b, pl.ds(r0, c), 2 * RET_W:3 * RET_W]
        kr_ref[pl.ds(pl.multiple_of(ci * c, c), c), :] = k.astype(BF16)
        kw = jnp.concatenate([(k * write_f).astype(BF16), (k * write_b).astype(BF16)], axis=1)
        d = lax.dot_general(kw, v, (((0,), (0,)), ((), ())), preferred_element_type=F32)
        dsf_ref[ci] = jnp.where(same_head, d[:RET_W], 0.0)
        dsb_ref[ci] = jnp.where(same_head, d[RET_W:], 0.0)
        return carry

    lax.fori_loop(0, nch, increments, 0, unroll=group)

    def scan(b, direction, ds_ref, out_ref, carry_decay):
        st_ref[...] = jnp.zeros((RET_W, RET_W), F32)
        if has_s0:
            for h in range(RET_HEADS):
                sl = slice(h * RET_DK, (h + 1) * RET_DK)
                st_ref[sl, sl] = s0_ref[b, direction, h]

        def step(n, carry):
            ci = b * nc + (n if direction == 0 else nc - 1 - n)
            out_ref[ci] = st_ref[...].astype(BF16)
            st_ref[...] = carry_decay * st_ref[...] + ds_ref[ci]
            return carry

        lax.fori_loop(0, nc, step, 0, unroll=min(nc, 4))
        if emit_state:
            for h in range(RET_HEADS):
                sl = slice(h * RET_DK, (h + 1) * RET_DK)
                _store_slab(sfin_ref, b, layer, n_carry == 0, st_ref[sl, sl], (direction, h))

    for b in range(nb):
        scan(b, 0, dsf_ref, sfs_ref, carry_f)
        scan(b, 1, dsb_ref, sbs_ref, carry_b)

    def chunk_output(ci, b, r0):
        q = load_qk(0, b, r0)
        kb = kr_ref[pl.ds(pl.multiple_of(ci * c, c), c), :]
        v = ur_ref[b, pl.ds(r0, c), 2 * RET_W:3 * RET_W]
        att = lax.dot_general(_head_blocks(q.astype(BF16), RET_DK), kb,
                              (((1,), (1,)), ((), ())), preferred_element_type=F32)
        o4 = jnp.dot((att * decay).astype(BF16), v, preferred_element_type=F32)
        o = o4[(RET_HEADS - 1) * c:]
        for h in range(RET_HEADS - 2, -1, -1):
            o = jnp.where(out_head == h, o4[h * c:(h + 1) * c], o)
        o = o + jnp.dot((q * read_f).astype(BF16), sfs_ref[ci], preferred_element_type=F32)
        return o + jnp.dot((q * read_b).astype(BF16), sbs_ref[ci], preferred_element_type=F32)

    def outputs(it, carry):
        pos = [chunk_pos(it * group + u) for u in range(group)]
        o = jnp.concatenate([chunk_output(it * group + u, *pos[u]) for u in range(group)], axis=0)
        ms = jnp.dot((o * o).astype(BF16), group_mean, preferred_element_type=F32)
        y = o * lax.rsqrt(ms + EPS) * gn_ref[layer:layer + 1, :]
        for u, (b, r0) in enumerate(pos):
            z = ur_ref[b, pl.ds(r0, c), 3 * RET_W:4 * RET_W].astype(F32)
            yb_ref[b, pl.ds(r0, c), :] = (y[u * c:(u + 1) * c] * _silu(z)).astype(BF16)
        return carry

    lax.fori_loop(0, nch // group, outputs, 0)


def _retention(ur, layer, dec, gn, *, nb, emit_state, rope_tabs=None, state=None, state_carry=()):
    b, t, _ = ur.shape
    nch = nb * (t // RET_CHUNK)
    assert nch % RET_UNROLL == 0
    rope = rope_tabs is not None
    has_s0 = state is not None
    in_specs = [pl.BlockSpec(memory_space=pltpu.SMEM),
                pl.BlockSpec((nb, t, 4 * RET_W), lambda i: (i, 0, 0))]
    args = [dec, ur]
    if rope:
        in_specs += [_const_spec((t, LANES)), _const_spec((t, LANES))]
        args += list(rope_tabs)
    if has_s0:
        in_specs.append(pl.BlockSpec((nb, None, 2, RET_HEADS, RET_DK, RET_DK),
                                     lambda i: (i, layer, 0, 0, 0, 0)))
        args.append(state)
    in_specs.append(_const_spec((DEPTH, RET_W)))
    args.append(gn)
    carry_specs, carry, aliases = _carry_args(state_carry, len(args), 1)
    state_scratch = lambda dt: pltpu.VMEM((nch, RET_W, RET_W), dt)
    out_specs = [pl.BlockSpec((nb, t, RET_W), lambda i: (i, 0, 0))]
    out_shape = [jax.ShapeDtypeStruct((b, t, RET_W), BF16)]
    if emit_state:
        out_specs.append(_slab_spec(nb, (2, RET_HEADS, RET_DK, RET_DK), layer, not state_carry))
        out_shape.append(jax.ShapeDtypeStruct((b, DEPTH, 2, RET_HEADS, RET_DK, RET_DK), F32))
    return pl.pallas_call(
        functools.partial(_ret_kernel, nb=nb, t=t, layer=layer, rope=rope, has_s0=has_s0,
                          n_carry=len(carry), emit_state=emit_state),
        grid=(b // nb,),
        in_specs=in_specs + carry_specs,
        out_specs=out_specs,
        out_shape=out_shape,
        input_output_aliases=aliases,
        scratch_shapes=[pltpu.VMEM((nb * t, RET_W), BF16),
                        state_scratch(F32), state_scratch(F32), state_scratch(BF16), state_scratch(BF16),
                        pltpu.VMEM((RET_W, RET_W), F32)],
        compiler_params=_params(("parallel",)),
        name="retention",
    )(*args, *carry)


def _kv_variants(a, ones_block):
    lane_half = lax.broadcasted_iota(jnp.int32, a.shape, 1) // HEAD_DIM
    swapped = pltpu.roll(a, HEAD_DIM, axis=1)
    out = []
    for g in range(ATT_KV_HEADS):
        row = []
        for half in range(2):
            var = jnp.where(lane_half == half, a if half == g else swapped, 0.0).astype(BF16)
            if ones_block:
                ones = jnp.where(lane_half == half, 1.0, 0.0).astype(BF16)
                var = jnp.concatenate([var, ones], axis=1)
            row.append(var)
        out.append(row)
    return out


def _attend_group(q2, keys, values, masks, sink_ref, layer, g, z2):
    rows = q2.shape[0]
    upper = lax.broadcasted_iota(jnp.int32, (rows, 1), 0) < rows // 2
    acc = None
    sink_terms = []
    for half in range(2):
        sink = jnp.where(upper, sink_ref[layer, g, half], sink_ref[layer, g, 2 + half]) * LOG2E
        logits = []
        for kpart, mask in zip(keys[half], masks):
            l = lax.dot_general(q2, kpart, (((1,), (1,)), ((), ())), preferred_element_type=F32)
            logits.append(l if mask is None else jnp.where(mask, l, NEG))
        m = sink
        for l in logits:
            m = jnp.maximum(m, jnp.max(l, axis=-1, keepdims=True))
        for l, vpart in zip(logits, values[half]):
            pv = jnp.dot(jnp.exp2(l - m).astype(BF16), vpart, preferred_element_type=F32)
            acc = pv if acc is None else acc + pv
        sink_terms.append(jnp.exp2(sink - m))
    lane = lax.broadcasted_iota(jnp.int32, (rows, LANES), 1)
    den = acc[:, LANES:] + jnp.where(lane < HEAD_DIM, sink_terms[0], sink_terms[1])
    return (acc[:, :LANES] * (1.0 / den) * _silu(z2)).astype(BF16)


def _pair_rows(ref_slice, g):
    return jnp.concatenate([ref_slice(2 * g), ref_slice(2 * g + 1)], axis=0)


def _ctx_attn_kernel(sink_ref, ua_ref, yc_ref, *, nb, t, layer):
    for b in range(nb):
        kv = ua_ref[b, :, 2 * ATT_W:].astype(F32)
        kvar = _kv_variants(kv[:, :LANES] * (HEAD_DIM ** -0.5 * LOG2E), False)
        vvar = _kv_variants(kv[:, LANES:], True)
        for g in range(ATT_KV_HEADS):
            q2 = _pair_rows(lambda p: ua_ref[b, :, p * LANES:(p + 1) * LANES], g)
            z2 = _pair_rows(lambda p: ua_ref[b, :, ATT_W + p * LANES:ATT_W + (p + 1) * LANES], g)
            o = _attend_group(q2, [[kvar[g][0]], [kvar[g][1]]], [[vvar[g][0]], [vvar[g][1]]], [None],
                              sink_ref, layer, g, z2.astype(F32))
            yc_ref[b, :, 2 * g * LANES:(2 * g + 1) * LANES] = o[:t]
            yc_ref[b, :, (2 * g + 1) * LANES:(2 * g + 2) * LANES] = o[t:]


def _ctx_attention(ua, layer, sink):
    b, t, w = ua.shape
    nb = NB_CTX_ATTN
    return pl.pallas_call(
        functools.partial(_ctx_attn_kernel, nb=nb, t=t, layer=layer),
        grid=(b // nb,),
        in_specs=[pl.BlockSpec(memory_space=pltpu.SMEM),
                  pl.BlockSpec((nb, t, w), lambda i: (i, 0, 0))],
        out_specs=pl.BlockSpec((nb, t, ATT_W), lambda i: (i, 0, 0)),
        out_shape=jax.ShapeDtypeStruct((b, t, ATT_W), BF16),
        compiler_params=_params(("parallel",)),
        name="ctx_attention",
    )(sink, ua)


def _lat_attn_kernel(sink_ref, qz_ref, kv_ref, ck_ref, cv_ref, cos_ref, sin_ref, yc_ref,
                     kl_ref, vl_ref, kc_ref, vc_ref, *, t, tq, tiles, layer):
    j = pl.program_id(1)
    scale = HEAD_DIM ** -0.5 * LOG2E

    @pl.when(j == 0)
    def _prepare():
        kv = kv_ref[...].astype(F32)
        kvar = _kv_variants(_rope(kv[:, :LANES], cos_ref[...], sin_ref[...]) * scale, False)
        vvar = _kv_variants(kv[:, LANES:], True)
        cvar = _kv_variants(ck_ref[...] * scale, False)
        dvar = _kv_variants(cv_ref[...], True)
        for g in range(ATT_KV_HEADS):
            for half in range(2):
                i = 2 * g + half
                for ref, var in ((kl_ref, kvar), (vl_ref, vvar)):
                    pad = jnp.zeros((WINDOW, ref.shape[-1]), BF16)
                    ref[i, 0:WINDOW, :] = pad
                    ref[i, WINDOW:WINDOW + t, :] = var[g][half]
                    ref[i, WINDOW + t:, :] = pad
                kc_ref[i] = cvar[g][half]
                vc_ref[i] = dvar[g][half]

    nloc = tq + 2 * WINDOW
    rr = lax.broadcasted_iota(jnp.int32, (2 * tq, nloc), 0) & (tq - 1)
    ss = lax.broadcasted_iota(jnp.int32, (2 * tq, nloc), 1)
    for u in range(tiles):
        rows = slice(u * tq, (u + 1) * tq)
        first = (j * tiles + u) * tq
        r0 = pl.multiple_of(first, tq)
        band = ((ss - rr >= 0) & (ss - rr <= 2 * WINDOW)
                & (ss >= WINDOW - first) & (ss < t + WINDOW - first))
        cs = cos_ref[pl.ds(r0, tq), :]
        sn = sin_ref[pl.ds(r0, tq), :]
        for g in range(ATT_KV_HEADS):
            q2 = _pair_rows(
                lambda p: _rope(qz_ref[rows, p * LANES:(p + 1) * LANES].astype(F32), cs, sn).astype(BF16), g)
            z2 = _pair_rows(lambda p: qz_ref[rows, ATT_W + p * LANES:ATT_W + (p + 1) * LANES], g)
            keys = [[kl_ref[2 * g + half, pl.ds(r0, nloc), :], kc_ref[2 * g + half]] for half in range(2)]
            vals = [[vl_ref[2 * g + half, pl.ds(r0, nloc), :], vc_ref[2 * g + half]] for half in range(2)]
            o = _attend_group(q2, keys, vals, [band, None], sink_ref, layer, g, z2.astype(F32))
            yc_ref[rows, 2 * g * LANES:(2 * g + 1) * LANES] = o[:tq]
            yc_ref[rows, (2 * g + 1) * LANES:(2 * g + 2) * LANES] = o[tq:]


def _lat_attention(ua, cache_k, cache_v, layer, sink, rope_tabs):
    b, t, _ = ua.shape
    past = cache_k.shape[2]
    tq = TQ_LATENT
    tiles = LAT_TILES_PER_STEP
    nvar = 2 * ATT_KV_HEADS
    cache_spec = pl.BlockSpec((None, None, past, ATT_KV_W), lambda i, j: (i, layer, 0, 0))
    return pl.pallas_call(
        functools.partial(_lat_attn_kernel, t=t, tq=tq, tiles=tiles, layer=layer),
        grid=(b, t // (tiles * tq)),
        in_specs=[pl.BlockSpec(memory_space=pltpu.SMEM),
                  pl.BlockSpec((None, tiles * tq, 2 * ATT_W), lambda i, j: (i, j, 0)),
                  pl.BlockSpec((None, t, 2 * ATT_KV_W), lambda i, j: (i, 0, 2 * ATT_W // (2 * ATT_KV_W))),
                  cache_spec, cache_spec,
                  _const_spec((t, LANES)), _const_spec((t, LANES))],
        out_specs=pl.BlockSpec((None, tiles * tq, ATT_W), lambda i, j: (i, j, 0)),
        out_shape=jax.ShapeDtypeStruct((b, t, ATT_W), BF16),
        scratch_shapes=[pltpu.VMEM((nvar, t + 2 * WINDOW, LANES), BF16),
                        pltpu.VMEM((nvar, t + 2 * WINDOW, 2 * LANES), BF16),
                        pltpu.VMEM((nvar, past, LANES), BF16),
                        pltpu.VMEM((nvar, past, 2 * LANES), BF16)],
        compiler_params=_params(("parallel", "arbitrary")),
        name="lat_attention",
    )(sink, ua, ua, cache_k, cache_v, *rope_tabs)


def _outproj_kernel(x_ref, ya_ref, yb_ref, yc_ref, mod_ref, gpre_ref, gpost_ref,
                    wg_ref, wgt_ref, wa_ref, wb_ref, wc_ref, wo_ref, o_ref, *, layer, tm, row0, rows_per_mod):
    for r in range(tm // SUB_ROWS):
        rows = slice(r * SUB_ROWS, (r + 1) * SUB_ROWS)
        x = x_ref[rows, :]
        h, row = _modulated_norm(x, gpre_ref[layer:layer + 1, :], mod_ref, tm, row0, rows_per_mod)

        def gate_logits(c):
            lo, hi = c * D_MODEL, min((c + 1) * D_MODEL, C_MG)
            mg = jnp.dot(h, wg_ref[:, lo:hi].astype(BF16), preferred_element_type=F32)
            if hi - lo < D_MODEL:
                tail = jnp.dot(h, wgt_ref[...].astype(BF16), preferred_element_type=F32)
                mg = jnp.concatenate([mg, tail], axis=1)
            return mg

        merged = None
        for c, (y_ref, w_ref) in enumerate(((ya_ref, wa_ref), (yb_ref, wb_ref), (yc_ref, wc_ref))):
            term = _sigmoid(gate_logits(c)) * jnp.dot(y_ref[rows, :], w_ref[...].astype(BF16),
                                                      preferred_element_type=F32)
            merged = term if merged is None else merged + term
        out = jnp.dot(merged.astype(BF16), wo_ref[...].astype(BF16), preferred_element_type=F32)
        ms = jnp.mean(out * out, axis=-1, keepdims=True)
        normed = out * lax.rsqrt(ms + EPS) * gpost_ref[layer:layer + 1, :]
        gate = mod_ref[pl.ds(row, 1), 2 * D_MODEL:3 * D_MODEL]
        o_ref[rows, :] = x + gate * normed


def _outproj(x2, ya, yb, yc, layer, mod, g_pre, g_post, w_in, wa, wb, wc, wo, *, row0, rows_per_mod):
    n = x2.shape[0]
    tm = TM_OUTPROJ
    row_spec = lambda w: pl.BlockSpec((tm, w), lambda i: (i, 0))
    return pl.pallas_call(
        functools.partial(_outproj_kernel, layer=layer, tm=tm, row0=row0, rows_per_mod=rows_per_mod),
        grid=(n // tm,),
        in_specs=[row_spec(D_MODEL), row_spec(FOURIER_W), row_spec(RET_W), row_spec(ATT_W),
                  _layer_spec((MOD_ROWS, 3 * D_MODEL), layer),
                  _const_spec((DEPTH, D_MODEL)), _const_spec((DEPTH, D_MODEL)),
                  _layer_spec((D_MODEL, C_MG), layer, col_block=1),
                  _layer_spec((D_MODEL, C_END - 2 * C_MG), layer, col_block=2 * C_MG // (C_END - 2 * C_MG)),
                  _layer_spec((FOURIER_W, D_MODEL), layer), _layer_spec((RET_W, D_MODEL), layer),
                  _layer_spec((ATT_W, D_MODEL), layer), _layer_spec((D_MODEL, D_MODEL), layer)],
        out_specs=row_spec(D_MODEL),
        out_shape=jax.ShapeDtypeStruct((n, D_MODEL), F32),
        compiler_params=_params(("parallel",)),
        name="outproj",
    )(x2, ya, yb, yc, mod, g_pre, g_post, w_in, w_in, wa, wb, wc, wo)


def _layer(x, layer, mod, p, *, latent, dft_chan, dft_pos, rope_tabs=None, cache_k=None, cache_v=None,
           state=None, kv_carry=(), state_carry=()):
    b, t, _ = x.shape
    n = b * t
    row0, rows_per_mod = (1, t) if latent else (0, n)
    x2 = x.reshape(n, D_MODEL)
    xcs, fz, ur, ua, *kv32 = _inproj(x2, layer, mod, p["g_pre"], p["w_in"], dft_chan,
                                     seq=t, row0=row0, rows_per_mod=rows_per_mod, emit_kv=not latent,
                                     kv_carry=kv_carry)
    ya = _fourier(xcs.reshape(b, t, -1), fz.reshape(b, t, -1), dft_pos, p["w_four"], layer,
                  bg=b if latent else BG_CTX_FOURIER, tq=min(t, TQ_FOURIER))
    ur3 = ur.reshape(b, t, -1)
    ua3 = ua.reshape(b, t, -1)
    if latent:
        (yb,), s_fin = _retention(ur3, layer, p["dec"], p["gn"], nb=1, emit_state=False,
                                  rope_tabs=rope_tabs, state=state), None
        yc = _lat_attention(ua3, cache_k, cache_v, layer, p["sink"], rope_tabs)
    else:
        yb, s_fin = _retention(ur3, layer, p["dec"], p["gn"], nb=NB_CTX_RET, emit_state=True,
                               state_carry=state_carry)
        yc = _ctx_attention(ua3, layer, p["sink"])
    out = _outproj(x2, ya.reshape(n, -1), yb.reshape(n, -1), yc.reshape(n, -1), layer, mod,
                   p["g_pre"], p["g_post"], p["w_in"], p["w_pa"], p["w_pb"], p["w_pc"], p["w_out"],
                   row0=row0, rows_per_mod=rows_per_mod)
    return out.reshape(b, t, D_MODEL), kv32, s_fin


def kernel(x_prompt, x_sample, cache_k, cache_v, state_ret, c, c_ctx, w_mod, b_mod, g_pre, g_post, w_in,
           w_four, ret_decay, ret_gn, attn_sink, w_branch_a, w_branch_b, w_branch_c, w_out):
    batch, seq, _ = x_prompt.shape
    dec_batch, dec_seq, _ = x_sample.shape
    past = cache_k.shape[2]
    assert 1 + dec_batch <= MOD_ROWS

    cv = jnp.zeros((MOD_ROWS, D_MODEL), F32).at[0].set(c_ctx).at[1:1 + dec_batch].set(c)
    mod = _modulation(cv, w_mod, b_mod)

    dft_chan, dft_ctx = _dft_tables(seq)
    _, dft_lat = _dft_tables(dec_seq)
    rope_tabs = _rope_tables(dec_seq)
    ck = cache_k.reshape(dec_batch, DEPTH, past, ATT_KV_W)
    cvv = cache_v.reshape(dec_batch, DEPTH, past, ATT_KV_W)

    p = dict(
        g_pre=g_pre, g_post=g_post,
        w_in=w_in, w_four=w_four, w_pa=w_branch_a, w_pb=w_branch_b, w_pc=w_branch_c, w_out=w_out,
        dec=ret_decay, gn=ret_gn, sink=attn_sink)

    xp = x_prompt
    kv_all, s_all = (), ()
    for l in range(DEPTH):
        xp, kv_all, s_all = _layer(xp, l, mod, p, latent=False, dft_chan=dft_chan, dft_pos=dft_ctx,
                                   kv_carry=tuple(kv_all), state_carry=s_all)
        s_all = (s_all,)

    xs = x_sample
    for l in range(DEPTH):
        xs, _, _ = _layer(xs, l, mod, p, latent=True, dft_chan=dft_chan, dft_pos=dft_lat,
                             rope_tabs=rope_tabs, cache_k=ck, cache_v=cvv, state=state_ret)

    new_k, new_v = (jnp.transpose(a, (0, 1, 4, 2, 3)) for a in kv_all)
    return (xp, xs, new_k, new_v, s_all[0])
```

```python
import functools
import math

import numpy as np
import jax
import jax.numpy as jnp
from jax import lax
from jax.experimental import pallas as pl
from jax.experimental.pallas import tpu as pltpu

F32 = jnp.float32
BF16 = jnp.bfloat16

D_MODEL = 1024
DEPTH = 2
GRID_W = 64
HEAD_DIM = 64
FOURIER_GROUPS = 4
FOURIER_GROUP_W = 64
FOURIER_W = FOURIER_GROUPS * FOURIER_GROUP_W
RET_HEADS = 4
RET_DK = 64
RET_W = RET_HEADS * RET_DK
RET_CHUNK = 128
ATT_Q_HEADS = 8
ATT_KV_HEADS = 2
ATT_W = ATT_Q_HEADS * HEAD_DIM
ATT_KV_W = ATT_KV_HEADS * HEAD_DIM
WINDOW = 128
ROPE_BASE = 10000.0
EPS = 1e-6
MOD_ROWS = 8
LANES = 128
NEG = -1e30
LOG2E = math.log2(math.e)
VMEM_LIMIT = 56 * 1024 * 1024

C_FX, C_RQ, C_RZ_END = 0, 512, 1536
C_AQ, C_AK, C_AV, C_AZ, C_MG, C_END = 1536, 2048, 2176, 2304, 2816, 5888

TM_INPROJ = 1024
TM_OUTPROJ = 1024
SUB_ROWS = 512
TQ_LATENT = 256
LAT_TILES_PER_STEP = 4
TQ_FOURIER = 512
NB_CTX_RET = 8
NB_CTX_ATTN = 4
BG_CTX_FOURIER = 8
RET_UNROLL = 16


def _sigmoid(x):
    return 0.5 * jnp.tanh(0.5 * x) + 0.5


def _silu(x):
    return x * _sigmoid(x)


def _params(sem):
    return pltpu.CompilerParams(dimension_semantics=sem, vmem_limit_bytes=VMEM_LIMIT)


def _const_spec(shape):
    nd = len(shape)
    return pl.BlockSpec(shape, lambda *_: (0,) * nd, pipeline_mode=pl.Buffered(1))


def _layer_spec(shape, layer, col_block=0):
    idx = (layer,) + (0,) * (len(shape) - 1) + (col_block,)
    return pl.BlockSpec((None,) + tuple(shape), lambda *_: idx, pipeline_mode=pl.Buffered(1))


def _carry_args(carry, first_input, first_output):
    specs = [pl.BlockSpec(memory_space=pl.ANY)] * len(carry)
    aliases = {first_input + k: first_output + k for k in range(len(carry))}
    return specs, list(carry), aliases


def _slab_spec(lead, rest, layer, whole):
    zeros = (0,) * len(rest)
    if whole:
        return pl.BlockSpec((lead, DEPTH) + tuple(rest), lambda i: (i, 0) + zeros)
    return pl.BlockSpec((lead, None) + tuple(rest), lambda i: (i, layer) + zeros)


def _store_slab(ref, lead_idx, layer, whole, value, rest_idx=()):
    if not whole:
        ref[(lead_idx,) + tuple(rest_idx)] = value
        return
    for l in range(DEPTH):
        ref[(lead_idx, l) + tuple(rest_idx)] = value if l == layer else jnp.zeros_like(value)


def _dft_tables(t):
    c = np.arange(FOURIER_GROUP_W)
    ang = 2.0 * np.pi * ((c[:, None] * c[None, :]) % FOURIER_GROUP_W) / FOURIER_GROUP_W
    eye = np.eye(FOURIER_GROUPS)
    s64 = FOURIER_GROUP_W ** -0.5
    chan = np.concatenate([np.kron(eye, np.cos(ang) * s64), np.kron(eye, np.sin(ang) * s64)], axis=1)
    p = np.arange(t)
    angt = 2.0 * np.pi * ((p[:, None] * p[None, :]) % t) / t
    pos = np.concatenate([np.cos(angt), -np.sin(angt)], axis=1) * (t ** -0.5)
    return jnp.asarray(chan, F32).astype(BF16), jnp.asarray(pos, F32).astype(BF16)


def _rope_tables(t):
    quarter = HEAD_DIM // 4
    lane = np.arange(LANES) % HEAD_DIM
    inv = ROPE_BASE ** (-(lane % quarter).astype(np.float64) / quarter)
    n = np.arange(t)
    pos = np.where(lane[None, :] < HEAD_DIM // 2, (n // GRID_W)[:, None], (n % GRID_W)[:, None])
    ang = pos.astype(np.float64) * inv[None, :]
    sign = np.where((lane % (2 * quarter)) < quarter, -1.0, 1.0)
    return jnp.asarray(np.cos(ang), F32), jnp.asarray(np.sin(ang) * sign[None, :], F32)


def _rope(x, cos, sin):
    lane = lax.broadcasted_iota(jnp.int32, x.shape, 1)
    first = (lane & 31) < 16
    partner = jnp.where(first, pltpu.roll(x, LANES - 16, axis=1), pltpu.roll(x, 16, axis=1))
    return x * cos + partner * sin


def _split_bf16(x):
    hi = x.astype(BF16)
    return hi, (x - hi.astype(F32)).astype(BF16)


def _mod_kernel(cv_ref, w_ref, b_ref, o_ref):
    a_hi, a_lo = _split_bf16(_silu(cv_ref[...]))
    w_hi, w_lo = _split_bf16(w_ref[...])
    dot = functools.partial(jnp.dot, preferred_element_type=F32)
    bias = b_ref[pl.ds(pl.program_id(0), 1), :]
    o_ref[...] = dot(a_hi, w_hi) + (dot(a_lo, w_hi) + dot(a_hi, w_lo)) + bias


def _modulation(cv, w_mod, b_mod):
    tn = 1024
    return pl.pallas_call(
        _mod_kernel,
        grid=(DEPTH, 3 * D_MODEL // tn),
        in_specs=[pl.BlockSpec((MOD_ROWS, D_MODEL), lambda l, j: (0, 0)),
                  pl.BlockSpec((None, D_MODEL, tn), lambda l, j: (l, 0, j)),
                  pl.BlockSpec((DEPTH, tn), lambda l, j: (0, j))],
        out_specs=pl.BlockSpec((None, MOD_ROWS, tn), lambda l, j: (l, 0, j)),
        out_shape=jax.ShapeDtypeStruct((DEPTH, MOD_ROWS, 3 * D_MODEL), F32),
        compiler_params=_params(("parallel", "parallel")),
        name="modulation",
    )(cv, w_mod, b_mod)


def _modulated_norm(x, g, mod_ref, tm, row0, rows_per_mod):
    row = row0 + (pl.program_id(0) * tm) // rows_per_mod
    ms = jnp.mean(x * x, axis=-1, keepdims=True)
    y = x * lax.rsqrt(ms + EPS) * g
    shift = mod_ref[pl.ds(row, 1), 0:D_MODEL]
    scale = mod_ref[pl.ds(row, 1), D_MODEL:2 * D_MODEL]
    return (y * (1.0 + scale) + shift).astype(BF16), row


def _inproj_kernel(x_ref, mod_ref, g_ref, w_ref, dft_ref, *refs, layer, tm, seq, row0, rows_per_mod,
                   n_carry):
    xcs_ref, fz_ref, ur_ref, ua_ref, *kv_refs = refs[n_carry:]
    for r in range(tm // SUB_ROWS):
        rows = slice(r * SUB_ROWS, (r + 1) * SUB_ROWS)
        h, _ = _modulated_norm(x_ref[rows, :], g_ref[layer:layer + 1, :], mod_ref, tm, row0, rows_per_mod)

        def mm(c0, c1):
            return jnp.dot(h, w_ref[:, c0:c1].astype(BF16), preferred_element_type=F32)

        f = mm(C_FX, C_RQ)
        xcs_ref[rows, :] = jnp.dot(f[:, :FOURIER_W].astype(BF16), dft_ref[...],
                                   preferred_element_type=F32).astype(BF16)
        fz_ref[rows, :] = f[:, FOURIER_W:].astype(BF16)
        ur_ref[rows, :] = mm(C_RQ, C_RZ_END).astype(BF16)
        ua_ref[rows, 0:ATT_W] = mm(C_AQ, C_AK).astype(BF16)
        ua_ref[rows, ATT_W:2 * ATT_W] = mm(C_AZ, C_MG).astype(BF16)
        kv = mm(C_AK, C_AZ)
        if kv_refs:
            for s in range(SUB_ROWS // seq):
                blk = kv[s * seq:(s + 1) * seq]
                bi = (r * SUB_ROWS) // seq + s
                for ref, cols in zip(kv_refs, (blk[:, :ATT_KV_W], blk[:, ATT_KV_W:])):
                    _store_slab(ref, bi, layer, n_carry == 0,
                                cols.T.reshape(ATT_KV_HEADS, HEAD_DIM, seq))
        ua_ref[rows, 2 * ATT_W:] = kv.astype(BF16)


def _inproj(x2, layer, mod, g_pre, w_in, dft, *, seq, row0, rows_per_mod, emit_kv, kv_carry=()):
    n = x2.shape[0]
    tm = TM_INPROJ
    row_spec = lambda w: pl.BlockSpec((tm, w), lambda i: (i, 0))
    widths = (2 * FOURIER_W, FOURIER_W, 4 * RET_W, 2 * ATT_W + 2 * ATT_KV_W)
    out_shape = [jax.ShapeDtypeStruct((n, w), BF16) for w in widths]
    out_specs = [row_spec(w) for w in widths]
    if emit_kv:
        assert SUB_ROWS % seq == 0
        out_shape += [jax.ShapeDtypeStruct((n // seq, DEPTH, ATT_KV_HEADS, HEAD_DIM, seq), F32)] * 2
        out_specs += [_slab_spec(tm // seq, (ATT_KV_HEADS, HEAD_DIM, seq), layer, not kv_carry)] * 2
    args = [x2, mod, g_pre, w_in, dft]
    carry_specs, carry, aliases = _carry_args(kv_carry, len(args), len(widths))
    return pl.pallas_call(
        functools.partial(_inproj_kernel, layer=layer, tm=tm, seq=seq, row0=row0, rows_per_mod=rows_per_mod,
                          n_carry=len(carry)),
        grid=(n // tm,),
        in_specs=[row_spec(D_MODEL),
                  _layer_spec((MOD_ROWS, 3 * D_MODEL), layer),
                  _const_spec((DEPTH, D_MODEL)),
                  _layer_spec((D_MODEL, C_MG), layer),
                  _const_spec((FOURIER_W, 2 * FOURIER_W))] + carry_specs,
        out_specs=out_specs,
        out_shape=out_shape,
        input_output_aliases=aliases,
        compiler_params=_params(("parallel",)),
        name="inproj",
    )(*args, *carry)


def _fourier_kernel(ct_ref, xcs_ref, fz_ref, w_ref, ya_ref, xcat_ref, *, bg, t):
    @pl.when(pl.program_id(1) == 0)
    def _gather():
        for b in range(bg):
            cols = slice(b * FOURIER_W, (b + 1) * FOURIER_W)
            xcat_ref[0:t, cols] = xcs_ref[b, :, 0:FOURIER_W]
            xcat_ref[t:2 * t, cols] = xcs_ref[b, :, FOURIER_W:]

    yr = jnp.dot(ct_ref[...], xcat_ref[...], preferred_element_type=F32).astype(BF16)
    w = w_ref[...].astype(BF16)
    for b in range(bg):
        ya = jnp.dot(yr[:, b * FOURIER_W:(b + 1) * FOURIER_W], w, preferred_element_type=F32)
        ya_ref[b] = (ya * _silu(fz_ref[b].astype(F32))).astype(BF16)


def _fourier(xcs, fz, ct, w_four, layer, *, bg, tq):
    b, t, _ = xcs.shape
    return pl.pallas_call(
        functools.partial(_fourier_kernel, bg=bg, t=t),
        grid=(b // bg, t // tq),
        in_specs=[pl.BlockSpec((tq, 2 * t), lambda i, j: (j, 0)),
                  pl.BlockSpec((bg, t, 2 * FOURIER_W), lambda i, j: (i, 0, 0)),
                  pl.BlockSpec((bg, tq, FOURIER_W), lambda i, j: (i, j, 0)),
                  _layer_spec((FOURIER_W, FOURIER_W), layer)],
        out_specs=pl.BlockSpec((bg, tq, FOURIER_W), lambda i, j: (i, j, 0)),
        out_shape=jax.ShapeDtypeStruct((b, t, FOURIER_W), BF16),
        scratch_shapes=[pltpu.VMEM((2 * t, bg * FOURIER_W), BF16)],
        compiler_params=_params(("parallel", "arbitrary")),
        name="fourier",
    )(ct, xcs, fz, w_four)


def _log_sigmoid(x):
    return jnp.minimum(x, 0.0) - jnp.log(1.0 + jnp.exp(-jnp.abs(x)))


def _head_blocks(a, width):
    lane = lax.broadcasted_iota(jnp.int32, a.shape, 1)
    zero = jnp.zeros_like(a)
    return jnp.concatenate(
        [jnp.where((lane >= h * width) & (lane < (h + 1) * width), a, zero) for h in range(RET_HEADS)],
        axis=0)


def _per_head(dec_ref, layer, direction, head_of):
    out = jnp.full(head_of.shape, dec_ref[layer, direction, RET_HEADS - 1], F32)
    for h in range(RET_HEADS - 2, -1, -1):
        out = jnp.where(head_of == h, dec_ref[layer, direction, h], out)
    return out


def _ret_kernel(*refs, nb, t, layer, rope, has_s0, n_carry, emit_state):
    refs = list(refs)
    dec_ref = refs.pop(0)
    ur_ref = refs.pop(0)
    cos_ref = refs.pop(0) if rope else None
    sin_ref = refs.pop(0) if rope else None
    s0_ref = refs.pop(0) if has_s0 else None
    gn_ref = refs.pop(0)
    del refs[:n_carry]
    yb_ref = refs.pop(0)
    sfin_ref = refs.pop(0) if emit_state else None
    (kr_ref, dsf_ref, dsb_ref, sfs_ref, sbs_ref, st_ref) = refs
    c = RET_CHUNK
    nc = t // c
    nch = nb * nc
    group = RET_UNROLL

    lane_head = lax.broadcasted_iota(jnp.int32, (1, RET_W), 1) // RET_DK
    lgf = _log_sigmoid(_per_head(dec_ref, layer, 0, lane_head))
    lgb = _log_sigmoid(_per_head(dec_ref, layer, 1, lane_head))
    ri = lax.broadcasted_iota(jnp.int32, (c, RET_W), 0).astype(F32)
    read_f = jnp.exp((ri + 1.0) * lgf)
    read_b = jnp.exp((c - ri) * lgb)
    write_f = jnp.exp((c - 1.0 - ri) * lgf)
    write_b = jnp.exp(ri * lgb)
    carry_f = jnp.exp(c * lgf)
    carry_b = jnp.exp(c * lgb)
    ii = lax.broadcasted_iota(jnp.int32, (RET_HEADS * c, c), 0)
    jj = lax.broadcasted_iota(jnp.int32, (RET_HEADS * c, c), 1)
    row_head = ii // c
    diff = ((ii & (c - 1)) - jj).astype(F32)
    decay = (jnp.where(diff >= 0, jnp.exp(jnp.maximum(diff, 0.0)
                                          * _log_sigmoid(_per_head(dec_ref, layer, 0, row_head))), 0.0)
             + jnp.where(diff <= 0, jnp.exp(jnp.maximum(-diff, 0.0)
                                            * _log_sigmoid(_per_head(dec_ref, layer, 1, row_head))), 0.0))
    r2 = lax.broadcasted_iota(jnp.int32, (RET_W, RET_W), 0)
    c2 = lax.broadcasted_iota(jnp.int32, (RET_W, RET_W), 1)
    same_head = (r2 // RET_DK) == (c2 // RET_DK)
    group_mean = jnp.where(same_head, 1.0 / RET_DK, 0.0).astype(BF16)
    out_head = lax.broadcasted_iota(jnp.int32, (c, RET_W), 1) // RET_DK

    def chunk_pos(ci):
        return ci // nc, pl.multiple_of((ci % nc) * c, c)

    def load_qk(col, b, r0):
        a = ur_ref[b, pl.ds(r0, c), col:col + RET_W].astype(F32)
        if rope:
            cs = cos_ref[pl.ds(r0, c), :]
            sn = sin_ref[pl.ds(r0, c), :]
            a = jnp.concatenate([_rope(a[:, :LANES], cs, sn), _rope(a[:, LANES:], cs, sn)], axis=1)
        return a

    def increments(ci, carry):
        b, r0 = chunk_pos(ci)
        k = load_qk(RET_W, b, r0) * (RET_DK ** -0.5)
        v = ur_ref[b, pl.ds(r0, c), 2 * RET_W:3 * RET_W]
        kr_ref[pl.ds(pl.multiple_of(ci * c, c), c), :] = k.astype(BF16)
        kw = jnp.concatenate([(k * write_f).astype(BF16), (k * write_b).astype(BF16)], axis=1)
        d = lax.dot_general(kw, v, (((0,), (0,)), ((), ())), preferred_element_type=F32)
        dsf_ref[ci] = jnp.where(same_head, d[:RET_W], 0.0)
        dsb_ref[ci] = jnp.where(same_head, d[RET_W:], 0.0)
        return carry

    lax.fori_loop(0, nch, increments, 0, unroll=group)

    def scan(b, direction, ds_ref, out_ref, carry_decay):
        st_ref[...] = jnp.zeros((RET_W, RET_W), F32)
        if has_s0:
            for h in range(RET_HEADS):
                sl = slice(h * RET_DK, (h + 1) * RET_DK)
                st_ref[sl, sl] = s0_ref[b, direction, h]

        def step(n, carry):
            ci = b * nc + (n if direction == 0 else nc - 1 - n)
            out_ref[ci] = st_ref[...].astype(BF16)
            st_ref[...] = carry_decay * st_ref[...] + ds_ref[ci]
            return carry

        lax.fori_loop(0, nc, step, 0, unroll=min(nc, 4))
        if emit_state:
            for h in range(RET_HEADS):
                sl = slice(h * RET_DK, (h + 1) * RET_DK)
                _store_slab(sfin_ref, b, layer, n_carry == 0, st_ref[sl, sl], (direction, h))

    for b in range(nb):
        scan(b, 0, dsf_ref, sfs_ref, carry_f)
        scan(b, 1, dsb_ref, sbs_ref, carry_b)

    def chunk_output(ci, b, r0):
        q = load_qk(0, b, r0)
        kb = kr_ref[pl.ds(pl.multiple_of(ci * c, c), c), :]
        v = ur_ref[b, pl.ds(r0, c), 2 * RET_W:3 * RET_W]
        att = lax.dot_general(_head_blocks(q.astype(BF16), RET_DK), kb,
                              (((1,), (1,)), ((), ())), preferred_element_type=F32)
        o4 = jnp.dot((att * decay).astype(BF16), v, preferred_element_type=F32)
        o = o4[(RET_HEADS - 1) * c:]
        for h in range(RET_HEADS - 2, -1, -1):
            o = jnp.where(out_head == h, o4[h * c:(h + 1) * c], o)
        o = o + jnp.dot((q * read_f).astype(BF16), sfs_ref[ci], preferred_element_type=F32)
        return o + jnp.dot((q * read_b).astype(BF16), sbs_ref[ci], preferred_element_type=F32)

    def outputs(it, carry):
        pos = [chunk_pos(it * group + u) for u in range(group)]
        o = jnp.concatenate([chunk_output(it * group + u, *pos[u]) for u in range(group)], axis=0)
        ms = jnp.dot((o * o).astype(BF16), group_mean, preferred_element_type=F32)
        y = o * lax.rsqrt(ms + EPS) * gn_ref[layer:layer + 1, :]
        for u, (b, r0) in enumerate(pos):
            z = ur_ref[b, pl.ds(r0, c), 3 * RET_W:4 * RET_W].astype(F32)
            yb_ref[b, pl.ds(r0, c), :] = (y[u * c:(u + 1) * c] * _silu(z)).astype(BF16)
        return carry

    lax.fori_loop(0, nch // group, outputs, 0)


def _retention(ur, layer, dec, gn, *, nb, emit_state, rope_tabs=None, state=None, state_carry=()):
    b, t, _ = ur.shape
    nch = nb * (t // RET_CHUNK)
    assert nch % RET_UNROLL == 0
    rope = rope_tabs is not None
    has_s0 = state is not None
    in_specs = [pl.BlockSpec(memory_space=pltpu.SMEM),
                pl.BlockSpec((nb, t, 4 * RET_W), lambda i: (i, 0, 0))]
    args = [dec, ur]
    if rope:
        in_specs += [_const_spec((t, LANES)), _const_spec((t, LANES))]
        args += list(rope_tabs)
    if has_s0:
        in_specs.append(pl.BlockSpec((nb, None, 2, RET_HEADS, RET_DK, RET_DK),
                                     lambda i: (i, layer, 0, 0, 0, 0)))
        args.append(state)
    in_specs.append(_const_spec((DEPTH, RET_W)))
    args.append(gn)
    carry_specs, carry, aliases = _carry_args(state_carry, len(args), 1)
    state_scratch = lambda dt: pltpu.VMEM((nch, RET_W, RET_W), dt)
    out_specs = [pl.BlockSpec((nb, t, RET_W), lambda i: (i, 0, 0))]
    out_shape = [jax.ShapeDtypeStruct((b, t, RET_W), BF16)]
    if emit_state:
        out_specs.append(_slab_spec(nb, (2, RET_HEADS, RET_DK, RET_DK), layer, not state_carry))
        out_shape.append(jax.ShapeDtypeStruct((b, DEPTH, 2, RET_HEADS, RET_DK, RET_DK), F32))
    return pl.pallas_call(
        functools.partial(_ret_kernel, nb=nb, t=t, layer=layer, rope=rope, has_s0=has_s0,
                          n_carry=len(carry), emit_state=emit_state),
        grid=(b // nb,),
        in_specs=in_specs + carry_specs,
        out_specs=out_specs,
        out_shape=out_shape,
        input_output_aliases=aliases,
        scratch_shapes=[pltpu.VMEM((nb * t, RET_W), BF16),
                        state_scratch(F32), state_scratch(F32), state_scratch(BF16), state_scratch(BF16),
                        pltpu.VMEM((RET_W, RET_W), F32)],
        compiler_params=_params(("parallel",)),
        name="retention",
    )(*args, *carry)


def _kv_variants(a, ones_block):
    lane_half = lax.broadcasted_iota(jnp.int32, a.shape, 1) // HEAD_DIM
    swapped = pltpu.roll(a, HEAD_DIM, axis=1)
    out = []
    for g in range(ATT_KV_HEADS):
        row = []
        for half in range(2):
            var = jnp.where(lane_half == half, a if half == g else swapped, 0.0).astype(BF16)
            if ones_block:
                ones = jnp.where(lane_half == half, 1.0, 0.0).astype(BF16)
                var = jnp.concatenate([var, ones], axis=1)
            row.append(var)
        out.append(row)
    return out


def _attend_group(q2, keys, values, masks, sink_ref, layer, g, z2):
    rows = q2.shape[0]
    upper = lax.broadcasted_iota(jnp.int32, (rows, 1), 0) < rows // 2
    acc = None
    sink_terms = []
    for half in range(2):
        sink = jnp.where(upper, sink_ref[layer, g, half], sink_ref[layer, g, 2 + half]) * LOG2E
        logits = []
        for kpart, mask in zip(keys[half], masks):
            l = lax.dot_general(q2, kpart, (((1,), (1,)), ((), ())), preferred_element_type=F32)
            logits.append(l if mask is None else jnp.where(mask, l, NEG))
        m = sink
        for l in logits:
            m = jnp.maximum(m, jnp.max(l, axis=-1, keepdims=True))
        for l, vpart in zip(logits, values[half]):
            pv = jnp.dot(jnp.exp2(l - m).astype(BF16), vpart, preferred_element_type=F32)
            acc = pv if acc is None else acc + pv
        sink_terms.append(jnp.exp2(sink - m))
    lane = lax.broadcasted_iota(jnp.int32, (rows, LANES), 1)
    den = acc[:, LANES:] + jnp.where(lane < HEAD_DIM, sink_terms[0], sink_terms[1])
    return (acc[:, :LANES] * (1.0 / den) * _silu(z2)).astype(BF16)


def _pair_rows(ref_slice, g):
    return jnp.concatenate([ref_slice(2 * g), ref_slice(2 * g + 1)], axis=0)


def _ctx_attn_kernel(sink_ref, ua_ref, yc_ref, *, nb, t, layer):
    for b in range(nb):
        kv = ua_ref[b, :, 2 * ATT_W:].astype(F32)
        kvar = _kv_variants(kv[:, :LANES] * (HEAD_DIM ** -0.5 * LOG2E), False)
        vvar = _kv_variants(kv[:, LANES:], True)
        for g in range(ATT_KV_HEADS):
            q2 = _pair_rows(lambda p: ua_ref[b, :, p * LANES:(p + 1) * LANES], g)
            z2 = _pair_rows(lambda p: ua_ref[b, :, ATT_W + p * LANES:ATT_W + (p + 1) * LANES], g)
            o = _attend_group(q2, [[kvar[g][0]], [kvar[g][1]]], [[vvar[g][0]], [vvar[g][1]]], [None],
                              sink_ref, layer, g, z2.astype(F32))
            yc_ref[b, :, 2 * g * LANES:(2 * g + 1) * LANES] = o[:t]
            yc_ref[b, :, (2 * g + 1) * LANES:(2 * g + 2) * LANES] = o[t:]


def _ctx_attention(ua, layer, sink):
    b, t, w = ua.shape
    nb = NB_CTX_ATTN
    return pl.pallas_call(
        functools.partial(_ctx_attn_kernel, nb=nb, t=t, layer=layer),
        grid=(b // nb,),
        in_specs=[pl.BlockSpec(memory_space=pltpu.SMEM),
                  pl.BlockSpec((nb, t, w), lambda i: (i, 0, 0))],
        out_specs=pl.BlockSpec((nb, t, ATT_W), lambda i: (i, 0, 0)),
        out_shape=jax.ShapeDtypeStruct((b, t, ATT_W), BF16),
        compiler_params=_params(("parallel",)),
        name="ctx_attention",
    )(sink, ua)


def _lat_attn_kernel(sink_ref, qz_ref, kv_ref, ck_ref, cv_ref, cos_ref, sin_ref, yc_ref,
                     kl_ref, vl_ref, kc_ref, vc_ref, *, t, tq, tiles, layer):
    j = pl.program_id(1)
    scale = HEAD_DIM ** -0.5 * LOG2E

    @pl.when(j == 0)
    def _prepare():
        kv = kv_ref[...].astype(F32)
        kvar = _kv_variants(_rope(kv[:, :LANES], cos_ref[...], sin_ref[...]) * scale, False)
        vvar = _kv_variants(kv[:, LANES:], True)
        cvar = _kv_variants(ck_ref[...] * scale, False)
        dvar = _kv_variants(cv_ref[...], True)
        for g in range(ATT_KV_HEADS):
            for half in range(2):
                i = 2 * g + half
                for ref, var in ((kl_ref, kvar), (vl_ref, vvar)):
                    pad = jnp.zeros((WINDOW, ref.shape[-1]), BF16)
                    ref[i, 0:WINDOW, :] = pad
                    ref[i, WINDOW:WINDOW + t, :] = var[g][half]
                    ref[i, WINDOW + t:, :] = pad
                kc_ref[i] = cvar[g][half]
                vc_ref[i] = dvar[g][half]

    nloc = tq + 2 * WINDOW
    rr = lax.broadcasted_iota(jnp.int32, (2 * tq, nloc), 0) & (tq - 1)
    ss = lax.broadcasted_iota(jnp.int32, (2 * tq, nloc), 1)
    for u in range(tiles):
        rows = slice(u * tq, (u + 1) * tq)
        first = (j * tiles + u) * tq
        r0 = pl.multiple_of(first, tq)
        band = ((ss - rr >= 0) & (ss - rr <= 2 * WINDOW)
                & (ss >= WINDOW - first) & (ss < t + WINDOW - first))
        cs = cos_ref[pl.ds(r0, tq), :]
        sn = sin_ref[pl.ds(r0, tq), :]
        for g in range(ATT_KV_HEADS):
            q2 = _pair_rows(
                lambda p: _rope(qz_ref[rows, p * LANES:(p + 1) * LANES].astype(F32), cs, sn).astype(BF16), g)
            z2 = _pair_rows(lambda p: qz_ref[rows, ATT_W + p * LANES:ATT_W + (p + 1) * LANES], g)
            keys = [[kl_ref[2 * g + half, pl.ds(r0, nloc), :], kc_ref[2 * g + half]] for half in range(2)]
            vals = [[vl_ref[2 * g + half, pl.ds(r0, nloc), :], vc_ref[2 * g + half]] for half in range(2)]
            o = _attend_group(q2, keys, vals, [band, None], sink_ref, layer, g, z2.astype(F32))
            yc_ref[rows, 2 * g * LANES:(2 * g + 1) * LANES] = o[:tq]
            yc_ref[rows, (2 * g + 1) * LANES:(2 * g + 2) * LANES] = o[tq:]


def _lat_attention(ua, cache_k, cache_v, layer, sink, rope_tabs):
    b, t, _ = ua.shape
    past = cache_k.shape[2]
    tq = TQ_LATENT
    tiles = LAT_TILES_PER_STEP
    nvar = 2 * ATT_KV_HEADS
    cache_spec = pl.BlockSpec((None, None, past, ATT_KV_W), lambda i, j: (i, layer, 0, 0))
    return pl.pallas_call(
        functools.partial(_lat_attn_kernel, t=t, tq=tq, tiles=tiles, layer=layer),
        grid=(b, t // (tiles * tq)),
        in_specs=[pl.BlockSpec(memory_space=pltpu.SMEM),
                  pl.BlockSpec((None, tiles * tq, 2 * ATT_W), lambda i, j: (i, j, 0)),
                  pl.BlockSpec((None, t, 2 * ATT_KV_W), lambda i, j: (i, 0, 2 * ATT_W // (2 * ATT_KV_W))),
                  cache_spec, cache_spec,
                  _const_spec((t, LANES)), _const_spec((t, LANES))],
        out_specs=pl.BlockSpec((None, tiles * tq, ATT_W), lambda i, j: (i, j, 0)),
        out_shape=jax.ShapeDtypeStruct((b, t, ATT_W), BF16),
        scratch_shapes=[pltpu.VMEM((nvar, t + 2 * WINDOW, LANES), BF16),
                        pltpu.VMEM((nvar, t + 2 * WINDOW, 2 * LANES), BF16),
                        pltpu.VMEM((nvar, past, LANES), BF16),
                        pltpu.VMEM((nvar, past, 2 * LANES), BF16)],
        compiler_params=_params(("parallel", "arbitrary")),
        name="lat_attention",
    )(sink, ua, ua, cache_k, cache_v, *rope_tabs)


def _outproj_kernel(x_ref, ya_ref, yb_ref, yc_ref, mod_ref, gpre_ref, gpost_ref,
                    wg_ref, wgt_ref, wa_ref, wb_ref, wc_ref, wo_ref, o_ref, *, layer, tm, row0, rows_per_mod):
    for r in range(tm // SUB_ROWS):
        rows = slice(r * SUB_ROWS, (r + 1) * SUB_ROWS)
        x = x_ref[rows, :]
        h, row = _modulated_norm(x, gpre_ref[layer:layer + 1, :], mod_ref, tm, row0, rows_per_mod)

        def gate_logits(c):
            lo, hi = c * D_MODEL, min((c + 1) * D_MODEL, C_MG)
            mg = jnp.dot(h, wg_ref[:, lo:hi].astype(BF16), preferred_element_type=F32)
            if hi - lo < D_MODEL:
                tail = jnp.dot(h, wgt_ref[...].astype(BF16), preferred_element_type=F32)
                mg = jnp.concatenate([mg, tail], axis=1)
            return mg

        merged = None
        for c, (y_ref, w_ref) in enumerate(((ya_ref, wa_ref), (yb_ref, wb_ref), (yc_ref, wc_ref))):
            term = _sigmoid(gate_logits(c)) * jnp.dot(y_ref[rows, :], w_ref[...].astype(BF16),
                                                      preferred_element_type=F32)
            merged = term if merged is None else merged + term
        out = jnp.dot(merged.astype(BF16), wo_ref[...].astype(BF16), preferred_element_type=F32)
        ms = jnp.mean(out * out, axis=-1, keepdims=True)
        normed = out * lax.rsqrt(ms + EPS) * gpost_ref[layer:layer + 1, :]
        gate = mod_ref[pl.ds(row, 1), 2 * D_MODEL:3 * D_MODEL]
        o_ref[rows, :] = x + gate * normed


def _outproj(x2, ya, yb, yc, layer, mod, g_pre, g_post, w_in, wa, wb, wc, wo, *, row0, rows_per_mod):
    n = x2.shape[0]
    tm = TM_OUTPROJ
    row_spec = lambda w: pl.BlockSpec((tm, w), lambda i: (i, 0))
    return pl.pallas_call(
        functools.partial(_outproj_kernel, layer=layer, tm=tm, row0=row0, rows_per_mod=rows_per_mod),
        grid=(n // tm,),
        in_specs=[row_spec(D_MODEL), row_spec(FOURIER_W), row_spec(RET_W), row_spec(ATT_W),
                  _layer_spec((MOD_ROWS, 3 * D_MODEL), layer),
                  _const_spec((DEPTH, D_MODEL)), _const_spec((DEPTH, D_MODEL)),
                  _layer_spec((D_MODEL, C_MG), layer, col_block=1),
                  _layer_spec((D_MODEL, C_END - 2 * C_MG), layer, col_block=2 * C_MG // (C_END - 2 * C_MG)),
                  _layer_spec((FOURIER_W, D_MODEL), layer), _layer_spec((RET_W, D_MODEL), layer),
                  _layer_spec((ATT_W, D_MODEL), layer), _layer_spec((D_MODEL, D_MODEL), layer)],
        out_specs=row_spec(D_MODEL),
        out_shape=jax.ShapeDtypeStruct((n, D_MODEL), F32),
        compiler_params=_params(("parallel",)),
        name="outproj",
    )(x2, ya, yb, yc, mod, g_pre, g_post, w_in, w_in, wa, wb, wc, wo)


def _layer(x, layer, mod, p, *, latent, dft_chan, dft_pos, rope_tabs=None, cache_k=None, cache_v=None,
           state=None, kv_carry=(), state_carry=()):
    b, t, _ = x.shape
    n = b * t
    row0, rows_per_mod = (1, t) if latent else (0, n)
    x2 = x.reshape(n, D_MODEL)
    xcs, fz, ur, ua, *kv32 = _inproj(x2, layer, mod, p["g_pre"], p["w_in"], dft_chan,
                                     seq=t, row0=row0, rows_per_mod=rows_per_mod, emit_kv=not latent,
                                     kv_carry=kv_carry)
    ya = _fourier(xcs.reshape(b, t, -1), fz.reshape(b, t, -1), dft_pos, p["w_four"], layer,
                  bg=b if latent else BG_CTX_FOURIER, tq=min(t, TQ_FOURIER))
    ur3 = ur.reshape(b, t, -1)
    ua3 = ua.reshape(b, t, -1)
    if latent:
        (yb,), s_fin = _retention(ur3, layer, p["dec"], p["gn"], nb=1, emit_state=False,
                                  rope_tabs=rope_tabs, state=state), None
        yc = _lat_attention(ua3, cache_k, cache_v, layer, p["sink"], rope_tabs)
    else:
        yb, s_fin = _retention(ur3, layer, p["dec"], p["gn"], nb=NB_CTX_RET, emit_state=True,
                               state_carry=state_carry)
        yc = _ctx_attention(ua3, layer, p["sink"])
    out = _outproj(x2, ya.reshape(n, -1), yb.reshape(n, -1), yc.reshape(n, -1), layer, mod,
                   p["g_pre"], p["g_post"], p["w_in"], p["w_pa"], p["w_pb"], p["w_pc"], p["w_out"],
                   row0=row0, rows_per_mod=rows_per_mod)
    return out.reshape(b, t, D_MODEL), kv32, s_fin


def kernel(x_prompt, x_sample, cache_k, cache_v, state_ret, c, c_ctx, w_mod, b_mod, g_pre, g_post, w_in,
           w_four, ret_decay, ret_gn, attn_sink, w_branch_a, w_branch_b, w_branch_c, w_out):
    batch, seq, _ = x_prompt.shape
    dec_batch, dec_seq, _ = x_sample.shape
    past = cache_k.shape[2]
    assert 1 + dec_batch <= MOD_ROWS

    cv = jnp.zeros((MOD_ROWS, D_MODEL), F32).at[0].set(c_ctx).at[1:1 + dec_batch].set(c)
    mod = _modulation(cv, w_mod, b_mod)

    dft_chan, dft_ctx = _dft_tables(seq)
    _, dft_lat = _dft_tables(dec_seq)
    rope_tabs = _rope_tables(dec_seq)
    ck = cache_k.reshape(dec_batch, DEPTH, past, ATT_KV_W)
    cvv = cache_v.reshape(dec_batch, DEPTH, past, ATT_KV_W)

    p = dict(
        g_pre=g_pre, g_post=g_post,
        w_in=w_in, w_four=w_four, w_pa=w_branch_a, w_pb=w_branch_b, w_pc=w_branch_c, w_out=w_out,
        dec=ret_decay, gn=ret_gn, sink=attn_sink)

    xp = x_prompt
    kv_all, s_all = (), ()
    for l in range(DEPTH):
        xp, kv_all, s_all = _layer(xp, l, mod, p, latent=False, dft_chan=dft_chan, dft_pos=dft_ctx,
                                   kv_carry=tuple(kv_all), state_carry=s_all)
        s_all = (s_all,)

    xs = x_sample
    for l in range(DEPTH):
        xs, _, _ = _layer(xs, l, mod, p, latent=True, dft_chan=dft_chan, dft_pos=dft_lat,
                             rope_tabs=rope_tabs, cache_k=ck, cache_v=cvv, state=state_ret)

    new_k, new_v = (jnp.transpose(a, (0, 1, 4, 2, 3)) for a in kv_all)
    return (xp, xs, new_k, new_v, s_all[0])
```

```python
import functools
import math

import numpy as np
import jax
import jax.numpy as jnp
from jax import lax
from jax.experimental import pallas as pl
from jax.experimental.pallas import tpu as pltpu

F32 = jnp.float32
BF16 = jnp.bfloat16

D_MODEL = 1024
DEPTH = 2
GRID_W = 64
HEAD_DIM = 64
FOURIER_GROUPS = 4
FOURIER_GROUP_W = 64
FOURIER_W = FOURIER_GROUPS * FOURIER_GROUP_W
RET_HEADS = 4
RET_DK = 64
RET_W = RET_HEADS * RET_DK
RET_CHUNK = 128
ATT_Q_HEADS = 8
ATT_KV_HEADS = 2
ATT_W = ATT_Q_HEADS * HEAD_DIM
ATT_KV_W = ATT_KV_HEADS * HEAD_DIM
WINDOW = 128
ROPE_BASE = 10000.0
EPS = 1e-6
MOD_ROWS = 8
LANES = 128
NEG = -1e30
LOG2E = math.log2(math.e)
VMEM_LIMIT = 56 * 1024 * 1024

C_FX, C_RQ, C_RZ_END = 0, 512, 1536
C_AQ, C_AK, C_AV, C_AZ, C_MG, C_END = 1536, 2048, 2176, 2304, 2816, 5888

TM_INPROJ = 1024
TM_OUTPROJ = 1024
SUB_ROWS = 512
TQ_LATENT = 256
LAT_TILES_PER_STEP = 4
TQ_FOURIER = 512
NB_CTX_RET = 8
NB_CTX_ATTN = 4
BG_CTX_FOURIER = 8
RET_UNROLL = 16


def _sigmoid(x):
    return 0.5 * jnp.tanh(0.5 * x) + 0.5


def _silu(x):
    return x * _sigmoid(x)


def _params(sem):
    return pltpu.CompilerParams(dimension_semantics=sem, vmem_limit_bytes=VMEM_LIMIT)


def _const_spec(shape):
    nd = len(shape)
    return pl.BlockSpec(shape, lambda *_: (0,) * nd, pipeline_mode=pl.Buffered(1))


def _layer_spec(shape, layer, col_block=0):
    idx = (layer,) + (0,) * (len(shape) - 1) + (col_block,)
    return pl.BlockSpec((None,) + tuple(shape), lambda *_: idx, pipeline_mode=pl.Buffered(1))


def _carry_args(carry, first_input, first_output):
    specs = [pl.BlockSpec(memory_space=pl.ANY)] * len(carry)
    aliases = {first_input + k: first_output + k for k in range(len(carry))}
    return specs, list(carry), aliases


def _slab_spec(lead, rest, layer, whole):
    zeros = (0,) * len(rest)
    if whole:
        return pl.BlockSpec((lead, DEPTH) + tuple(rest), lambda i: (i, 0) + zeros)
    return pl.BlockSpec((lead, None) + tuple(rest), lambda i: (i, layer) + zeros)


def _store_slab(ref, lead_idx, layer, whole, value, rest_idx=()):
    if not whole:
        ref[(lead_idx,) + tuple(rest_idx)] = value
        return
    for l in range(DEPTH):
        ref[(lead_idx, l) + tuple(rest_idx)] = value if l == layer else jnp.zeros_like(value)


def _dft_tables(t):
    c = np.arange(FOURIER_GROUP_W)
    ang = 2.0 * np.pi * ((c[:, None] * c[None, :]) % FOURIER_GROUP_W) / FOURIER_GROUP_W
    eye = np.eye(FOURIER_GROUPS)
    s64 = FOURIER_GROUP_W ** -0.5
    chan = np.concatenate([np.kron(eye, np.cos(ang) * s64), np.kron(eye, np.sin(ang) * s64)], axis=1)
    p = np.arange(t)
    angt = 2.0 * np.pi * ((p[:, None] * p[None, :]) % t) / t
    pos = np.concatenate([np.cos(angt), -np.sin(angt)], axis=1) * (t ** -0.5)
    return jnp.asarray(chan, F32).astype(BF16), jnp.asarray(pos, F32).astype(BF16)


def _rope_tables(t):
    quarter = HEAD_DIM // 4
    lane = np.arange(LANES) % HEAD_DIM
    inv = ROPE_BASE ** (-(lane % quarter).astype(np.float64) / quarter)
    n = np.arange(t)
    pos = np.where(lane[None, :] < HEAD_DIM // 2, (n // GRID_W)[:, None], (n % GRID_W)[:, None])
    ang = pos.astype(np.float64) * inv[None, :]
    sign = np.where((lane % (2 * quarter)) < quarter, -1.0, 1.0)
    return jnp.asarray(np.cos(ang), F32), jnp.asarray(np.sin(ang) * sign[None, :], F32)


def _rope(x, cos, sin):
    lane = lax.broadcasted_iota(jnp.int32, x.shape, 1)
    first = (lane & 31) < 16
    partner = jnp.where(first, pltpu.roll(x, LANES - 16, axis=1), pltpu.roll(x, 16, axis=1))
    return x * cos + partner * sin


def _split_bf16(x):
    hi = x.astype(BF16)
    return hi, (x - hi.astype(F32)).astype(BF16)


def _mod_kernel(cv_ref, w_ref, b_ref, o_ref):
    a_hi, a_lo = _split_bf16(_silu(cv_ref[...]))
    w_hi, w_lo = _split_bf16(w_ref[...])
    dot = functools.partial(jnp.dot, preferred_element_type=F32)
    bias = b_ref[pl.ds(pl.program_id(0), 1), :]
    o_ref[...] = dot(a_hi, w_hi) + (dot(a_lo, w_hi) + dot(a_hi, w_lo)) + bias


def _modulation(cv, w_mod, b_mod):
    tn = 1024
    return pl.pallas_call(
        _mod_kernel,
        grid=(DEPTH, 3 * D_MODEL // tn),
        in_specs=[pl.BlockSpec((MOD_ROWS, D_MODEL), lambda l, j: (0, 0)),
                  pl.BlockSpec((None, D_MODEL, tn), lambda l, j: (l, 0, j)),
                  pl.BlockSpec((DEPTH, tn), lambda l, j: (0, j))],
        out_specs=pl.BlockSpec((None, MOD_ROWS, tn), lambda l, j: (l, 0, j)),
        out_shape=jax.ShapeDtypeStruct((DEPTH, MOD_ROWS, 3 * D_MODEL), F32),
        compiler_params=_params(("parallel", "parallel")),
        name="modulation",
    )(cv, w_mod, b_mod)


def _modulated_norm(x, g, mod_ref, tm, row0, rows_per_mod):
    row = row0 + (pl.program_id(0) * tm) // rows_per_mod
    ms = jnp.mean(x * x, axis=-1, keepdims=True)
    y = x * lax.rsqrt(ms + EPS) * g
    shift = mod_ref[pl.ds(row, 1), 0:D_MODEL]
    scale = mod_ref[pl.ds(row, 1), D_MODEL:2 * D_MODEL]
    return (y * (1.0 + scale) + shift).astype(BF16), row


def _inproj_kernel(x_ref, mod_ref, g_ref, w_ref, dft_ref, *refs, layer, tm, seq, row0, rows_per_mod,
                   n_carry, rope):
    cos_ref, sin_ref = refs[:2] if rope else (None, None)
    refs = refs[2 if rope else 0:]
    xcs_ref, fz_ref, ur_ref, ua_ref, *kv_refs = refs[n_carry:]
    for r in range(tm // SUB_ROWS):
        rows = slice(r * SUB_ROWS, (r + 1) * SUB_ROWS)
        h, _ = _modulated_norm(x_ref[rows, :], g_ref[layer:layer + 1, :], mod_ref, tm, row0, rows_per_mod)

        def mm(c0, c1):
            return jnp.dot(h, w_ref[:, c0:c1].astype(BF16), preferred_element_type=F32)

        def roped(a):
            if not rope:
                return a.astype(BF16)
            cs, sn = cos_ref[rows, :], sin_ref[rows, :]
            return jnp.concatenate([_rope(a[:, c:c + LANES], cs, sn) for c in range(0, a.shape[1], LANES)],
                                   axis=1).astype(BF16)

        f = mm(C_FX, C_RQ)
        xcs_ref[rows, :] = jnp.dot(f[:, :FOURIER_W].astype(BF16), dft_ref[...],
                                   preferred_element_type=F32).astype(BF16)
        fz_ref[rows, :] = f[:, FOURIER_W:].astype(BF16)
        ur = mm(C_RQ, C_RZ_END)
        ur_ref[rows, 0:2 * RET_W] = roped(ur[:, :2 * RET_W])
        ur_ref[rows, 2 * RET_W:] = ur[:, 2 * RET_W:].astype(BF16)
        ua_ref[rows, 0:ATT_W] = roped(mm(C_AQ, C_AK))
        ua_ref[rows, ATT_W:2 * ATT_W] = mm(C_AZ, C_MG).astype(BF16)
        kv = mm(C_AK, C_AZ)
        if kv_refs:
            for s in range(SUB_ROWS // seq):
                blk = kv[s * seq:(s + 1) * seq]
                bi = (r * SUB_ROWS) // seq + s
                for ref, cols in zip(kv_refs, (blk[:, :ATT_KV_W], blk[:, ATT_KV_W:])):
                    _store_slab(ref, bi, layer, n_carry == 0,
                                cols.T.reshape(ATT_KV_HEADS, HEAD_DIM, seq))
        ua_ref[rows, 2 * ATT_W:2 * ATT_W + ATT_KV_W] = roped(kv[:, :ATT_KV_W])
        ua_ref[rows, 2 * ATT_W + ATT_KV_W:] = kv[:, ATT_KV_W:].astype(BF16)


def _inproj(x2, layer, mod, g_pre, w_in, dft, *, seq, row0, rows_per_mod, emit_kv, kv_carry=(),
            rope_tabs=()):
    n = x2.shape[0]
    tm = TM_INPROJ
    row_spec = lambda w: pl.BlockSpec((tm, w), lambda i: (i, 0))
    widths = (2 * FOURIER_W, FOURIER_W, 4 * RET_W, 2 * ATT_W + 2 * ATT_KV_W)
    out_shape = [jax.ShapeDtypeStruct((n, w), BF16) for w in widths]
    out_specs = [row_spec(w) for w in widths]
    if emit_kv:
        assert SUB_ROWS % seq == 0
        out_shape += [jax.ShapeDtypeStruct((n // seq, DEPTH, ATT_KV_HEADS, HEAD_DIM, seq), F32)] * 2
        out_specs += [_slab_spec(tm // seq, (ATT_KV_HEADS, HEAD_DIM, seq), layer, not kv_carry)] * 2
    args = [x2, mod, g_pre, w_in, dft, *rope_tabs]
    rope_specs = [pl.BlockSpec((tm, LANES), lambda i: (i % (seq // tm), 0))] * len(rope_tabs)
    carry_specs, carry, aliases = _carry_args(kv_carry, len(args), len(widths))
    return pl.pallas_call(
        functools.partial(_inproj_kernel, layer=layer, tm=tm, seq=seq, row0=row0, rows_per_mod=rows_per_mod,
                          n_carry=len(carry), rope=bool(rope_tabs)),
        grid=(n // tm,),
        in_specs=[row_spec(D_MODEL),
                  _layer_spec((MOD_ROWS, 3 * D_MODEL), layer),
                  _const_spec((DEPTH, D_MODEL)),
                  _layer_spec((D_MODEL, C_MG), layer),
                  _const_spec((FOURIER_W, 2 * FOURIER_W))] + rope_specs + carry_specs,
        out_specs=out_specs,
        out_shape=out_shape,
        input_output_aliases=aliases,
        compiler_params=_params(("parallel",)),
        name="inproj",
    )(*args, *carry)


def _fourier_kernel(ct_ref, xcs_ref, fz_ref, w_ref, ya_ref, xcat_ref, *, bg, t):
    @pl.when(pl.program_id(1) == 0)
    def _gather():
        for b in range(bg):
            cols = slice(b * FOURIER_W, (b + 1) * FOURIER_W)
            xcat_ref[0:t, cols] = xcs_ref[b, :, 0:FOURIER_W]
            xcat_ref[t:2 * t, cols] = xcs_ref[b, :, FOURIER_W:]

    yr = jnp.dot(ct_ref[...], xcat_ref[...], preferred_element_type=F32).astype(BF16)
    w = w_ref[...].astype(BF16)
    for b in range(bg):
        ya = jnp.dot(yr[:, b * FOURIER_W:(b + 1) * FOURIER_W], w, preferred_element_type=F32)
        ya_ref[b] = (ya * _silu(fz_ref[b].astype(F32))).astype(BF16)


def _fourier(xcs, fz, ct, w_four, layer, *, bg, tq):
    b, t, _ = xcs.shape
    return pl.pallas_call(
        functools.partial(_fourier_kernel, bg=bg, t=t),
        grid=(b // bg, t // tq),
        in_specs=[pl.BlockSpec((tq, 2 * t), lambda i, j: (j, 0)),
                  pl.BlockSpec((bg, t, 2 * FOURIER_W), lambda i, j: (i, 0, 0)),
                  pl.BlockSpec((bg, tq, FOURIER_W), lambda i, j: (i, j, 0)),
                  _layer_spec((FOURIER_W, FOURIER_W), layer)],
        out_specs=pl.BlockSpec((bg, tq, FOURIER_W), lambda i, j: (i, j, 0)),
        out_shape=jax.ShapeDtypeStruct((b, t, FOURIER_W), BF16),
        scratch_shapes=[pltpu.VMEM((2 * t, bg * FOURIER_W), BF16)],
        compiler_params=_params(("parallel", "arbitrary")),
        name="fourier",
    )(ct, xcs, fz, w_four)


def _log_sigmoid(x):
    return jnp.minimum(x, 0.0) - jnp.log(1.0 + jnp.exp(-jnp.abs(x)))


def _head_blocks(a, width):
    lane = lax.broadcasted_iota(jnp.int32, a.shape, 1)
    zero = jnp.zeros_like(a)
    return jnp.concatenate(
        [jnp.where((lane >= h * width) & (lane < (h + 1) * width), a, zero) for h in range(RET_HEADS)],
        axis=0)


def _per_head(dec_ref, layer, direction, head_of):
    out = jnp.full(head_of.shape, dec_ref[layer, direction, RET_HEADS - 1], F32)
    for h in range(RET_HEADS - 2, -1, -1):
        out = jnp.where(head_of == h, dec_ref[layer, direction, h], out)
    return out


def _ret_kernel(*refs, nb, t, layer, has_s0, n_carry, emit_state):
    refs = list(refs)
    dec_ref = refs.pop(0)
    ur_ref = refs.pop(0)
    s0_ref = refs.pop(0) if has_s0 else None
    gn_ref = refs.pop(0)
    del refs[:n_carry]
    yb_ref = refs.pop(0)
    sfin_ref = refs.pop(0) if emit_state else None
    (kr_ref, dsf_ref, dsb_ref, sfs_ref, sbs_ref, st_ref) = refs
    c = RET_CHUNK
    nc = t // c
    nch = nb * nc
    group = RET_UNROLL

    lane_head = lax.broadcasted_iota(jnp.int32, (1, RET_W), 1) // RET_DK
    lgf = _log_sigmoid(_per_head(dec_ref, layer, 0, lane_head))
    lgb = _log_sigmoid(_per_head(dec_ref, layer, 1, lane_head))
    ri = lax.broadcasted_iota(jnp.int32, (c, RET_W), 0).astype(F32)
    read_f = jnp.exp((ri + 1.0) * lgf)
    read_b = jnp.exp((c - ri) * lgb)
    write_f = jnp.exp((c - 1.0 - ri) * lgf)
    write_b = jnp.exp(ri * lgb)
    carry_f = jnp.exp(c * lgf)
    carry_b = jnp.exp(c * lgb)
    ii = lax.broadcasted_iota(jnp.int32, (RET_HEADS * c, c), 0)
    jj = lax.broadcasted_iota(jnp.int32, (RET_HEADS * c, c), 1)
    row_head = ii // c
    diff = ((ii & (c - 1)) - jj).astype(F32)
    decay = (jnp.where(diff >= 0, jnp.exp(jnp.maximum(diff, 0.0)
                                          * _log_sigmoid(_per_head(dec_ref, layer, 0, row_head))), 0.0)
             + jnp.where(diff <= 0, jnp.exp(jnp.maximum(-diff, 0.0)
                                            * _log_sigmoid(_per_head(dec_ref, layer, 1, row_head))), 0.0))
    r2 = lax.broadcasted_iota(jnp.int32, (RET_W, RET_W), 0)
    c2 = lax.broadcasted_iota(jnp.int32, (RET_W, RET_W), 1)
    same_head = (r2 // RET_DK) == (c2 // RET_DK)
    group_mean = jnp.where(same_head, 1.0 / RET_DK, 0.0).astype(BF16)
    out_head = lax.broadcasted_iota(jnp.int32, (c, RET_W), 1) // RET_DK

    def chunk_pos(ci):
        return ci // nc, pl.multiple_of((ci % nc) * c, c)

    def load_qk(col, b, r0):
        return ur_ref[b, pl.ds(r0, c), col:col + RET_W].astype(F32)

    def increments(ci, carry):
        b, r0 = chunk_pos(ci)
        k = load_qk(RET_W, b, r0) * (RET_DK ** -0.5)
        v = ur_ref[b, pl.ds(r0, c), 2 * RET_W:3 * RET_W]
        kr_ref[pl.ds(pl.multiple_of(ci * c, c), c), :] = k.astype(BF16)
        kw = jnp.concatenate([(k * write_f).astype(BF16), (k * write_b).astype(BF16)], axis=1)
        d = lax.dot_general(kw, v, (((0,), (0,)), ((), ())), preferred_element_type=F32)
        dsf_ref[ci] = jnp.where(same_head, d[:RET_W], 0.0)
        dsb_ref[ci] = jnp.where(same_head, d[RET_W:], 0.0)
        return carry

    lax.fori_loop(0, nch, increments, 0, unroll=group)

    def scan(b, direction, ds_ref, out_ref, carry_decay):
        st_ref[...] = jnp.zeros((RET_W, RET_W), F32)
        if has_s0:
            for h in range(RET_HEADS):
                sl = slice(h * RET_DK, (h + 1) * RET_DK)
                st_ref[sl, sl] = s0_ref[b, direction, h]

        def step(n, carry):
            ci = b * nc + (n if direction == 0 else nc - 1 - n)
            out_ref[ci] = st_ref[...].astype(BF16)
            st_ref[...] = carry_decay * st_ref[...] + ds_ref[ci]
            return carry

        lax.fori_loop(0, nc, step, 0, unroll=min(nc, 4))
        if emit_state:
            for h in range(RET_HEADS):
                sl = slice(h * RET_DK, (h + 1) * RET_DK)
                _store_slab(sfin_ref, b, layer, n_carry == 0, st_ref[sl, sl], (direction, h))

    for b in range(nb):
        scan(b, 0, dsf_ref, sfs_ref, carry_f)
        scan(b, 1, dsb_ref, sbs_ref, carry_b)

    def chunk_output(ci, b, r0):
        q = load_qk(0, b, r0)
        kb = kr_ref[pl.ds(pl.multiple_of(ci * c, c), c), :]
        v = ur_ref[b, pl.ds(r0, c), 2 * RET_W:3 * RET_W]
        att = lax.dot_general(_head_blocks(q.astype(BF16), RET_DK), kb,
                              (((1,), (1,)), ((), ())), preferred_element_type=F32)
        o4 = jnp.dot((att * decay).astype(BF16), v, preferred_element_type=F32)
        o = o4[(RET_HEADS - 1) * c:]
        for h in range(RET_HEADS - 2, -1, -1):
            o = jnp.where(out_head == h, o4[h * c:(h + 1) * c], o)
        o = o + jnp.dot((q * read_f).astype(BF16), sfs_ref[ci], preferred_element_type=F32)
        return o + jnp.dot((q * read_b).astype(BF16), sbs_ref[ci], preferred_element_type=F32)

    def outputs(it, carry):
        pos = [chunk_pos(it * group + u) for u in range(group)]
        o = jnp.concatenate([chunk_output(it * group + u, *pos[u]) for u in range(group)], axis=0)
        ms = jnp.dot((o * o).astype(BF16), group_mean, preferred_element_type=F32)
        y = o * lax.rsqrt(ms + EPS) * gn_ref[layer:layer + 1, :]
        for u, (b, r0) in enumerate(pos):
            z = ur_ref[b, pl.ds(r0, c), 3 * RET_W:4 * RET_W].astype(F32)
            yb_ref[b, pl.ds(r0, c), :] = (y[u * c:(u + 1) * c] * _silu(z)).astype(BF16)
        return carry

    lax.fori_loop(0, nch // group, outputs, 0)


def _retention(ur, layer, dec, gn, *, nb, emit_state, state=None, state_carry=()):
    b, t, _ = ur.shape
    nch = nb * (t // RET_CHUNK)
    assert nch % RET_UNROLL == 0
    has_s0 = state is not None
    in_specs = [pl.BlockSpec(memory_space=pltpu.SMEM),
                pl.BlockSpec((nb, t, 4 * RET_W), lambda i: (i, 0, 0))]
    args = [dec, ur]
    if has_s0:
        in_specs.append(pl.BlockSpec((nb, None, 2, RET_HEADS, RET_DK, RET_DK),
                                     lambda i: (i, layer, 0, 0, 0, 0)))
        args.append(state)
    in_specs.append(_const_spec((DEPTH, RET_W)))
    args.append(gn)
    carry_specs, carry, aliases = _carry_args(state_carry, len(args), 1)
    state_scratch = lambda dt: pltpu.VMEM((nch, RET_W, RET_W), dt)
    out_specs = [pl.BlockSpec((nb, t, RET_W), lambda i: (i, 0, 0))]
    out_shape = [jax.ShapeDtypeStruct((b, t, RET_W), BF16)]
    if emit_state:
        out_specs.append(_slab_spec(nb, (2, RET_HEADS, RET_DK, RET_DK), layer, not state_carry))
        out_shape.append(jax.ShapeDtypeStruct((b, DEPTH, 2, RET_HEADS, RET_DK, RET_DK), F32))
    return pl.pallas_call(
        functools.partial(_ret_kernel, nb=nb, t=t, layer=layer, has_s0=has_s0,
                          n_carry=len(carry), emit_state=emit_state),
        grid=(b // nb,),
        in_specs=in_specs + carry_specs,
        out_specs=out_specs,
        out_shape=out_shape,
        input_output_aliases=aliases,
        scratch_shapes=[pltpu.VMEM((nb * t, RET_W), BF16),
                        state_scratch(F32), state_scratch(F32), state_scratch(BF16), state_scratch(BF16),
                        pltpu.VMEM((RET_W, RET_W), F32)],
        compiler_params=_params(("parallel",)),
        name="retention",
    )(*args, *carry)


def _kv_variants(a, ones_block):
    lane_half = lax.broadcasted_iota(jnp.int32, a.shape, 1) // HEAD_DIM
    swapped = pltpu.roll(a, HEAD_DIM, axis=1)
    out = []
    for g in range(ATT_KV_HEADS):
        row = []
        for half in range(2):
            var = jnp.where(lane_half == half, a if half == g else swapped, 0.0).astype(BF16)
            if ones_block:
                ones = jnp.where(lane_half == half, 1.0, 0.0).astype(BF16)
                var = jnp.concatenate([var, ones], axis=1)
            row.append(var)
        out.append(row)
    return out


def _attend_group(q2, keys, values, masks, sink_ref, layer, g, z2):
    rows = q2.shape[0]
    upper = lax.broadcasted_iota(jnp.int32, (rows, 1), 0) < rows // 2
    acc = None
    sink_terms = []
    for half in range(2):
        sink = jnp.where(upper, sink_ref[layer, g, half], sink_ref[layer, g, 2 + half]) * LOG2E
        logits = []
        for kpart, mask in zip(keys[half], masks):
            l = lax.dot_general(q2, kpart, (((1,), (1,)), ((), ())), preferred_element_type=F32)
            logits.append(l if mask is None else jnp.where(mask, l, NEG))
        m = sink
        for l in logits:
            m = jnp.maximum(m, jnp.max(l, axis=-1, keepdims=True))
        for l, vpart in zip(logits, values[half]):
            pv = jnp.dot(jnp.exp2(l - m).astype(BF16), vpart, preferred_element_type=F32)
            acc = pv if acc is None else acc + pv
        sink_terms.append(jnp.exp2(sink - m))
    lane = lax.broadcasted_iota(jnp.int32, (rows, LANES), 1)
    den = acc[:, LANES:] + jnp.where(lane < HEAD_DIM, sink_terms[0], sink_terms[1])
    return (acc[:, :LANES] * (1.0 / den) * _silu(z2)).astype(BF16)


def _pair_rows(ref_slice, g):
    return jnp.concatenate([ref_slice(2 * g), ref_slice(2 * g + 1)], axis=0)


def _ctx_attn_kernel(sink_ref, ua_ref, yc_ref, *, nb, t, layer):
    for b in range(nb):
        kv = ua_ref[b, :, 2 * ATT_W:].astype(F32)
        kvar = _kv_variants(kv[:, :LANES] * (HEAD_DIM ** -0.5 * LOG2E), False)
        vvar = _kv_variants(kv[:, LANES:], True)
        for g in range(ATT_KV_HEADS):
            q2 = _pair_rows(lambda p: ua_ref[b, :, p * LANES:(p + 1) * LANES], g)
            z2 = _pair_rows(lambda p: ua_ref[b, :, ATT_W + p * LANES:ATT_W + (p + 1) * LANES], g)
            o = _attend_group(q2, [[kvar[g][0]], [kvar[g][1]]], [[vvar[g][0]], [vvar[g][1]]], [None],
                              sink_ref, layer, g, z2.astype(F32))
            yc_ref[b, :, 2 * g * LANES:(2 * g + 1) * LANES] = o[:t]
            yc_ref[b, :, (2 * g + 1) * LANES:(2 * g + 2) * LANES] = o[t:]


def _ctx_attention(ua, layer, sink):
    b, t, w = ua.shape
    nb = NB_CTX_ATTN
    return pl.pallas_call(
        functools.partial(_ctx_attn_kernel, nb=nb, t=t, layer=layer),
        grid=(b // nb,),
        in_specs=[pl.BlockSpec(memory_space=pltpu.SMEM),
                  pl.BlockSpec((nb, t, w), lambda i: (i, 0, 0))],
        out_specs=pl.BlockSpec((nb, t, ATT_W), lambda i: (i, 0, 0)),
        out_shape=jax.ShapeDtypeStruct((b, t, ATT_W), BF16),
        compiler_params=_params(("parallel",)),
        name="ctx_attention",
    )(sink, ua)


def _lat_attn_kernel(sink_ref, qz_ref, kv_ref, ck_ref, cv_ref, yc_ref,
                     kl_ref, vl_ref, kc_ref, vc_ref, *, t, tq, tiles, layer):
    j = pl.program_id(1)
    scale = HEAD_DIM ** -0.5 * LOG2E

    @pl.when(j == 0)
    def _prepare():
        kv = kv_ref[...].astype(F32)
        kvar = _kv_variants(kv[:, :LANES] * scale, False)
        vvar = _kv_variants(kv[:, LANES:], True)
        cvar = _kv_variants(ck_ref[...] * scale, False)
        dvar = _kv_variants(cv_ref[...], True)
        for g in range(ATT_KV_HEADS):
            for half in range(2):
                i = 2 * g + half
                for ref, var in ((kl_ref, kvar), (vl_ref, vvar)):
                    pad = jnp.zeros((WINDOW, ref.shape[-1]), BF16)
                    ref[i, 0:WINDOW, :] = pad
                    ref[i, WINDOW:WINDOW + t, :] = var[g][half]
                    ref[i, WINDOW + t:, :] = pad
                kc_ref[i] = cvar[g][half]
                vc_ref[i] = dvar[g][half]

    nloc = tq + 2 * WINDOW
    rr = lax.broadcasted_iota(jnp.int32, (2 * tq, nloc), 0) & (tq - 1)
    ss = lax.broadcasted_iota(jnp.int32, (2 * tq, nloc), 1)
    for u in range(tiles):
        rows = slice(u * tq, (u + 1) * tq)
        first = (j * tiles + u) * tq
        r0 = pl.multiple_of(first, tq)
        band = ((ss - rr >= 0) & (ss - rr <= 2 * WINDOW)
                & (ss >= WINDOW - first) & (ss < t + WINDOW - first))
        for g in range(ATT_KV_HEADS):
            q2 = _pair_rows(lambda p: qz_ref[rows, p * LANES:(p + 1) * LANES], g)
            z2 = _pair_rows(lambda p: qz_ref[rows, ATT_W + p * LANES:ATT_W + (p + 1) * LANES], g)
            keys = [[kl_ref[2 * g + half, pl.ds(r0, nloc), :], kc_ref[2 * g + half]] for half in range(2)]
            vals = [[vl_ref[2 * g + half, pl.ds(r0, nloc), :], vc_ref[2 * g + half]] for half in range(2)]
            o = _attend_group(q2, keys, vals, [band, None], sink_ref, layer, g, z2.astype(F32))
            yc_ref[rows, 2 * g * LANES:(2 * g + 1) * LANES] = o[:tq]
            yc_ref[rows, (2 * g + 1) * LANES:(2 * g + 2) * LANES] = o[tq:]


def _lat_attention(ua, cache_k, cache_v, layer, sink):
    b, t, _ = ua.shape
    past = cache_k.shape[2]
    tq = TQ_LATENT
    tiles = LAT_TILES_PER_STEP
    nvar = 2 * ATT_KV_HEADS
    cache_spec = pl.BlockSpec((None, None, past, ATT_KV_W), lambda i, j: (i, layer, 0, 0))
    return pl.pallas_call(
        functools.partial(_lat_attn_kernel, t=t, tq=tq, tiles=tiles, layer=layer),
        grid=(b, t // (tiles * tq)),
        in_specs=[pl.BlockSpec(memory_space=pltpu.SMEM),
                  pl.BlockSpec((None, tiles * tq, 2 * ATT_W), lambda i, j: (i, j, 0)),
                  pl.BlockSpec((None, t, 2 * ATT_KV_W), lambda i, j: (i, 0, 2 * ATT_W // (2 * ATT_KV_W))),
                  cache_spec, cache_spec],
        out_specs=pl.BlockSpec((None, tiles * tq, ATT_W), lambda i, j: (i, j, 0)),
        out_shape=jax.ShapeDtypeStruct((b, t, ATT_W), BF16),
        scratch_shapes=[pltpu.VMEM((nvar, t + 2 * WINDOW, LANES), BF16),
                        pltpu.VMEM((nvar, t + 2 * WINDOW, 2 * LANES), BF16),
                        pltpu.VMEM((nvar, past, LANES), BF16),
                        pltpu.VMEM((nvar, past, 2 * LANES), BF16)],
        compiler_params=_params(("parallel", "arbitrary")),
        name="lat_attention",
    )(sink, ua, ua, cache_k, cache_v)


def _outproj_kernel(x_ref, ya_ref, yb_ref, yc_ref, mod_ref, gpre_ref, gpost_ref,
                    wg_ref, wgt_ref, wa_ref, wb_ref, wc_ref, wo_ref, o_ref, *, layer, tm, row0, rows_per_mod):
    for r in range(tm // SUB_ROWS):
        rows = slice(r * SUB_ROWS, (r + 1) * SUB_ROWS)
        x = x_ref[rows, :]
        h, row = _modulated_norm(x, gpre_ref[layer:layer + 1, :], mod_ref, tm, row0, rows_per_mod)

        def gate_logits(c):
            lo, hi = c * D_MODEL, min((c + 1) * D_MODEL, C_MG)
            mg = jnp.dot(h, wg_ref[:, lo:hi].astype(BF16), preferred_element_type=F32)
            if hi - lo < D_MODEL:
                tail = jnp.dot(h, wgt_ref[...].astype(BF16), preferred_element_type=F32)
                mg = jnp.concatenate([mg, tail], axis=1)
            return mg

        merged = None
        for c, (y_ref, w_ref) in enumerate(((ya_ref, wa_ref), (yb_ref, wb_ref), (yc_ref, wc_ref))):
            term = _sigmoid(gate_logits(c)) * jnp.dot(y_ref[rows, :], w_ref[...].astype(BF16),
                                                      preferred_element_type=F32)
            merged = term if merged is None else merged + term
        out = jnp.dot(merged.astype(BF16), wo_ref[...].astype(BF16), preferred_element_type=F32)
        ms = jnp.mean(out * out, axis=-1, keepdims=True)
        normed = out * lax.rsqrt(ms + EPS) * gpost_ref[layer:layer + 1, :]
        gate = mod_ref[pl.ds(row, 1), 2 * D_MODEL:3 * D_MODEL]
        o_ref[rows, :] = x + gate * normed


def _outproj(x2, ya, yb, yc, layer, mod, g_pre, g_post, w_in, wa, wb, wc, wo, *, row0, rows_per_mod):
    n = x2.shape[0]
    tm = TM_OUTPROJ
    row_spec = lambda w: pl.BlockSpec((tm, w), lambda i: (i, 0))
    return pl.pallas_call(
        functools.partial(_outproj_kernel, layer=layer, tm=tm, row0=row0, rows_per_mod=rows_per_mod),
        grid=(n // tm,),
        in_specs=[row_spec(D_MODEL), row_spec(FOURIER_W), row_spec(RET_W), row_spec(ATT_W),
                  _layer_spec((MOD_ROWS, 3 * D_MODEL), layer),
                  _const_spec((DEPTH, D_MODEL)), _const_spec((DEPTH, D_MODEL)),
                  _layer_spec((D_MODEL, C_MG), layer, col_block=1),
                  _layer_spec((D_MODEL, C_END - 2 * C_MG), layer, col_block=2 * C_MG // (C_END - 2 * C_MG)),
                  _layer_spec((FOURIER_W, D_MODEL), layer), _layer_spec((RET_W, D_MODEL), layer),
                  _layer_spec((ATT_W, D_MODEL), layer), _layer_spec((D_MODEL, D_MODEL), layer)],
        out_specs=row_spec(D_MODEL),
        out_shape=jax.ShapeDtypeStruct((n, D_MODEL), F32),
        compiler_params=_params(("parallel",)),
        name="outproj",
    )(x2, ya, yb, yc, mod, g_pre, g_post, w_in, w_in, wa, wb, wc, wo)


def _layer(x, layer, mod, p, *, latent, dft_chan, dft_pos, rope_tabs=None, cache_k=None, cache_v=None,
           state=None, kv_carry=(), state_carry=()):
    b, t, _ = x.shape
    n = b * t
    row0, rows_per_mod = (1, t) if latent else (0, n)
    x2 = x.reshape(n, D_MODEL)
    xcs, fz, ur, ua, *kv32 = _inproj(x2, layer, mod, p["g_pre"], p["w_in"], dft_chan,
                                     seq=t, row0=row0, rows_per_mod=rows_per_mod, emit_kv=not latent,
                                     kv_carry=kv_carry, rope_tabs=rope_tabs if latent else ())
    ya = _fourier(xcs.reshape(b, t, -1), fz.reshape(b, t, -1), dft_pos, p["w_four"], layer,
                  bg=b if latent else BG_CTX_FOURIER, tq=min(t, TQ_FOURIER))
    ur3 = ur.reshape(b, t, -1)
    ua3 = ua.reshape(b, t, -1)
    if latent:
        (yb,), s_fin = _retention(ur3, layer, p["dec"], p["gn"], nb=1, emit_state=False, state=state), None
        yc = _lat_attention(ua3, cache_k, cache_v, layer, p["sink"])
    else:
        yb, s_fin = _retention(ur3, layer, p["dec"], p["gn"], nb=NB_CTX_RET, emit_state=True,
                               state_carry=state_carry)
        yc = _ctx_attention(ua3, layer, p["sink"])
    out = _outproj(x2, ya.reshape(n, -1), yb.reshape(n, -1), yc.reshape(n, -1), layer, mod,
                   p["g_pre"], p["g_post"], p["w_in"], p["w_pa"], p["w_pb"], p["w_pc"], p["w_out"],
                   row0=row0, rows_per_mod=rows_per_mod)
    return out.reshape(b, t, D_MODEL), kv32, s_fin


def kernel(x_prompt, x_sample, cache_k, cache_v, state_ret, c, c_ctx, w_mod, b_mod, g_pre, g_post, w_in,
           w_four, ret_decay, ret_gn, attn_sink, w_branch_a, w_branch_b, w_branch_c, w_out):
    batch, seq, _ = x_prompt.shape
    dec_batch, dec_seq, _ = x_sample.shape
    past = cache_k.shape[2]
    assert 1 + dec_batch <= MOD_ROWS

    cv = jnp.zeros((MOD_ROWS, D_MODEL), F32).at[0].set(c_ctx).at[1:1 + dec_batch].set(c)
    mod = _modulation(cv, w_mod, b_mod)

    dft_chan, dft_ctx = _dft_tables(seq)
    _, dft_lat = _dft_tables(dec_seq)
    rope_tabs = _rope_tables(dec_seq)
    ck = cache_k.reshape(dec_batch, DEPTH, past, ATT_KV_W)
    cvv = cache_v.reshape(dec_batch, DEPTH, past, ATT_KV_W)

    p = dict(
        g_pre=g_pre, g_post=g_post,
        w_in=w_in, w_four=w_four, w_pa=w_branch_a, w_pb=w_branch_b, w_pc=w_branch_c, w_out=w_out,
        dec=ret_decay, gn=ret_gn, sink=attn_sink)

    xp = x_prompt
    kv_all, s_all = (), ()
    for l in range(DEPTH):
        xp, kv_all, s_all = _layer(xp, l, mod, p, latent=False, dft_chan=dft_chan, dft_pos=dft_ctx,
                                   kv_carry=tuple(kv_all), state_carry=s_all)
        s_all = (s_all,)

    xs = x_sample
    for l in range(DEPTH):
        xs, _, _ = _layer(xs, l, mod, p, latent=True, dft_chan=dft_chan, dft_pos=dft_lat,
                             rope_tabs=rope_tabs, cache_k=ck, cache_v=cvv, state=state_ret)

    new_k, new_v = (jnp.transpose(a, (0, 1, 4, 2, 3)) for a in kv_all)
    return (xp, xs, new_k, new_v, s_all[0])
```

```python
import functools
import math

import numpy as np
import jax
import jax.numpy as jnp
from jax import lax
from jax.experimental import pallas as pl
from jax.experimental.pallas import tpu as pltpu

F32 = jnp.float32
BF16 = jnp.bfloat16

D_MODEL = 1024
DEPTH = 2
GRID_W = 64
HEAD_DIM = 64
FOURIER_GROUPS = 4
FOURIER_GROUP_W = 64
FOURIER_W = FOURIER_GROUPS * FOURIER_GROUP_W
RET_HEADS = 4
RET_DK = 64
RET_W = RET_HEADS * RET_DK
RET_CHUNK = 128
ATT_Q_HEADS = 8
ATT_KV_HEADS = 2
ATT_W = ATT_Q_HEADS * HEAD_DIM
ATT_KV_W = ATT_KV_HEADS * HEAD_DIM
WINDOW = 128
ROPE_BASE = 10000.0
EPS = 1e-6
MOD_ROWS = 8
LANES = 128
NEG = -1e30
LOG2E = math.log2(math.e)
VMEM_LIMIT = 56 * 1024 * 1024

C_FX, C_RQ, C_RZ_END = 0, 512, 1536
C_AQ, C_AK, C_AV, C_AZ, C_MG, C_END = 1536, 2048, 2176, 2304, 2816, 5888

TM_INPROJ = 1024
TM_OUTPROJ = 1024
SUB_ROWS = 512
TQ_LATENT = 256
LAT_TILES_PER_STEP = 4
TQ_FOURIER = 512
NB_CTX_RET = 8
NB_CTX_ATTN = 4
BG_CTX_FOURIER = 8
RET_UNROLL = 16


def _sigmoid(x):
    return 0.5 * jnp.tanh(0.5 * x) + 0.5


def _silu(x):
    return x * _sigmoid(x)


def _params(sem):
    return pltpu.CompilerParams(dimension_semantics=sem, vmem_limit_bytes=VMEM_LIMIT)


def _const_spec(shape):
    nd = len(shape)
    return pl.BlockSpec(shape, lambda *_: (0,) * nd, pipeline_mode=pl.Buffered(1))


def _layer_spec(shape, layer, col_block=0):
    idx = (layer,) + (0,) * (len(shape) - 1) + (col_block,)
    return pl.BlockSpec((None,) + tuple(shape), lambda *_: idx, pipeline_mode=pl.Buffered(1))


def _carry_args(carry, first_input, first_output):
    specs = [pl.BlockSpec(memory_space=pl.ANY)] * len(carry)
    aliases = {first_input + k: first_output + k for k in range(len(carry))}
    return specs, list(carry), aliases


def _slab_spec(lead, rest, layer, whole):
    zeros = (0,) * len(rest)
    if whole:
        return pl.BlockSpec((lead, DEPTH) + tuple(rest), lambda i: (i, 0) + zeros)
    return pl.BlockSpec((lead, None) + tuple(rest), lambda i: (i, layer) + zeros)


def _store_slab(ref, lead_idx, layer, whole, value, rest_idx=()):
    if not whole:
        ref[(lead_idx,) + tuple(rest_idx)] = value
        return
    for l in range(DEPTH):
        ref[(lead_idx, l) + tuple(rest_idx)] = value if l == layer else jnp.zeros_like(value)


def _dft_tables(t):
    c = np.arange(FOURIER_GROUP_W)
    ang = 2.0 * np.pi * ((c[:, None] * c[None, :]) % FOURIER_GROUP_W) / FOURIER_GROUP_W
    eye = np.eye(FOURIER_GROUPS)
    s64 = FOURIER_GROUP_W ** -0.5
    chan = np.concatenate([np.kron(eye, np.cos(ang) * s64), np.kron(eye, np.sin(ang) * s64)], axis=1)
    p = np.arange(t)
    angt = 2.0 * np.pi * ((p[:, None] * p[None, :]) % t) / t
    pos = np.concatenate([np.cos(angt), -np.sin(angt)], axis=1) * (t ** -0.5)
    return jnp.asarray(chan, F32).astype(BF16), jnp.asarray(pos, F32).astype(BF16)


def _rope_tables(t):
    quarter = HEAD_DIM // 4
    lane = np.arange(LANES) % HEAD_DIM
    inv = ROPE_BASE ** (-(lane % quarter).astype(np.float64) / quarter)
    n = np.arange(t)
    pos = np.where(lane[None, :] < HEAD_DIM // 2, (n // GRID_W)[:, None], (n % GRID_W)[:, None])
    ang = pos.astype(np.float64) * inv[None, :]
    sign = np.where((lane % (2 * quarter)) < quarter, -1.0, 1.0)
    return jnp.asarray(np.cos(ang), F32), jnp.asarray(np.sin(ang) * sign[None, :], F32)


def _rope(x, cos, sin):
    lane = lax.broadcasted_iota(jnp.int32, x.shape, 1)
    first = (lane & 31) < 16
    partner = jnp.where(first, pltpu.roll(x, LANES - 16, axis=1), pltpu.roll(x, 16, axis=1))
    return x * cos + partner * sin


def _split_bf16(x):
    hi = x.astype(BF16)
    return hi, (x - hi.astype(F32)).astype(BF16)


def _mod_kernel(cv_ref, w_ref, b_ref, o_ref):
    a_hi, a_lo = _split_bf16(_silu(cv_ref[...]))
    w_hi, w_lo = _split_bf16(w_ref[...])
    dot = functools.partial(jnp.dot, preferred_element_type=F32)
    bias = b_ref[pl.ds(pl.program_id(0), 1), :]
    o_ref[...] = dot(a_hi, w_hi) + (dot(a_lo, w_hi) + dot(a_hi, w_lo)) + bias


def _modulation(cv, w_mod, b_mod):
    tn = 1024
    return pl.pallas_call(
        _mod_kernel,
        grid=(DEPTH, 3 * D_MODEL // tn),
        in_specs=[pl.BlockSpec((MOD_ROWS, D_MODEL), lambda l, j: (0, 0)),
                  pl.BlockSpec((None, D_MODEL, tn), lambda l, j: (l, 0, j)),
                  pl.BlockSpec((DEPTH, tn), lambda l, j: (0, j))],
        out_specs=pl.BlockSpec((None, MOD_ROWS, tn), lambda l, j: (l, 0, j)),
        out_shape=jax.ShapeDtypeStruct((DEPTH, MOD_ROWS, 3 * D_MODEL), F32),
        compiler_params=_params(("parallel", "parallel")),
        name="modulation",
    )(cv, w_mod, b_mod)


def _modulated_norm(x, g, mod_ref, tm, row0, rows_per_mod):
    row = row0 + (pl.program_id(0) * tm) // rows_per_mod
    ms = jnp.mean(x * x, axis=-1, keepdims=True)
    y = x * lax.rsqrt(ms + EPS) * g
    shift = mod_ref[pl.ds(row, 1), 0:D_MODEL]
    scale = mod_ref[pl.ds(row, 1), D_MODEL:2 * D_MODEL]
    return (y * (1.0 + scale) + shift).astype(BF16), row


def _inproj_kernel(x_ref, mod_ref, g_ref, w_ref, dft_ref, *refs, layer, tm, seq, row0, rows_per_mod,
                   n_carry, rope):
    cos_ref, sin_ref = refs[:2] if rope else (None, None)
    refs = refs[2 if rope else 0:]
    xcs_ref, fz_ref, ur_ref, ua_ref, *kv_refs = refs[n_carry:]
    for r in range(tm // SUB_ROWS):
        rows = slice(r * SUB_ROWS, (r + 1) * SUB_ROWS)
        h, _ = _modulated_norm(x_ref[rows, :], g_ref[layer:layer + 1, :], mod_ref, tm, row0, rows_per_mod)

        def mm(c0, c1):
            return jnp.dot(h, w_ref[:, c0:c1].astype(BF16), preferred_element_type=F32)

        def roped(a):
            if not rope:
                return a.astype(BF16)
            cs, sn = cos_ref[rows, :], sin_ref[rows, :]
            return jnp.concatenate([_rope(a[:, c:c + LANES], cs, sn) for c in range(0, a.shape[1], LANES)],
                                   axis=1).astype(BF16)

        f = mm(C_FX, C_RQ)
        xcs_ref[rows, :] = jnp.dot(f[:, :FOURIER_W].astype(BF16), dft_ref[...],
                                   preferred_element_type=F32).astype(BF16)
        fz_ref[rows, :] = _silu(f[:, FOURIER_W:]).astype(BF16)
        ur = mm(C_RQ, C_RZ_END)
        ur_ref[rows, 0:2 * RET_W] = roped(ur[:, :2 * RET_W])
        ur_ref[rows, 2 * RET_W:3 * RET_W] = ur[:, 2 * RET_W:3 * RET_W].astype(BF16)
        ur_ref[rows, 3 * RET_W:] = _silu(ur[:, 3 * RET_W:]).astype(BF16)
        ua_ref[rows, 0:ATT_W] = roped(mm(C_AQ, C_AK))
        ua_ref[rows, ATT_W:2 * ATT_W] = _silu(mm(C_AZ, C_MG)).astype(BF16)
        kv = mm(C_AK, C_AZ)
        if kv_refs:
            for s in range(SUB_ROWS // seq):
                blk = kv[s * seq:(s + 1) * seq]
                bi = (r * SUB_ROWS) // seq + s
                for ref, cols in zip(kv_refs, (blk[:, :ATT_KV_W], blk[:, ATT_KV_W:])):
                    _store_slab(ref, bi, layer, n_carry == 0,
                                cols.T.reshape(ATT_KV_HEADS, HEAD_DIM, seq))
        ua_ref[rows, 2 * ATT_W:2 * ATT_W + ATT_KV_W] = roped(kv[:, :ATT_KV_W])
        ua_ref[rows, 2 * ATT_W + ATT_KV_W:] = kv[:, ATT_KV_W:].astype(BF16)


def _inproj(x2, layer, mod, g_pre, w_in, dft, *, seq, row0, rows_per_mod, emit_kv, kv_carry=(),
            rope_tabs=()):
    n = x2.shape[0]
    tm = TM_INPROJ
    row_spec = lambda w: pl.BlockSpec((tm, w), lambda i: (i, 0))
    widths = (2 * FOURIER_W, FOURIER_W, 4 * RET_W, 2 * ATT_W + 2 * ATT_KV_W)
    out_shape = [jax.ShapeDtypeStruct((n, w), BF16) for w in widths]
    out_specs = [row_spec(w) for w in widths]
    if emit_kv:
        assert SUB_ROWS % seq == 0
        out_shape += [jax.ShapeDtypeStruct((n // seq, DEPTH, ATT_KV_HEADS, HEAD_DIM, seq), F32)] * 2
        out_specs += [_slab_spec(tm // seq, (ATT_KV_HEADS, HEAD_DIM, seq), layer, not kv_carry)] * 2
    args = [x2, mod, g_pre, w_in, dft, *rope_tabs]
    rope_specs = [pl.BlockSpec((tm, LANES), lambda i: (i % (seq // tm), 0))] * len(rope_tabs)
    carry_specs, carry, aliases = _carry_args(kv_carry, len(args), len(widths))
    return pl.pallas_call(
        functools.partial(_inproj_kernel, layer=layer, tm=tm, seq=seq, row0=row0, rows_per_mod=rows_per_mod,
                          n_carry=len(carry), rope=bool(rope_tabs)),
        grid=(n // tm,),
        in_specs=[row_spec(D_MODEL),
                  _layer_spec((MOD_ROWS, 3 * D_MODEL), layer),
                  _const_spec((DEPTH, D_MODEL)),
                  _layer_spec((D_MODEL, C_MG), layer),
                  _const_spec((FOURIER_W, 2 * FOURIER_W))] + rope_specs + carry_specs,
        out_specs=out_specs,
        out_shape=out_shape,
        input_output_aliases=aliases,
        compiler_params=_params(("parallel",)),
        name="inproj",
    )(*args, *carry)


def _fourier_kernel(ct_ref, xcs_ref, fz_ref, w_ref, ya_ref, xcat_ref, *, bg, t):
    @pl.when(pl.program_id(1) == 0)
    def _gather():
        for b in range(bg):
            cols = slice(b * FOURIER_W, (b + 1) * FOURIER_W)
            xcat_ref[0:t, cols] = xcs_ref[b, :, 0:FOURIER_W]
            xcat_ref[t:2 * t, cols] = xcs_ref[b, :, FOURIER_W:]

    yr = jnp.dot(ct_ref[...], xcat_ref[...], preferred_element_type=F32).astype(BF16)
    w = w_ref[...].astype(BF16)
    for b in range(bg):
        ya = jnp.dot(yr[:, b * FOURIER_W:(b + 1) * FOURIER_W], w, preferred_element_type=F32)
        ya_ref[b] = (ya * fz_ref[b].astype(F32)).astype(BF16)


def _fourier(xcs, fz, ct, w_four, layer, *, bg, tq):
    b, t, _ = xcs.shape
    return pl.pallas_call(
        functools.partial(_fourier_kernel, bg=bg, t=t),
        grid=(b // bg, t // tq),
        in_specs=[pl.BlockSpec((tq, 2 * t), lambda i, j: (j, 0)),
                  pl.BlockSpec((bg, t, 2 * FOURIER_W), lambda i, j: (i, 0, 0)),
                  pl.BlockSpec((bg, tq, FOURIER_W), lambda i, j: (i, j, 0)),
                  _layer_spec((FOURIER_W, FOURIER_W), layer)],
        out_specs=pl.BlockSpec((bg, tq, FOURIER_W), lambda i, j: (i, j, 0)),
        out_shape=jax.ShapeDtypeStruct((b, t, FOURIER_W), BF16),
        scratch_shapes=[pltpu.VMEM((2 * t, bg * FOURIER_W), BF16)],
        compiler_params=_params(("parallel", "arbitrary")),
        name="fourier",
    )(ct, xcs, fz, w_four)


def _log_sigmoid(x):
    return jnp.minimum(x, 0.0) - jnp.log(1.0 + jnp.exp(-jnp.abs(x)))


def _head_blocks(a, width):
    lane = lax.broadcasted_iota(jnp.int32, a.shape, 1)
    zero = jnp.zeros_like(a)
    return jnp.concatenate(
        [jnp.where((lane >= h * width) & (lane < (h + 1) * width), a, zero) for h in range(RET_HEADS)],
        axis=0)


def _per_head(dec_ref, layer, direction, head_of):
    out = jnp.full(head_of.shape, dec_ref[layer, direction, RET_HEADS - 1], F32)
    for h in range(RET_HEADS - 2, -1, -1):
        out = jnp.where(head_of == h, dec_ref[layer, direction, h], out)
    return out


def _ret_kernel(*refs, nb, t, layer, has_s0, n_carry, emit_state):
    refs = list(refs)
    dec_ref = refs.pop(0)
    ur_ref = refs.pop(0)
    s0_ref = refs.pop(0) if has_s0 else None
    gn_ref = refs.pop(0)
    del refs[:n_carry]
    yb_ref = refs.pop(0)
    sfin_ref = refs.pop(0) if emit_state else None
    (kr_ref, dsf_ref, dsb_ref, sfs_ref, sbs_ref, st_ref) = refs
    c = RET_CHUNK
    nc = t // c
    nch = nb * nc
    group = RET_UNROLL

    lane_head = lax.broadcasted_iota(jnp.int32, (1, RET_W), 1) // RET_DK
    lgf = _log_sigmoid(_per_head(dec_ref, layer, 0, lane_head))
    lgb = _log_sigmoid(_per_head(dec_ref, layer, 1, lane_head))
    ri = lax.broadcasted_iota(jnp.int32, (c, RET_W), 0).astype(F32)
    read_f = jnp.exp((ri + 1.0) * lgf)
    read_b = jnp.exp((c - ri) * lgb)
    write_f = jnp.exp((c - 1.0 - ri) * lgf)
    write_b = jnp.exp(ri * lgb)
    carry_f = jnp.exp(c * lgf)
    carry_b = jnp.exp(c * lgb)
    ii = lax.broadcasted_iota(jnp.int32, (RET_HEADS * c, c), 0)
    jj = lax.broadcasted_iota(jnp.int32, (RET_HEADS * c, c), 1)
    row_head = ii // c
    diff = ((ii & (c - 1)) - jj).astype(F32)
    decay = (jnp.where(diff >= 0, jnp.exp(jnp.maximum(diff, 0.0)
                                          * _log_sigmoid(_per_head(dec_ref, layer, 0, row_head))), 0.0)
             + jnp.where(diff <= 0, jnp.exp(jnp.maximum(-diff, 0.0)
                                            * _log_sigmoid(_per_head(dec_ref, layer, 1, row_head))), 0.0))
    r2 = lax.broadcasted_iota(jnp.int32, (RET_W, RET_W), 0)
    c2 = lax.broadcasted_iota(jnp.int32, (RET_W, RET_W), 1)
    same_head = (r2 // RET_DK) == (c2 // RET_DK)
    group_mean = jnp.where(same_head, 1.0 / RET_DK, 0.0).astype(BF16)
    out_head = lax.broadcasted_iota(jnp.int32, (c, RET_W), 1) // RET_DK

    def chunk_pos(ci):
        return ci // nc, pl.multiple_of((ci % nc) * c, c)

    def load_qk(col, b, r0):
        return ur_ref[b, pl.ds(r0, c), col:col + RET_W].astype(F32)

    def increments(ci, carry):
        b, r0 = chunk_pos(ci)
        k = load_qk(RET_W, b, r0) * (RET_DK ** -0.5)
        v = ur_ref[b, pl.ds(r0, c), 2 * RET_W:3 * RET_W]
        kr_ref[pl.ds(pl.multiple_of(ci * c, c), c), :] = k.astype(BF16)
        kw = jnp.concatenate([(k * write_f).astype(BF16), (k * write_b).astype(BF16)], axis=1)
        d = lax.dot_general(kw, v, (((0,), (0,)), ((), ())), preferred_element_type=F32)
        dsf_ref[ci] = jnp.where(same_head, d[:RET_W], 0.0)
        dsb_ref[ci] = jnp.where(same_head, d[RET_W:], 0.0)
        return carry

    lax.fori_loop(0, nch, increments, 0, unroll=group)

    def scan(b, direction, ds_ref, out_ref, carry_decay):
        st_ref[...] = jnp.zeros((RET_W, RET_W), F32)
        if has_s0:
            for h in range(RET_HEADS):
                sl = slice(h * RET_DK, (h + 1) * RET_DK)
                st_ref[sl, sl] = s0_ref[b, direction, h]

        def step(n, carry):
            ci = b * nc + (n if direction == 0 else nc - 1 - n)
            out_ref[ci] = st_ref[...].astype(BF16)
            st_ref[...] = carry_decay * st_ref[...] + ds_ref[ci]
            return carry

        lax.fori_loop(0, nc, step, 0, unroll=min(nc, 4))
        if emit_state:
            for h in range(RET_HEADS):
                sl = slice(h * RET_DK, (h + 1) * RET_DK)
                _store_slab(sfin_ref, b, layer, n_carry == 0, st_ref[sl, sl], (direction, h))

    for b in range(nb):
        scan(b, 0, dsf_ref, sfs_ref, carry_f)
        scan(b, 1, dsb_ref, sbs_ref, carry_b)

    def chunk_output(ci, b, r0):
        q = load_qk(0, b, r0)
        kb = kr_ref[pl.ds(pl.multiple_of(ci * c, c), c), :]
        v = ur_ref[b, pl.ds(r0, c), 2 * RET_W:3 * RET_W]
        att = lax.dot_general(_head_blocks(q.astype(BF16), RET_DK), kb,
                              (((1,), (1,)), ((), ())), preferred_element_type=F32)
        o4 = jnp.dot((att * decay).astype(BF16), v, preferred_element_type=F32)
        o = o4[(RET_HEADS - 1) * c:]
        for h in range(RET_HEADS - 2, -1, -1):
            o = jnp.where(out_head == h, o4[h * c:(h + 1) * c], o)
        o = o + jnp.dot((q * read_f).astype(BF16), sfs_ref[ci], preferred_element_type=F32)
        return o + jnp.dot((q * read_b).astype(BF16), sbs_ref[ci], preferred_element_type=F32)

    def outputs(it, carry):
        pos = [chunk_pos(it * group + u) for u in range(group)]
        o = jnp.concatenate([chunk_output(it * group + u, *pos[u]) for u in range(group)], axis=0)
        ms = jnp.dot((o * o).astype(BF16), group_mean, preferred_element_type=F32)
        y = o * lax.rsqrt(ms + EPS) * gn_ref[layer:layer + 1, :]
        for u, (b, r0) in enumerate(pos):
            z = ur_ref[b, pl.ds(r0, c), 3 * RET_W:4 * RET_W].astype(F32)
            yb_ref[b, pl.ds(r0, c), :] = (y[u * c:(u + 1) * c] * z).astype(BF16)
        return carry

    lax.fori_loop(0, nch // group, outputs, 0)


def _retention(ur, layer, dec, gn, *, nb, emit_state, state=None, state_carry=()):
    b, t, _ = ur.shape
    nch = nb * (t // RET_CHUNK)
    assert nch % RET_UNROLL == 0
    has_s0 = state is not None
    in_specs = [pl.BlockSpec(memory_space=pltpu.SMEM),
                pl.BlockSpec((nb, t, 4 * RET_W), lambda i: (i, 0, 0))]
    args = [dec, ur]
    if has_s0:
        in_specs.append(pl.BlockSpec((nb, None, 2, RET_HEADS, RET_DK, RET_DK),
                                     lambda i: (i, layer, 0, 0, 0, 0)))
        args.append(state)
    in_specs.append(_const_spec((DEPTH, RET_W)))
    args.append(gn)
    carry_specs, carry, aliases = _carry_args(state_carry, len(args), 1)
    state_scratch = lambda dt: pltpu.VMEM((nch, RET_W, RET_W), dt)
    out_specs = [pl.BlockSpec((nb, t, RET_W), lambda i: (i, 0, 0))]
    out_shape = [jax.ShapeDtypeStruct((b, t, RET_W), BF16)]
    if emit_state:
        out_specs.append(_slab_spec(nb, (2, RET_HEADS, RET_DK, RET_DK), layer, not state_carry))
        out_shape.append(jax.ShapeDtypeStruct((b, DEPTH, 2, RET_HEADS, RET_DK, RET_DK), F32))
    return pl.pallas_call(
        functools.partial(_ret_kernel, nb=nb, t=t, layer=layer, has_s0=has_s0,
                          n_carry=len(carry), emit_state=emit_state),
        grid=(b // nb,),
        in_specs=in_specs + carry_specs,
        out_specs=out_specs,
        out_shape=out_shape,
        input_output_aliases=aliases,
        scratch_shapes=[pltpu.VMEM((nb * t, RET_W), BF16),
                        state_scratch(F32), state_scratch(F32), state_scratch(BF16), state_scratch(BF16),
                        pltpu.VMEM((RET_W, RET_W), F32)],
        compiler_params=_params(("parallel",)),
        name="retention",
    )(*args, *carry)


def _kv_variants(a, ones_block):
    lane_half = lax.broadcasted_iota(jnp.int32, a.shape, 1) // HEAD_DIM
    swapped = pltpu.roll(a, HEAD_DIM, axis=1)
    out = []
    for g in range(ATT_KV_HEADS):
        row = []
        for half in range(2):
            var = jnp.where(lane_half == half, a if half == g else swapped, 0.0).astype(BF16)
            if ones_block:
                ones = jnp.where(lane_half == half, 1.0, 0.0).astype(BF16)
                var = jnp.concatenate([var, ones], axis=1)
            row.append(var)
        out.append(row)
    return out


def _attend_group(q2, keys, values, masks, sink_ref, layer, g, z2):
    rows = q2.shape[0]
    upper = lax.broadcasted_iota(jnp.int32, (rows, 1), 0) < rows // 2
    acc = None
    sink_terms = []
    for half in range(2):
        sink = jnp.where(upper, sink_ref[layer, g, half], sink_ref[layer, g, 2 + half]) * LOG2E
        logits = []
        for kpart, mask in zip(keys[half], masks):
            l = lax.dot_general(q2, kpart, (((1,), (1,)), ((), ())), preferred_element_type=F32)
            logits.append(l if mask is None else jnp.where(mask, l, NEG))
        m = sink
        for l in logits:
            m = jnp.maximum(m, jnp.max(l, axis=-1, keepdims=True))
        for l, vpart in zip(logits, values[half]):
            pv = jnp.dot(jnp.exp2(l - m).astype(BF16), vpart, preferred_element_type=F32)
            acc = pv if acc is None else acc + pv
        sink_terms.append(jnp.exp2(sink - m))
    lane = lax.broadcasted_iota(jnp.int32, (rows, LANES), 1)
    den = acc[:, LANES:] + jnp.where(lane < HEAD_DIM, sink_terms[0], sink_terms[1])
    return (acc[:, :LANES] * (1.0 / den) * z2).astype(BF16)


def _pair_rows(ref_slice, g):
    return jnp.concatenate([ref_slice(2 * g), ref_slice(2 * g + 1)], axis=0)


def _ctx_attn_kernel(sink_ref, ua_ref, yc_ref, *, nb, t, layer):
    for b in range(nb):
        kv = ua_ref[b, :, 2 * ATT_W:].astype(F32)
        kvar = _kv_variants(kv[:, :LANES] * (HEAD_DIM ** -0.5 * LOG2E), False)
        vvar = _kv_variants(kv[:, LANES:], True)
        for g in range(ATT_KV_HEADS):
            q2 = _pair_rows(lambda p: ua_ref[b, :, p * LANES:(p + 1) * LANES], g)
            z2 = _pair_rows(lambda p: ua_ref[b, :, ATT_W + p * LANES:ATT_W + (p + 1) * LANES], g)
            o = _attend_group(q2, [[kvar[g][0]], [kvar[g][1]]], [[vvar[g][0]], [vvar[g][1]]], [None],
                              sink_ref, layer, g, z2.astype(F32))
            yc_ref[b, :, 2 * g * LANES:(2 * g + 1) * LANES] = o[:t]
            yc_ref[b, :, (2 * g + 1) * LANES:(2 * g + 2) * LANES] = o[t:]


def _ctx_attention(ua, layer, sink):
    b, t, w = ua.shape
    nb = NB_CTX_ATTN
    return pl.pallas_call(
        functools.partial(_ctx_attn_kernel, nb=nb, t=t, layer=layer),
        grid=(b // nb,),
        in_specs=[pl.BlockSpec(memory_space=pltpu.SMEM),
                  pl.BlockSpec((nb, t, w), lambda i: (i, 0, 0))],
        out_specs=pl.BlockSpec((nb, t, ATT_W), lambda i: (i, 0, 0)),
        out_shape=jax.ShapeDtypeStruct((b, t, ATT_W), BF16),
        compiler_params=_params(("parallel",)),
        name="ctx_attention",
    )(sink, ua)


def _lat_attn_kernel(sink_ref, qz_ref, kv_ref, ck_ref, cv_ref, yc_ref,
                     kl_ref, vl_ref, kc_ref, vc_ref, *, t, tq, tiles, layer):
    j = pl.program_id(1)
    scale = HEAD_DIM ** -0.5 * LOG2E

    @pl.when(j == 0)
    def _prepare():
        kv = kv_ref[...].astype(F32)
        kvar = _kv_variants(kv[:, :LANES] * scale, False)
        vvar = _kv_variants(kv[:, LANES:], True)
        cvar = _kv_variants(ck_ref[...] * scale, False)
        dvar = _kv_variants(cv_ref[...], True)
        for g in range(ATT_KV_HEADS):
            for half in range(2):
                i = 2 * g + half
                for ref, var in ((kl_ref, kvar), (vl_ref, vvar)):
                    pad = jnp.zeros((WINDOW, ref.shape[-1]), BF16)
                    ref[i, 0:WINDOW, :] = pad
                    ref[i, WINDOW:WINDOW + t, :] = var[g][half]
                    ref[i, WINDOW + t:, :] = pad
                kc_ref[i] = cvar[g][half]
                vc_ref[i] = dvar[g][half]

    nloc = tq + 2 * WINDOW
    rr = lax.broadcasted_iota(jnp.int32, (2 * tq, nloc), 0) & (tq - 1)
    ss = lax.broadcasted_iota(jnp.int32, (2 * tq, nloc), 1)
    for u in range(tiles):
        rows = slice(u * tq, (u + 1) * tq)
        first = (j * tiles + u) * tq
        r0 = pl.multiple_of(first, tq)
        band = ((ss - rr >= 0) & (ss - rr <= 2 * WINDOW)
                & (ss >= WINDOW - first) & (ss < t + WINDOW - first))
        for g in range(ATT_KV_HEADS):
            q2 = _pair_rows(lambda p: qz_ref[rows, p * LANES:(p + 1) * LANES], g)
            z2 = _pair_rows(lambda p: qz_ref[rows, ATT_W + p * LANES:ATT_W + (p + 1) * LANES], g)
            keys = [[kl_ref[2 * g + half, pl.ds(r0, nloc), :], kc_ref[2 * g + half]] for half in range(2)]
            vals = [[vl_ref[2 * g + half, pl.ds(r0, nloc), :], vc_ref[2 * g + half]] for half in range(2)]
            o = _attend_group(q2, keys, vals, [band, None], sink_ref, layer, g, z2.astype(F32))
            yc_ref[rows, 2 * g * LANES:(2 * g + 1) * LANES] = o[:tq]
            yc_ref[rows, (2 * g + 1) * LANES:(2 * g + 2) * LANES] = o[tq:]


def _lat_attention(ua, cache_k, cache_v, layer, sink):
    b, t, _ = ua.shape
    past = cache_k.shape[2]
    tq = TQ_LATENT
    tiles = LAT_TILES_PER_STEP
    nvar = 2 * ATT_KV_HEADS
    cache_spec = pl.BlockSpec((None, None, past, ATT_KV_W), lambda i, j: (i, layer, 0, 0))
    return pl.pallas_call(
        functools.partial(_lat_attn_kernel, t=t, tq=tq, tiles=tiles, layer=layer),
        grid=(b, t // (tiles * tq)),
        in_specs=[pl.BlockSpec(memory_space=pltpu.SMEM),
                  pl.BlockSpec((None, tiles * tq, 2 * ATT_W), lambda i, j: (i, j, 0)),
                  pl.BlockSpec((None, t, 2 * ATT_KV_W), lambda i, j: (i, 0, 2 * ATT_W // (2 * ATT_KV_W))),
                  cache_spec, cache_spec],
        out_specs=pl.BlockSpec((None, tiles * tq, ATT_W), lambda i, j: (i, j, 0)),
        out_shape=jax.ShapeDtypeStruct((b, t, ATT_W), BF16),
        scratch_shapes=[pltpu.VMEM((nvar, t + 2 * WINDOW, LANES), BF16),
                        pltpu.VMEM((nvar, t + 2 * WINDOW, 2 * LANES), BF16),
                        pltpu.VMEM((nvar, past, LANES), BF16),
                        pltpu.VMEM((nvar, past, 2 * LANES), BF16)],
        compiler_params=_params(("parallel", "arbitrary")),
        name="lat_attention",
    )(sink, ua, ua, cache_k, cache_v)


def _outproj_kernel(x_ref, ya_ref, yb_ref, yc_ref, mod_ref, gpre_ref, gpost_ref,
                    wg_ref, wgt_ref, wa_ref, wb_ref, wc_ref, wo_ref, o_ref, *, layer, tm, row0, rows_per_mod):
    for r in range(tm // SUB_ROWS):
        rows = slice(r * SUB_ROWS, (r + 1) * SUB_ROWS)
        x = x_ref[rows, :]
        h, row = _modulated_norm(x, gpre_ref[layer:layer + 1, :], mod_ref, tm, row0, rows_per_mod)

        def gate_logits(c):
            lo, hi = c * D_MODEL, min((c + 1) * D_MODEL, C_MG)
            mg = jnp.dot(h, wg_ref[:, lo:hi].astype(BF16), preferred_element_type=F32)
            if hi - lo < D_MODEL:
                tail = jnp.dot(h, wgt_ref[...].astype(BF16), preferred_element_type=F32)
                mg = jnp.concatenate([mg, tail], axis=1)
            return mg

        merged = None
        for c, (y_ref, w_ref) in enumerate(((ya_ref, wa_ref), (yb_ref, wb_ref), (yc_ref, wc_ref))):
            term = _sigmoid(gate_logits(c)) * jnp.dot(y_ref[rows, :], w_ref[...].astype(BF16),
                                                      preferred_element_type=F32)
            merged = term if merged is None else merged + term
        out = jnp.dot(merged.astype(BF16), wo_ref[...].astype(BF16), preferred_element_type=F32)
        ms = jnp.mean(out * out, axis=-1, keepdims=True)
        normed = out * lax.rsqrt(ms + EPS) * gpost_ref[layer:layer + 1, :]
        gate = mod_ref[pl.ds(row, 1), 2 * D_MODEL:3 * D_MODEL]
        o_ref[rows, :] = x + gate * normed


def _outproj(x2, ya, yb, yc, layer, mod, g_pre, g_post, w_in, wa, wb, wc, wo, *, row0, rows_per_mod):
    n = x2.shape[0]
    tm = TM_OUTPROJ
    row_spec = lambda w: pl.BlockSpec((tm, w), lambda i: (i, 0))
    return pl.pallas_call(
        functools.partial(_outproj_kernel, layer=layer, tm=tm, row0=row0, rows_per_mod=rows_per_mod),
        grid=(n // tm,),
        in_specs=[row_spec(D_MODEL), row_spec(FOURIER_W), row_spec(RET_W), row_spec(ATT_W),
                  _layer_spec((MOD_ROWS, 3 * D_MODEL), layer),
                  _const_spec((DEPTH, D_MODEL)), _const_spec((DEPTH, D_MODEL)),
                  _layer_spec((D_MODEL, C_MG), layer, col_block=1),
                  _layer_spec((D_MODEL, C_END - 2 * C_MG), layer, col_block=2 * C_MG // (C_END - 2 * C_MG)),
                  _layer_spec((FOURIER_W, D_MODEL), layer), _layer_spec((RET_W, D_MODEL), layer),
                  _layer_spec((ATT_W, D_MODEL), layer), _layer_spec((D_MODEL, D_MODEL), layer)],
        out_specs=row_spec(D_MODEL),
        out_shape=jax.ShapeDtypeStruct((n, D_MODEL), F32),
        compiler_params=_params(("parallel",)),
        name="outproj",
    )(x2, ya, yb, yc, mod, g_pre, g_post, w_in, w_in, wa, wb, wc, wo)


def _layer(x, layer, mod, p, *, latent, dft_chan, dft_pos, rope_tabs=None, cache_k=None, cache_v=None,
           state=None, kv_carry=(), state_carry=()):
    b, t, _ = x.shape
    n = b * t
    row0, rows_per_mod = (1, t) if latent else (0, n)
    x2 = x.reshape(n, D_MODEL)
    xcs, fz, ur, ua, *kv32 = _inproj(x2, layer, mod, p["g_pre"], p["w_in"], dft_chan,
                                     seq=t, row0=row0, rows_per_mod=rows_per_mod, emit_kv=not latent,
                                     kv_carry=kv_carry, rope_tabs=rope_tabs if latent else ())
    ya = _fourier(xcs.reshape(b, t, -1), fz.reshape(b, t, -1), dft_pos, p["w_four"], layer,
                  bg=b if latent else BG_CTX_FOURIER, tq=min(t, TQ_FOURIER))
    ur3 = ur.reshape(b, t, -1)
    ua3 = ua.reshape(b, t, -1)
    if latent:
        (yb,), s_fin = _retention(ur3, layer, p["dec"], p["gn"], nb=1, emit_state=False, state=state), None
        yc = _lat_attention(ua3, cache_k, cache_v, layer, p["sink"])
    else:
        yb, s_fin = _retention(ur3, layer, p["dec"], p["gn"], nb=NB_CTX_RET, emit_state=True,
                               state_carry=state_carry)
        yc = _ctx_attention(ua3, layer, p["sink"])
    out = _outproj(x2, ya.reshape(n, -1), yb.reshape(n, -1), yc.reshape(n, -1), layer, mod,
                   p["g_pre"], p["g_post"], p["w_in"], p["w_pa"], p["w_pb"], p["w_pc"], p["w_out"],
                   row0=row0, rows_per_mod=rows_per_mod)
    return out.reshape(b, t, D_MODEL), kv32, s_fin


def kernel(x_prompt, x_sample, cache_k, cache_v, state_ret, c, c_ctx, w_mod, b_mod, g_pre, g_post, w_in,
           w_four, ret_decay, ret_gn, attn_sink, w_branch_a, w_branch_b, w_branch_c, w_out):
    batch, seq, _ = x_prompt.shape
    dec_batch, dec_seq, _ = x_sample.shape
    past = cache_k.shape[2]
    assert 1 + dec_batch <= MOD_ROWS

    cv = jnp.zeros((MOD_ROWS, D_MODEL), F32).at[0].set(c_ctx).at[1:1 + dec_batch].set(c)
    mod = _modulation(cv, w_mod, b_mod)

    dft_chan, dft_ctx = _dft_tables(seq)
    _, dft_lat = _dft_tables(dec_seq)
    rope_tabs = _rope_tables(dec_seq)
    ck = cache_k.reshape(dec_batch, DEPTH, past, ATT_KV_W)
    cvv = cache_v.reshape(dec_batch, DEPTH, past, ATT_KV_W)

    p = dict(
        g_pre=g_pre, g_post=g_post,
        w_in=w_in, w_four=w_four, w_pa=w_branch_a, w_pb=w_branch_b, w_pc=w_branch_c, w_out=w_out,
        dec=ret_decay, gn=ret_gn, sink=attn_sink)

    xp = x_prompt
    kv_all, s_all = (), ()
    for l in range(DEPTH):
        xp, kv_all, s_all = _layer(xp, l, mod, p, latent=False, dft_chan=dft_chan, dft_pos=dft_ctx,
                                   kv_carry=tuple(kv_all), state_carry=s_all)
        s_all = (s_all,)

    xs = x_sample
    for l in range(DEPTH):
        xs, _, _ = _layer(xs, l, mod, p, latent=True, dft_chan=dft_chan, dft_pos=dft_lat,
                             rope_tabs=rope_tabs, cache_k=ck, cache_v=cvv, state=state_ret)

    new_k, new_v = (jnp.transpose(a, (0, 1, 4, 2, 3)) for a in kv_all)
    return (xp, xs, new_k, new_v, s_all[0])
```

```python
import functools
import math

import numpy as np
import jax
import jax.numpy as jnp
from jax import lax
from jax.experimental import pallas as pl
from jax.experimental.pallas import tpu as pltpu

F32 = jnp.float32
BF16 = jnp.bfloat16

D_MODEL = 1024
DEPTH = 2
GRID_W = 64
HEAD_DIM = 64
FOURIER_GROUPS = 4
FOURIER_GROUP_W = 64
FOURIER_W = FOURIER_GROUPS * FOURIER_GROUP_W
RET_HEADS = 4
RET_DK = 64
RET_W = RET_HEADS * RET_DK
RET_CHUNK = 128
ATT_Q_HEADS = 8
ATT_KV_HEADS = 2
ATT_W = ATT_Q_HEADS * HEAD_DIM
ATT_KV_W = ATT_KV_HEADS * HEAD_DIM
WINDOW = 128
ROPE_BASE = 10000.0
EPS = 1e-6
MOD_ROWS = 8
LANES = 128
NEG = -1e30
LOG2E = math.log2(math.e)
VMEM_LIMIT = 56 * 1024 * 1024

C_FX, C_RQ, C_RZ_END = 0, 512, 1536
C_AQ, C_AK, C_AV, C_AZ, C_MG, C_END = 1536, 2048, 2176, 2304, 2816, 5888

TM_INPROJ = 1024
TM_OUTPROJ = 1024
SUB_ROWS = 512
TQ_LATENT = 256
LAT_TILES_PER_STEP = 4
TQ_FOURIER = 512
NB_CTX_RET = 8
NB_CTX_ATTN = 4
BG_CTX_FOURIER = 8
RET_UNROLL = 16


def _sigmoid(x):
    return 0.5 * jnp.tanh(0.5 * x) + 0.5


def _silu(x):
    return x * _sigmoid(x)


def _params(sem):
    return pltpu.CompilerParams(dimension_semantics=sem, vmem_limit_bytes=VMEM_LIMIT)


def _const_spec(shape):
    nd = len(shape)
    return pl.BlockSpec(shape, lambda *_: (0,) * nd, pipeline_mode=pl.Buffered(1))


def _layer_spec(shape, layer, col_block=0):
    idx = (layer,) + (0,) * (len(shape) - 1) + (col_block,)
    return pl.BlockSpec((None,) + tuple(shape), lambda *_: idx, pipeline_mode=pl.Buffered(1))


def _carry_args(carry, first_input, first_output):
    specs = [pl.BlockSpec(memory_space=pl.ANY)] * len(carry)
    aliases = {first_input + k: first_output + k for k in range(len(carry))}
    return specs, list(carry), aliases


def _slab_spec(lead, rest, layer, whole):
    zeros = (0,) * len(rest)
    if whole:
        return pl.BlockSpec((lead, DEPTH) + tuple(rest), lambda i: (i, 0) + zeros)
    return pl.BlockSpec((lead, None) + tuple(rest), lambda i: (i, layer) + zeros)


def _store_slab(ref, lead_idx, layer, whole, value, rest_idx=()):
    if not whole:
        ref[(lead_idx,) + tuple(rest_idx)] = value
        return
    for l in range(DEPTH):
        ref[(lead_idx, l) + tuple(rest_idx)] = value if l == layer else jnp.zeros_like(value)


def _dft_tables(t):
    c = np.arange(FOURIER_GROUP_W)
    ang = 2.0 * np.pi * ((c[:, None] * c[None, :]) % FOURIER_GROUP_W) / FOURIER_GROUP_W
    eye = np.eye(FOURIER_GROUPS)
    s64 = FOURIER_GROUP_W ** -0.5
    chan = np.concatenate([np.kron(eye, np.cos(ang) * s64), np.kron(eye, np.sin(ang) * s64)], axis=1)
    p = np.arange(t)
    angt = 2.0 * np.pi * ((p[:, None] * p[None, :]) % t) / t
    pos = np.concatenate([np.cos(angt), -np.sin(angt)], axis=1) * (t ** -0.5)
    return jnp.asarray(chan, F32).astype(BF16), jnp.asarray(pos, F32).astype(BF16)


def _rope_tables(t):
    quarter = HEAD_DIM // 4
    lane = np.arange(LANES) % HEAD_DIM
    inv = ROPE_BASE ** (-(lane % quarter).astype(np.float64) / quarter)
    n = np.arange(t)
    pos = np.where(lane[None, :] < HEAD_DIM // 2, (n // GRID_W)[:, None], (n % GRID_W)[:, None])
    ang = pos.astype(np.float64) * inv[None, :]
    sign = np.where((lane % (2 * quarter)) < quarter, -1.0, 1.0)
    return jnp.asarray(np.cos(ang), F32), jnp.asarray(np.sin(ang) * sign[None, :], F32)


def _rope(x, cos, sin):
    lane = lax.broadcasted_iota(jnp.int32, x.shape, 1)
    first = (lane & 31) < 16
    partner = jnp.where(first, pltpu.roll(x, LANES - 16, axis=1), pltpu.roll(x, 16, axis=1))
    return x * cos + partner * sin


def _mod_kernel(cv_ref, w_ref, b_ref, o_ref):
    a = _silu(cv_ref[...]).astype(BF16)
    bias = b_ref[pl.ds(pl.program_id(0), 1), :]
    o_ref[...] = jnp.dot(a, w_ref[...].astype(BF16), preferred_element_type=F32) + bias


def _modulation(cv, w_mod, b_mod):
    tn = 1024
    return pl.pallas_call(
        _mod_kernel,
        grid=(DEPTH, 3 * D_MODEL // tn),
        in_specs=[pl.BlockSpec((MOD_ROWS, D_MODEL), lambda l, j: (0, 0)),
                  pl.BlockSpec((None, D_MODEL, tn), lambda l, j: (l, 0, j)),
                  pl.BlockSpec((DEPTH, tn), lambda l, j: (0, j))],
        out_specs=pl.BlockSpec((None, MOD_ROWS, tn), lambda l, j: (l, 0, j)),
        out_shape=jax.ShapeDtypeStruct((DEPTH, MOD_ROWS, 3 * D_MODEL), F32),
        compiler_params=_params(("parallel", "parallel")),
        name="modulation",
    )(cv, w_mod, b_mod)


def _modulated_norm(x, g, mod_ref, tm, row0, rows_per_mod):
    row = row0 + (pl.program_id(0) * tm) // rows_per_mod
    ms = jnp.mean(x * x, axis=-1, keepdims=True)
    y = x * lax.rsqrt(ms + EPS) * g
    shift = mod_ref[pl.ds(row, 1), 0:D_MODEL]
    scale = mod_ref[pl.ds(row, 1), D_MODEL:2 * D_MODEL]
    return (y * (1.0 + scale) + shift).astype(BF16), row


def _inproj_kernel(x_ref, mod_ref, g_ref, w_ref, dft_ref, *refs, layer, tm, seq, row0, rows_per_mod,
                   n_carry, rope):
    cos_ref, sin_ref = refs[:2] if rope else (None, None)
    refs = refs[2 if rope else 0:]
    xcs_ref, fz_ref, ur_ref, ua_ref, *kv_refs = refs[n_carry:]
    for r in range(tm // SUB_ROWS):
        rows = slice(r * SUB_ROWS, (r + 1) * SUB_ROWS)
        h, _ = _modulated_norm(x_ref[rows, :], g_ref[layer:layer + 1, :], mod_ref, tm, row0, rows_per_mod)

        def mm(c0, c1):
            return jnp.dot(h, w_ref[:, c0:c1].astype(BF16), preferred_element_type=F32)

        def roped(a):
            if not rope:
                return a.astype(BF16)
            cs, sn = cos_ref[rows, :], sin_ref[rows, :]
            return jnp.concatenate([_rope(a[:, c:c + LANES], cs, sn) for c in range(0, a.shape[1], LANES)],
                                   axis=1).astype(BF16)

        f = mm(C_FX, C_RQ)
        xcs_ref[rows, :] = jnp.dot(f[:, :FOURIER_W].astype(BF16), dft_ref[...],
                                   preferred_element_type=F32).astype(BF16)
        fz_ref[rows, :] = _silu(f[:, FOURIER_W:]).astype(BF16)
        ur = mm(C_RQ, C_RZ_END)
        ur_ref[rows, 0:2 * RET_W] = roped(ur[:, :2 * RET_W])
        ur_ref[rows, 2 * RET_W:3 * RET_W] = ur[:, 2 * RET_W:3 * RET_W].astype(BF16)
        ur_ref[rows, 3 * RET_W:] = _silu(ur[:, 3 * RET_W:]).astype(BF16)
        ua_ref[rows, 0:ATT_W] = roped(mm(C_AQ, C_AK))
        ua_ref[rows, ATT_W:2 * ATT_W] = _silu(mm(C_AZ, C_MG)).astype(BF16)
        kv = mm(C_AK, C_AZ)
        if kv_refs:
            for s in range(SUB_ROWS // seq):
                blk = kv[s * seq:(s + 1) * seq]
                bi = (r * SUB_ROWS) // seq + s
                for ref, cols in zip(kv_refs, (blk[:, :ATT_KV_W], blk[:, ATT_KV_W:])):
                    _store_slab(ref, bi, layer, n_carry == 0,
                                cols.T.reshape(ATT_KV_HEADS, HEAD_DIM, seq))
        ua_ref[rows, 2 * ATT_W:2 * ATT_W + ATT_KV_W] = roped(kv[:, :ATT_KV_W])
        ua_ref[rows, 2 * ATT_W + ATT_KV_W:] = kv[:, ATT_KV_W:].astype(BF16)


def _inproj(x2, layer, mod, g_pre, w_in, dft, *, seq, row0, rows_per_mod, emit_kv, kv_carry=(),
            rope_tabs=()):
    n = x2.shape[0]
    tm = TM_INPROJ
    row_spec = lambda w: pl.BlockSpec((tm, w), lambda i: (i, 0))
    widths = (2 * FOURIER_W, FOURIER_W, 4 * RET_W, 2 * ATT_W + 2 * ATT_KV_W)
    out_shape = [jax.ShapeDtypeStruct((n, w), BF16) for w in widths]
    out_specs = [row_spec(w) for w in widths]
    if emit_kv:
        assert SUB_ROWS % seq == 0
        out_shape += [jax.ShapeDtypeStruct((n // seq, DEPTH, ATT_KV_HEADS, HEAD_DIM, seq), F32)] * 2
        out_specs += [_slab_spec(tm // seq, (ATT_KV_HEADS, HEAD_DIM, seq), layer, not kv_carry)] * 2
    args = [x2, mod, g_pre, w_in, dft, *rope_tabs]
    rope_specs = [pl.BlockSpec((tm, LANES), lambda i: (i % (seq // tm), 0))] * len(rope_tabs)
    carry_specs, carry, aliases = _carry_args(kv_carry, len(args), len(widths))
    return pl.pallas_call(
        functools.partial(_inproj_kernel, layer=layer, tm=tm, seq=seq, row0=row0, rows_per_mod=rows_per_mod,
                          n_carry=len(carry), rope=bool(rope_tabs)),
        grid=(n // tm,),
        in_specs=[row_spec(D_MODEL),
                  _layer_spec((MOD_ROWS, 3 * D_MODEL), layer),
                  _const_spec((DEPTH, D_MODEL)),
                  _layer_spec((D_MODEL, C_MG), layer),
                  _const_spec((FOURIER_W, 2 * FOURIER_W))] + rope_specs + carry_specs,
        out_specs=out_specs,
        out_shape=out_shape,
        input_output_aliases=aliases,
        compiler_params=_params(("parallel",)),
        name="inproj",
    )(*args, *carry)


def _fourier_kernel(ct_ref, xcs_ref, fz_ref, w_ref, ya_ref, xcat_ref, *, bg, t):
    @pl.when(pl.program_id(1) == 0)
    def _gather():
        for b in range(bg):
            cols = slice(b * FOURIER_W, (b + 1) * FOURIER_W)
            xcat_ref[0:t, cols] = xcs_ref[b, :, 0:FOURIER_W]
            xcat_ref[t:2 * t, cols] = xcs_ref[b, :, FOURIER_W:]

    yr = jnp.dot(ct_ref[...], xcat_ref[...], preferred_element_type=F32).astype(BF16)
    w = w_ref[...].astype(BF16)
    for b in range(bg):
        ya = jnp.dot(yr[:, b * FOURIER_W:(b + 1) * FOURIER_W], w, preferred_element_type=F32)
        ya_ref[b] = (ya * fz_ref[b].astype(F32)).astype(BF16)


def _fourier(xcs, fz, ct, w_four, layer, *, bg, tq):
    b, t, _ = xcs.shape
    return pl.pallas_call(
        functools.partial(_fourier_kernel, bg=bg, t=t),
        grid=(b // bg, t // tq),
        in_specs=[pl.BlockSpec((tq, 2 * t), lambda i, j: (j, 0)),
                  pl.BlockSpec((bg, t, 2 * FOURIER_W), lambda i, j: (i, 0, 0)),
                  pl.BlockSpec((bg, tq, FOURIER_W), lambda i, j: (i, j, 0)),
                  _layer_spec((FOURIER_W, FOURIER_W), layer)],
        out_specs=pl.BlockSpec((bg, tq, FOURIER_W), lambda i, j: (i, j, 0)),
        out_shape=jax.ShapeDtypeStruct((b, t, FOURIER_W), BF16),
        scratch_shapes=[pltpu.VMEM((2 * t, bg * FOURIER_W), BF16)],
        compiler_params=_params(("parallel", "arbitrary")),
        name="fourier",
    )(ct, xcs, fz, w_four)


def _log_sigmoid(x):
    return jnp.minimum(x, 0.0) - jnp.log(1.0 + jnp.exp(-jnp.abs(x)))


def _head_blocks(a, width):
    lane = lax.broadcasted_iota(jnp.int32, a.shape, 1)
    zero = jnp.zeros_like(a)
    return jnp.concatenate(
        [jnp.where((lane >= h * width) & (lane < (h + 1) * width), a, zero) for h in range(RET_HEADS)],
        axis=0)


def _per_head(dec_ref, layer, direction, head_of):
    out = jnp.full(head_of.shape, dec_ref[layer, direction, RET_HEADS - 1], F32)
    for h in range(RET_HEADS - 2, -1, -1):
        out = jnp.where(head_of == h, dec_ref[layer, direction, h], out)
    return out


def _ret_kernel(*refs, nb, t, layer, has_s0, n_carry, emit_state):
    refs = list(refs)
    dec_ref = refs.pop(0)
    ur_ref = refs.pop(0)
    s0_ref = refs.pop(0) if has_s0 else None
    gn_ref = refs.pop(0)
    del refs[:n_carry]
    yb_ref = refs.pop(0)
    sfin_ref = refs.pop(0) if emit_state else None
    (kr_ref, dsf_ref, dsb_ref, sfs_ref, sbs_ref, st_ref) = refs
    c = RET_CHUNK
    nc = t // c
    nch = nb * nc
    group = RET_UNROLL

    lane_head = lax.broadcasted_iota(jnp.int32, (1, RET_W), 1) // RET_DK
    lgf = _log_sigmoid(_per_head(dec_ref, layer, 0, lane_head))
    lgb = _log_sigmoid(_per_head(dec_ref, layer, 1, lane_head))
    ri = lax.broadcasted_iota(jnp.int32, (c, RET_W), 0).astype(F32)
    read_f = jnp.exp((ri + 1.0) * lgf)
    read_b = jnp.exp((c - ri) * lgb)
    write_f = jnp.exp((c - 1.0 - ri) * lgf)
    write_b = jnp.exp(ri * lgb)
    carry_f = jnp.exp(c * lgf)
    carry_b = jnp.exp(c * lgb)
    ii = lax.broadcasted_iota(jnp.int32, (RET_HEADS * c, c), 0)
    jj = lax.broadcasted_iota(jnp.int32, (RET_HEADS * c, c), 1)
    row_head = ii // c
    diff = ((ii & (c - 1)) - jj).astype(F32)
    decay = (jnp.where(diff >= 0, jnp.exp(jnp.maximum(diff, 0.0)
                                          * _log_sigmoid(_per_head(dec_ref, layer, 0, row_head))), 0.0)
             + jnp.where(diff <= 0, jnp.exp(jnp.maximum(-diff, 0.0)
                                            * _log_sigmoid(_per_head(dec_ref, layer, 1, row_head))), 0.0))
    r2 = lax.broadcasted_iota(jnp.int32, (RET_W, RET_W), 0)
    c2 = lax.broadcasted_iota(jnp.int32, (RET_W, RET_W), 1)
    same_head = (r2 // RET_DK) == (c2 // RET_DK)
    group_mean = jnp.where(same_head, 1.0 / RET_DK, 0.0).astype(BF16)
    out_head = lax.broadcasted_iota(jnp.int32, (c, RET_W), 1) // RET_DK

    def chunk_pos(ci):
        return ci // nc, pl.multiple_of((ci % nc) * c, c)

    def load_qk(col, b, r0):
        return ur_ref[b, pl.ds(r0, c), col:col + RET_W].astype(F32)

    def increments(ci, carry):
        b, r0 = chunk_pos(ci)
        k = load_qk(RET_W, b, r0) * (RET_DK ** -0.5)
        v = ur_ref[b, pl.ds(r0, c), 2 * RET_W:3 * RET_W]
        kr_ref[pl.ds(pl.multiple_of(ci * c, c), c), :] = k.astype(BF16)
        kw = jnp.concatenate([(k * write_f).astype(BF16), (k * write_b).astype(BF16)], axis=1)
        d = lax.dot_general(kw, v, (((0,), (0,)), ((), ())), preferred_element_type=F32)
        dsf_ref[ci] = jnp.where(same_head, d[:RET_W], 0.0)
        dsb_ref[ci] = jnp.where(same_head, d[RET_W:], 0.0)
        return carry

    lax.fori_loop(0, nch, increments, 0, unroll=group)

    def scan(b, direction, ds_ref, out_ref, carry_decay):
        st_ref[...] = jnp.zeros((RET_W, RET_W), F32)
        if has_s0:
            for h in range(RET_HEADS):
                sl = slice(h * RET_DK, (h + 1) * RET_DK)
                st_ref[sl, sl] = s0_ref[b, direction, h]

        def step(n, carry):
            ci = b * nc + (n if direction == 0 else nc - 1 - n)
            out_ref[ci] = st_ref[...].astype(BF16)
            st_ref[...] = carry_decay * st_ref[...] + ds_ref[ci]
            return carry

        lax.fori_loop(0, nc, step, 0, unroll=min(nc, 4))
        if emit_state:
            for h in range(RET_HEADS):
                sl = slice(h * RET_DK, (h + 1) * RET_DK)
                _store_slab(sfin_ref, b, layer, n_carry == 0, st_ref[sl, sl], (direction, h))

    for b in range(nb):
        scan(b, 0, dsf_ref, sfs_ref, carry_f)
        scan(b, 1, dsb_ref, sbs_ref, carry_b)

    def chunk_output(ci, b, r0):
        q = load_qk(0, b, r0)
        kb = kr_ref[pl.ds(pl.multiple_of(ci * c, c), c), :]
        v = ur_ref[b, pl.ds(r0, c), 2 * RET_W:3 * RET_W]
        att = lax.dot_general(_head_blocks(q.astype(BF16), RET_DK), kb,
                              (((1,), (1,)), ((), ())), preferred_element_type=F32)
        o4 = jnp.dot((att * decay).astype(BF16), v, preferred_element_type=F32)
        o = o4[(RET_HEADS - 1) * c:]
        for h in range(RET_HEADS - 2, -1, -1):
            o = jnp.where(out_head == h, o4[h * c:(h + 1) * c], o)
        o = o + jnp.dot((q * read_f).astype(BF16), sfs_ref[ci], preferred_element_type=F32)
        return o + jnp.dot((q * read_b).astype(BF16), sbs_ref[ci], preferred_element_type=F32)

    def outputs(it, carry):
        pos = [chunk_pos(it * group + u) for u in range(group)]
        o = jnp.concatenate([chunk_output(it * group + u, *pos[u]) for u in range(group)], axis=0)
        ms = jnp.dot((o * o).astype(BF16), group_mean, preferred_element_type=F32)
        y = o * lax.rsqrt(ms + EPS) * gn_ref[layer:layer + 1, :]
        for u, (b, r0) in enumerate(pos):
            z = ur_ref[b, pl.ds(r0, c), 3 * RET_W:4 * RET_W].astype(F32)
            yb_ref[b, pl.ds(r0, c), :] = (y[u * c:(u + 1) * c] * z).astype(BF16)
        return carry

    lax.fori_loop(0, nch // group, outputs, 0)


def _retention(ur, layer, dec, gn, *, nb, emit_state, state=None, state_carry=()):
    b, t, _ = ur.shape
    nch = nb * (t // RET_CHUNK)
    assert nch % RET_UNROLL == 0
    has_s0 = state is not None
    in_specs = [pl.BlockSpec(memory_space=pltpu.SMEM),
                pl.BlockSpec((nb, t, 4 * RET_W), lambda i: (i, 0, 0))]
    args = [dec, ur]
    if has_s0:
        in_specs.append(pl.BlockSpec((nb, None, 2, RET_HEADS, RET_DK, RET_DK),
                                     lambda i: (i, layer, 0, 0, 0, 0)))
        args.append(state)
    in_specs.append(_const_spec((DEPTH, RET_W)))
    args.append(gn)
    carry_specs, carry, aliases = _carry_args(state_carry, len(args), 1)
    state_scratch = lambda dt: pltpu.VMEM((nch, RET_W, RET_W), dt)
    out_specs = [pl.BlockSpec((nb, t, RET_W), lambda i: (i, 0, 0))]
    out_shape = [jax.ShapeDtypeStruct((b, t, RET_W), BF16)]
    if emit_state:
        out_specs.append(_slab_spec(nb, (2, RET_HEADS, RET_DK, RET_DK), layer, not state_carry))
        out_shape.append(jax.ShapeDtypeStruct((b, DEPTH, 2, RET_HEADS, RET_DK, RET_DK), F32))
    return pl.pallas_call(
        functools.partial(_ret_kernel, nb=nb, t=t, layer=layer, has_s0=has_s0,
                          n_carry=len(carry), emit_state=emit_state),
        grid=(b // nb,),
        in_specs=in_specs + carry_specs,
        out_specs=out_specs,
        out_shape=out_shape,
        input_output_aliases=aliases,
        scratch_shapes=[pltpu.VMEM((nb * t, RET_W), BF16),
                        state_scratch(F32), state_scratch(F32), state_scratch(BF16), state_scratch(BF16),
                        pltpu.VMEM((RET_W, RET_W), F32)],
        compiler_params=_params(("parallel",)),
        name="retention",
    )(*args, *carry)


def _kv_variants(a, ones_block):
    lane_half = lax.broadcasted_iota(jnp.int32, a.shape, 1) // HEAD_DIM
    swapped = pltpu.roll(a, HEAD_DIM, axis=1)
    out = []
    for g in range(ATT_KV_HEADS):
        row = []
        for half in range(2):
            var = jnp.where(lane_half == half, a if half == g else swapped, 0.0).astype(BF16)
            if ones_block:
                ones = jnp.where(lane_half == half, 1.0, 0.0).astype(BF16)
                var = jnp.concatenate([var, ones], axis=1)
            row.append(var)
        out.append(row)
    return out


def _attend_group(q2, keys, values, masks, sink_ref, layer, g, z2):
    rows = q2.shape[0]
    upper = lax.broadcasted_iota(jnp.int32, (rows, 1), 0) < rows // 2
    acc = None
    sink_terms = []
    for half in range(2):
        sink = jnp.where(upper, sink_ref[layer, g, half], sink_ref[layer, g, 2 + half]) * LOG2E
        logits = []
        for kpart, mask in zip(keys[half], masks):
            l = lax.dot_general(q2, kpart, (((1,), (1,)), ((), ())), preferred_element_type=F32)
            logits.append(l if mask is None else jnp.where(mask, l, NEG))
        m = sink
        for l in logits:
            m = jnp.maximum(m, jnp.max(l, axis=-1, keepdims=True))
        for l, vpart in zip(logits, values[half]):
            pv = jnp.dot(jnp.exp2(l - m).astype(BF16), vpart, preferred_element_type=F32)
            acc = pv if acc is None else acc + pv
        sink_terms.append(jnp.exp2(sink - m))
    lane = lax.broadcasted_iota(jnp.int32, (rows, LANES), 1)
    den = acc[:, LANES:] + jnp.where(lane < HEAD_DIM, sink_terms[0], sink_terms[1])
    return (acc[:, :LANES] * (1.0 / den) * z2).astype(BF16)


def _pair_rows(ref_slice, g):
    return jnp.concatenate([ref_slice(2 * g), ref_slice(2 * g + 1)], axis=0)


def _ctx_attn_kernel(sink_ref, ua_ref, yc_ref, *, nb, t, layer):
    for b in range(nb):
        kv = ua_ref[b, :, 2 * ATT_W:].astype(F32)
        kvar = _kv_variants(kv[:, :LANES] * (HEAD_DIM ** -0.5 * LOG2E), False)
        vvar = _kv_variants(kv[:, LANES:], True)
        for g in range(ATT_KV_HEADS):
            q2 = _pair_rows(lambda p: ua_ref[b, :, p * LANES:(p + 1) * LANES], g)
            z2 = _pair_rows(lambda p: ua_ref[b, :, ATT_W + p * LANES:ATT_W + (p + 1) * LANES], g)
            o = _attend_group(q2, [[kvar[g][0]], [kvar[g][1]]], [[vvar[g][0]], [vvar[g][1]]], [None],
                              sink_ref, layer, g, z2.astype(F32))
            yc_ref[b, :, 2 * g * LANES:(2 * g + 1) * LANES] = o[:t]
            yc_ref[b, :, (2 * g + 1) * LANES:(2 * g + 2) * LANES] = o[t:]


def _ctx_attention(ua, layer, sink):
    b, t, w = ua.shape
    nb = NB_CTX_ATTN
    return pl.pallas_call(
        functools.partial(_ctx_attn_kernel, nb=nb, t=t, layer=layer),
        grid=(b // nb,),
        in_specs=[pl.BlockSpec(memory_space=pltpu.SMEM),
                  pl.BlockSpec((nb, t, w), lambda i: (i, 0, 0))],
        out_specs=pl.BlockSpec((nb, t, ATT_W), lambda i: (i, 0, 0)),
        out_shape=jax.ShapeDtypeStruct((b, t, ATT_W), BF16),
        compiler_params=_params(("parallel",)),
        name="ctx_attention",
    )(sink, ua)


def _lat_attn_kernel(sink_ref, qz_ref, kv_ref, ck_ref, cv_ref, yc_ref,
                     kl_ref, vl_ref, kc_ref, vc_ref, *, t, tq, tiles, layer):
    j = pl.program_id(1)
    scale = HEAD_DIM ** -0.5 * LOG2E

    @pl.when(j == 0)
    def _prepare():
        kv = kv_ref[...].astype(F32)
        kvar = _kv_variants(kv[:, :LANES] * scale, False)
        vvar = _kv_variants(kv[:, LANES:], True)
        cvar = _kv_variants(ck_ref[...] * scale, False)
        dvar = _kv_variants(cv_ref[...], True)
        for g in range(ATT_KV_HEADS):
            for half in range(2):
                i = 2 * g + half
                for ref, var in ((kl_ref, kvar), (vl_ref, vvar)):
                    pad = jnp.zeros((WINDOW, ref.shape[-1]), BF16)
                    ref[i, 0:WINDOW, :] = pad
                    ref[i, WINDOW:WINDOW + t, :] = var[g][half]
                    ref[i, WINDOW + t:, :] = pad
                kc_ref[i] = cvar[g][half]
                vc_ref[i] = dvar[g][half]

    nloc = tq + 2 * WINDOW
    rr = lax.broadcasted_iota(jnp.int32, (2 * tq, nloc), 0) & (tq - 1)
    ss = lax.broadcasted_iota(jnp.int32, (2 * tq, nloc), 1)
    for u in range(tiles):
        rows = slice(u * tq, (u + 1) * tq)
        first = (j * tiles + u) * tq
        r0 = pl.multiple_of(first, tq)
        band = ((ss - rr >= 0) & (ss - rr <= 2 * WINDOW)
                & (ss >= WINDOW - first) & (ss < t + WINDOW - first))
        for g in range(ATT_KV_HEADS):
            q2 = _pair_rows(lambda p: qz_ref[rows, p * LANES:(p + 1) * LANES], g)
            z2 = _pair_rows(lambda p: qz_ref[rows, ATT_W + p * LANES:ATT_W + (p + 1) * LANES], g)
            keys = [[kl_ref[2 * g + half, pl.ds(r0, nloc), :], kc_ref[2 * g + half]] for half in range(2)]
            vals = [[vl_ref[2 * g + half, pl.ds(r0, nloc), :], vc_ref[2 * g + half]] for half in range(2)]
            o = _attend_group(q2, keys, vals, [band, None], sink_ref, layer, g, z2.astype(F32))
            yc_ref[rows, 2 * g * LANES:(2 * g + 1) * LANES] = o[:tq]
            yc_ref[rows, (2 * g + 1) * LANES:(2 * g + 2) * LANES] = o[tq:]


def _lat_attention(ua, cache_k, cache_v, layer, sink):
    b, t, _ = ua.shape
    past = cache_k.shape[2]
    tq = TQ_LATENT
    tiles = LAT_TILES_PER_STEP
    nvar = 2 * ATT_KV_HEADS
    cache_spec = pl.BlockSpec((None, None, past, ATT_KV_W), lambda i, j: (i, layer, 0, 0))
    return pl.pallas_call(
        functools.partial(_lat_attn_kernel, t=t, tq=tq, tiles=tiles, layer=layer),
        grid=(b, t // (tiles * tq)),
        in_specs=[pl.BlockSpec(memory_space=pltpu.SMEM),
                  pl.BlockSpec((None, tiles * tq, 2 * ATT_W), lambda i, j: (i, j, 0)),
                  pl.BlockSpec((None, t, 2 * ATT_KV_W), lambda i, j: (i, 0, 2 * ATT_W // (2 * ATT_KV_W))),
                  cache_spec, cache_spec],
        out_specs=pl.BlockSpec((None, tiles * tq, ATT_W), lambda i, j: (i, j, 0)),
        out_shape=jax.ShapeDtypeStruct((b, t, ATT_W), BF16),
        scratch_shapes=[pltpu.VMEM((nvar, t + 2 * WINDOW, LANES), BF16),
                        pltpu.VMEM((nvar, t + 2 * WINDOW, 2 * LANES), BF16),
                        pltpu.VMEM((nvar, past, LANES), BF16),
                        pltpu.VMEM((nvar, past, 2 * LANES), BF16)],
        compiler_params=_params(("parallel", "arbitrary")),
        name="lat_attention",
    )(sink, ua, ua, cache_k, cache_v)


def _outproj_kernel(x_ref, ya_ref, yb_ref, yc_ref, mod_ref, gpre_ref, gpost_ref,
                    wg_ref, wgt_ref, wa_ref, wb_ref, wc_ref, wo_ref, o_ref, *, layer, tm, row0, rows_per_mod):
    for r in range(tm // SUB_ROWS):
        rows = slice(r * SUB_ROWS, (r + 1) * SUB_ROWS)
        x = x_ref[rows, :]
        h, row = _modulated_norm(x, gpre_ref[layer:layer + 1, :], mod_ref, tm, row0, rows_per_mod)

        def gate_logits(c):
            lo, hi = c * D_MODEL, min((c + 1) * D_MODEL, C_MG)
            mg = jnp.dot(h, wg_ref[:, lo:hi].astype(BF16), preferred_element_type=F32)
            if hi - lo < D_MODEL:
                tail = jnp.dot(h, wgt_ref[...].astype(BF16), preferred_element_type=F32)
                mg = jnp.concatenate([mg, tail], axis=1)
            return mg

        merged = None
        for c, (y_ref, w_ref) in enumerate(((ya_ref, wa_ref), (yb_ref, wb_ref), (yc_ref, wc_ref))):
            term = _sigmoid(gate_logits(c)) * jnp.dot(y_ref[rows, :], w_ref[...].astype(BF16),
                                                      preferred_element_type=F32)
            merged = term if merged is None else merged + term
        out = jnp.dot(merged.astype(BF16), wo_ref[...].astype(BF16), preferred_element_type=F32)
        ms = jnp.mean(out * out, axis=-1, keepdims=True)
        normed = out * lax.rsqrt(ms + EPS) * gpost_ref[layer:layer + 1, :]
        gate = mod_ref[pl.ds(row, 1), 2 * D_MODEL:3 * D_MODEL]
        o_ref[rows, :] = x + gate * normed


def _outproj(x2, ya, yb, yc, layer, mod, g_pre, g_post, w_in, wa, wb, wc, wo, *, row0, rows_per_mod):
    n = x2.shape[0]
    tm = TM_OUTPROJ
    row_spec = lambda w: pl.BlockSpec((tm, w), lambda i: (i, 0))
    return pl.pallas_call(
        functools.partial(_outproj_kernel, layer=layer, tm=tm, row0=row0, rows_per_mod=rows_per_mod),
        grid=(n // tm,),
        in_specs=[row_spec(D_MODEL), row_spec(FOURIER_W), row_spec(RET_W), row_spec(ATT_W),
                  _layer_spec((MOD_ROWS, 3 * D_MODEL), layer),
                  _const_spec((DEPTH, D_MODEL)), _const_spec((DEPTH, D_MODEL)),
                  _layer_spec((D_MODEL, C_MG), layer, col_block=1),
                  _layer_spec((D_MODEL, C_END - 2 * C_MG), layer, col_block=2 * C_MG // (C_END - 2 * C_MG)),
                  _layer_spec((FOURIER_W, D_MODEL), layer), _layer_spec((RET_W, D_MODEL), layer),
                  _layer_spec((ATT_W, D_MODEL), layer), _layer_spec((D_MODEL, D_MODEL), layer)],
        out_specs=row_spec(D_MODEL),
        out_shape=jax.ShapeDtypeStruct((n, D_MODEL), F32),
        compiler_params=_params(("parallel",)),
        name="outproj",
    )(x2, ya, yb, yc, mod, g_pre, g_post, w_in, w_in, wa, wb, wc, wo)


def _layer(x, layer, mod, p, *, latent, dft_chan, dft_pos, rope_tabs=None, cache_k=None, cache_v=None,
           state=None, kv_carry=(), state_carry=()):
    b, t, _ = x.shape
    n = b * t
    row0, rows_per_mod = (1, t) if latent else (0, n)
    x2 = x.reshape(n, D_MODEL)
    xcs, fz, ur, ua, *kv32 = _inproj(x2, layer, mod, p["g_pre"], p["w_in"], dft_chan,
                                     seq=t, row0=row0, rows_per_mod=rows_per_mod, emit_kv=not latent,
                                     kv_carry=kv_carry, rope_tabs=rope_tabs if latent else ())
    ya = _fourier(xcs.reshape(b, t, -1), fz.reshape(b, t, -1), dft_pos, p["w_four"], layer,
                  bg=b if latent else BG_CTX_FOURIER, tq=min(t, TQ_FOURIER))
    ur3 = ur.reshape(b, t, -1)
    ua3 = ua.reshape(b, t, -1)
    if latent:
        (yb,), s_fin = _retention(ur3, layer, p["dec"], p["gn"], nb=1, emit_state=False, state=state), None
        yc = _lat_attention(ua3, cache_k, cache_v, layer, p["sink"])
    else:
        yb, s_fin = _retention(ur3, layer, p["dec"], p["gn"], nb=NB_CTX_RET, emit_state=True,
                               state_carry=state_carry)
        yc = _ctx_attention(ua3, layer, p["sink"])
    out = _outproj(x2, ya.reshape(n, -1), yb.reshape(n, -1), yc.reshape(n, -1), layer, mod,
                   p["g_pre"], p["g_post"], p["w_in"], p["w_pa"], p["w_pb"], p["w_pc"], p["w_out"],
                   row0=row0, rows_per_mod=rows_per_mod)
    return out.reshape(b, t, D_MODEL), kv32, s_fin


def kernel(x_prompt, x_sample, cache_k, cache_v, state_ret, c, c_ctx, w_mod, b_mod, g_pre, g_post, w_in,
           w_four, ret_decay, ret_gn, attn_sink, w_branch_a, w_branch_b, w_branch_c, w_out):
    batch, seq, _ = x_prompt.shape
    dec_batch, dec_seq, _ = x_sample.shape
    past = cache_k.shape[2]
    assert 1 + dec_batch <= MOD_ROWS

    cv = jnp.zeros((MOD_ROWS, D_MODEL), F32).at[0].set(c_ctx).at[1:1 + dec_batch].set(c)
    mod = _modulation(cv, w_mod, b_mod)

    dft_chan, dft_ctx = _dft_tables(seq)
    _, dft_lat = _dft_tables(dec_seq)
    rope_tabs = _rope_tables(dec_seq)
    ck = cache_k.reshape(dec_batch, DEPTH, past, ATT_KV_W)
    cvv = cache_v.reshape(dec_batch, DEPTH, past, ATT_KV_W)

    p = dict(
        g_pre=g_pre, g_post=g_post,
        w_in=w_in, w_four=w_four, w_pa=w_branch_a, w_pb=w_branch_b, w_pc=w_branch_c, w_out=w_out,
        dec=ret_decay, gn=ret_gn, sink=attn_sink)

    xp = x_prompt
    kv_all, s_all = (), ()
    for l in range(DEPTH):
        xp, kv_all, s_all = _layer(xp, l, mod, p, latent=False, dft_chan=dft_chan, dft_pos=dft_ctx,
                                   kv_carry=tuple(kv_all), state_carry=s_all)
        s_all = (s_all,)

    xs = x_sample
    for l in range(DEPTH):
        xs, _, _ = _layer(xs, l, mod, p, latent=True, dft_chan=dft_chan, dft_pos=dft_lat,
                             rope_tabs=rope_tabs, cache_k=ck, cache_v=cvv, state=state_ret)

    new_k, new_v = (jnp.transpose(a, (0, 1, 4, 2, 3)) for a in kv_all)
    return (xp, xs, new_k, new_v, s_all[0])
```
